```python
import math
import jax, jax.numpy as jnp
from jax import lax
import numpy as np

D_MODEL = 1024
BATCH = 2
SEQ = 8192
DEPTH = 2

GRID_W = 64
CTX_LEN = 256
NORM_EPS = 1e-6

ATT_HEADS = 8
ATT_KV_HEADS = 2
HEAD_DIM = 64
ATT_WIDTH = ATT_HEADS * HEAD_DIM
KV_WIDTH = ATT_KV_HEADS * HEAD_DIM
ROPE_THETA = 10000.0
ROPE_FREQS = HEAD_DIM // 4
Q_BLOCK = 128

CONV_WIDTH = 512

POOL_WIDTH = 512
POOL_WINDOWS = (2, 4, 8, 16)
POOL_GROUP = POOL_WIDTH // 4

SSD_HEADS = 8
SSD_HEAD_DIM = 64
SSD_WIDTH = SSD_HEADS * SSD_HEAD_DIM
SSD_GROUPS = 2
SSD_STATE = 64
SSD_CHUNK = 128
SSD_CONV_CH = SSD_WIDTH + 2 * SSD_GROUPS * SSD_STATE
DT_MIN = 1e-3
DT_MAX = 1e-1

N_BRANCH = 4
BRANCH_WIDTH = 512

D_FF = 2816

IN_SPLITS = (ATT_WIDTH, KV_WIDTH, KV_WIDTH,
             CONV_WIDTH, CONV_WIDTH, CONV_WIDTH,
             POOL_WIDTH,
             SSD_WIDTH, SSD_CONV_CH, 2 * SSD_HEADS,
             N_BRANCH * D_MODEL)
IN_DIM = 768 + 1536 + 512 + 512 + 768 + 16 + 4096

kernel_name = "hybrid_parallel_gqa_conv_pool_ssd_adaln"

F32 = jnp.float32


def rms_norm(x, g):
    xf = x.astype(F32)
    y = xf * lax.rsqrt(jnp.mean(xf * xf, axis=-1, keepdims=True) + NORM_EPS)
    return (y * g.astype(F32)).astype(x.dtype)


def modulation(cond, w_mod, b_mod):
    m = jax.nn.silu(cond) @ w_mod + b_mod
    return jnp.split(m[..., None, :], 6, axis=-1)


def modulate(h, shift, scale):
    return h * (1 + scale) + shift


def conv3_centred(u, w):
    up = jnp.pad(u, ((0, 0), (1, 1), (0, 0)))
    return up[:, :-2] * w[0] + up[:, 1:-1] * w[1] + up[:, 2:] * w[2]


def split_in(proj):
    idx = [int(i) for i in np.cumsum(IN_SPLITS)[:-1]]
    return jnp.split(proj, idx, axis=-1)


def axial_rope_tables(n_tokens):
    rows = n_tokens // GRID_W
    row = jnp.repeat(jnp.arange(rows), GRID_W).astype(F32)
    col = jnp.tile(jnp.arange(GRID_W), rows).astype(F32)
    inv = ROPE_THETA ** (-jnp.arange(ROPE_FREQS, dtype=F32) / ROPE_FREQS)
    ang = jnp.stack([row[:, None] * inv, col[:, None] * inv], axis=1)
    return jnp.cos(ang), jnp.sin(ang)


def apply_axial_rope(t, cos, sin):
    b, n, h, d = t.shape
    tf = t.astype(F32).reshape(b, n, h, 2, 2, ROPE_FREQS)
    t1, t2 = tf[..., 0, :], tf[..., 1, :]
    cs, sn = cos[None, :, None], sin[None, :, None]
    out = jnp.stack([t1 * cs - t2 * sn, t1 * sn + t2 * cs], axis=-2)
    return out.reshape(b, n, h, d).astype(t.dtype)


def split_heads(t, n_heads):
    return t.reshape(t.shape[0], t.shape[1], n_heads, HEAD_DIM)


def gqa_attend(q, k, v):
    b, nq, h, dh = q.shape
    g = k.shape[2]
    qg = q.reshape(b, nq, g, h // g, dh)
    s = jnp.einsum('bqgrd,bkgd->bgrqk', qg, k).astype(F32) * (dh ** -0.5)
    p = jax.nn.softmax(s, axis=-1).astype(v.dtype)
    o = jnp.einsum('bgrqk,bkgd->bqgrd', p, v)
    return o.reshape(b, nq, h, dh)


def attention_branch(q, k, v, qc, kc, vc, q_norm, k_norm, with_ctx_out):
    b, n, _ = q.shape
    cos, sin = axial_rope_tables(n)
    q = apply_axial_rope(rms_norm(split_heads(q, ATT_HEADS), q_norm), cos, sin)
    k = apply_axial_rope(rms_norm(split_heads(k, ATT_KV_HEADS), k_norm), cos, sin)
    v = split_heads(v, ATT_KV_HEADS)
    kc = rms_norm(split_heads(kc, ATT_KV_HEADS), k_norm)
    vc = split_heads(vc, ATT_KV_HEADS)
    keys = jnp.concatenate([kc, k], axis=1)
    vals = jnp.concatenate([vc, v], axis=1)
    qb = q.reshape(b, n // Q_BLOCK, Q_BLOCK, ATT_HEADS, HEAD_DIM).transpose(1, 0, 2, 3, 4)
    out = lax.map(lambda blk: gqa_attend(blk, keys, vals), qb)
    out = out.transpose(1, 0, 2, 3, 4).reshape(b, n, ATT_WIDTH)
    out_c = None
    if with_ctx_out:
        qch = rms_norm(split_heads(qc, ATT_HEADS), q_norm)
        out_c = gqa_attend(qch, kc, vc).reshape(b, qc.shape[1], ATT_WIDTH)
    return out, out_c


def short_conv_branch(b_gate, c_gate, u, w_conv):
    return b_gate * conv3_centred(c_gate * u, w_conv)


def centred_pool_minus_self(u, window):
    n = u.shape[1]
    uf = u.astype(F32)
    cs = jnp.pad(jnp.cumsum(uf, axis=1), ((0, 0), (1, 0), (0, 0)))
    t = jnp.arange(n)
    lo = jnp.clip(t - window // 2, 0, n)
    hi = jnp.clip(t + window // 2, 0, n)
    mean = (cs[:, hi] - cs[:, lo]) / (hi - lo).astype(F32)[None, :, None]
    return (mean - uf).astype(u.dtype)


def pool_branch(u, w_pool, pool_scale):
    b, n, _ = u.shape
    groups = jnp.split(u, len(POOL_WINDOWS), axis=-1)
    pooled = jnp.stack([centred_pool_minus_self(g, w) for g, w in zip(groups, POOL_WINDOWS)], axis=-2)
    mixed = jnp.einsum('blgc,gcd->blgd', pooled, w_pool).reshape(b, n, POOL_WIDTH)
    return mixed * pool_scale


def segsum(a):
    t = a.shape[-1]
    rep = jnp.broadcast_to(a[..., :, None], a.shape + (t,))
    rep = jnp.where(jnp.tril(jnp.ones((t, t), bool), -1), rep, 0.0)
    s = jnp.cumsum(rep, axis=-2)
    return jnp.where(jnp.tril(jnp.ones((t, t), bool), 0), s, -jnp.inf)


def ssd_scan(x, dt, a_rate, bm, cm, h0):
    b, l, nh, p = x.shape
    g, n = bm.shape[2], bm.shape[3]
    r = nh // g
    nc, tt = l // SSD_CHUNK, SSD_CHUNK
    xdt = (x.astype(F32) * dt[..., None]).reshape(b, nc, tt, g, r, p)
    bc = bm.astype(F32).reshape(b, nc, tt, g, n)
    cc = cm.astype(F32).reshape(b, nc, tt, g, n)
    a = (dt * a_rate).reshape(b, nc, tt, g, r).transpose(0, 1, 3, 4, 2)
    a_cum = jnp.cumsum(a, axis=-1)
    cb = jnp.einsum('bclgn,bcsgn->bcgls', cc, bc)
    w = cb[:, :, :, None] * jnp.exp(segsum(a))
    y_diag = jnp.einsum('bcgrls,bcsgrp->bclgrp', w, xdt)
    decay_to_end = jnp.exp(a_cum[..., -1:] - a_cum)
    states = jnp.einsum('bcsgn,bcgrs,bcsgrp->bcgrpn', bc, decay_to_end, xdt)
    chunk_decay = jnp.exp(a_cum[..., -1])

    def step(h, inp):
        st, dec = inp
        return h * dec[..., None, None] + st, h

    h_final, h_prev = lax.scan(step, h0.reshape(b, g, r, p, n),
                               (jnp.moveaxis(states, 1, 0), jnp.moveaxis(chunk_decay, 1, 0)))
    h_prev = jnp.moveaxis(h_prev, 0, 1)
    y_off = jnp.einsum('bclgn,bcgrpn,bcgrl->bclgrp', cc, h_prev, jnp.exp(a_cum))
    y = (y_diag + y_off).reshape(b, l, nh, p)
    return y.astype(x.dtype), h_final.reshape(b, nh, p, n)


def ssd_prep(xbc, dt_raw, conv_w, conv_b, dt_bias):
    b, l, _ = xbc.shape
    xbc = jax.nn.silu(conv3_centred(xbc, conv_w) + conv_b)
    xs, bm, cm = jnp.split(xbc, [SSD_WIDTH, SSD_WIDTH + SSD_GROUPS * SSD_STATE], axis=-1)
    xs = xs.reshape(b, l, SSD_HEADS, SSD_HEAD_DIM)
    bm = bm.reshape(b, l, SSD_GROUPS, SSD_STATE)
    cm = cm.reshape(b, l, SSD_GROUPS, SSD_STATE)
    dt = jax.nn.softplus(dt_raw.astype(F32).reshape(b, l, 2, SSD_HEADS) + dt_bias.astype(F32))
    return xs, bm, cm, dt


def ssd_branch(z, xbc, dt_raw, zc, xbcc, dtc_raw, conv_w, conv_b, dt_bias, a_log, d_skip, norm_g,
               with_ctx_out):
    xs, bm, cm, dt = ssd_prep(xbc, dt_raw, conv_w, conv_b, dt_bias)
    xs_c, bm_c, cm_c, dt_c = ssd_prep(xbcc, dtc_raw, conv_w, conv_b, dt_bias)
    a_rate = -jnp.exp(a_log.astype(F32))
    b = xs.shape[0]
    y = d_skip[:, None] * xs
    y_c = d_skip[:, None] * xs_c
    for d, flip in enumerate((False, True)):
        f = (lambda t: jnp.flip(t, axis=1)) if flip else (lambda t: t)
        h0 = jnp.zeros((b, SSD_HEADS, SSD_HEAD_DIM, SSD_STATE), F32)
        yd_c, h_ctx = ssd_scan(f(xs_c), f(dt_c[:, :, d]), a_rate[d], f(bm_c), f(cm_c), h0)
        yd, _ = ssd_scan(f(xs), f(dt[:, :, d]), a_rate[d], f(bm), f(cm), h_ctx)
        y = y + f(yd)
        y_c = y_c + f(yd_c)
    out = rms_norm(y.reshape(z.shape) * jax.nn.silu(z), norm_g)
    out_c = rms_norm(y_c.reshape(zc.shape) * jax.nn.silu(zc), norm_g) if with_ctx_out else None
    return out, out_c


def merge_branches(branches, gate_logits, w_branch):
    b, n, _ = gate_logits.shape
    gates = jax.nn.sigmoid(gate_logits.astype(F32)).astype(gate_logits.dtype).reshape(b, n, N_BRANCH, D_MODEL)
    proj = jnp.einsum('blkc,kcd->blkd', jnp.stack(branches, axis=-2), w_branch)
    return jnp.sum(gates * proj, axis=-2)


def mixing_sublayer(h, hc, w_in, q_norm, k_norm, conv_short, pool_w, pool_scale,
                    ssd_conv_w, ssd_conv_b, ssd_dt_bias, ssd_a_log, ssd_d, ssd_norm,
                    w_branch, w_out, with_ctx_out):
    q, k, v, bg, cg, u, pu, z, xbc, dt, gl = split_in(h @ w_in)
    qc, kc, vc, bgc, cgc, uc, puc, zc, xbcc, dtc, glc = split_in(hc @ w_in)
    att, att_c = attention_branch(q, k, v, qc, kc, vc, q_norm, k_norm, with_ctx_out)
    ssd, ssd_c = ssd_branch(z, xbc, dt, zc, xbcc, dtc, ssd_conv_w, ssd_conv_b, ssd_dt_bias,
                            ssd_a_log, ssd_d, ssd_norm, with_ctx_out)
    y = merge_branches((att, short_conv_branch(bg, cg, u, conv_short),
                        pool_branch(pu, pool_w, pool_scale), ssd), gl, w_branch) @ w_out
    if not with_ctx_out:
        return y, None
    yc = merge_branches((att_c, short_conv_branch(bgc, cgc, uc, conv_short),
                         pool_branch(puc, pool_w, pool_scale), ssd_c), glc, w_branch) @ w_out
    return y, yc


def conv_ffn(h, w_up, w_conv, w_down):
    up = conv3_centred(h @ w_up, w_conv)
    g, v = jnp.split(up, 2, axis=-1)
    return (jax.nn.silu(g) * v) @ w_down


def setup_inputs(seed: int = 0) -> dict:
    key = jax.random.key(seed)
    ks = iter(jax.random.split(key, 40))

    def nrm(shape, scale):
        return jax.random.normal(next(ks), shape, F32) * scale

    def gain(shape):
        return 1.0 + 0.02 * jax.random.normal(next(ks), shape, F32)

    u = jax.random.uniform(next(ks), (DEPTH, 2, SSD_HEADS), F32)
    dt0 = jnp.exp(u * (math.log(DT_MAX) - math.log(DT_MIN)) + math.log(DT_MIN))
    ssd_dt_bias = dt0 + jnp.log(-jnp.expm1(-dt0))
    ssd_a_log = jnp.log(jax.random.uniform(next(ks), (DEPTH, 2, SSD_HEADS), F32, 1.0, 16.0))
    return {
        "x": nrm((BATCH, SEQ, D_MODEL), 1.0),
        "c": nrm((BATCH, D_MODEL), 1.0),
        "ctx": nrm((BATCH, CTX_LEN, D_MODEL), 1.0),
        "c_ctx": nrm((D_MODEL,), 1.0),
        "w_mod": nrm((DEPTH, D_MODEL, 6 * D_MODEL), 0.5 * D_MODEL ** -0.5),
        "b_mod": nrm((DEPTH, 6 * D_MODEL), 0.02),
        "norm_mix": gain((DEPTH, D_MODEL)),
        "w_in": nrm((DEPTH, D_MODEL, IN_DIM), D_MODEL ** -0.5),
        "q_norm": gain((DEPTH, HEAD_DIM)),
        "k_norm": gain((DEPTH, HEAD_DIM)),
        "conv_short": nrm((DEPTH, 3, CONV_WIDTH), 3 ** -0.5),
        "pool_w": nrm((DEPTH, len(POOL_WINDOWS), POOL_GROUP, POOL_GROUP), POOL_GROUP ** -0.5),
        "pool_scale": 1.0 + 0.1 * jax.random.normal(next(ks), (DEPTH, POOL_WIDTH), F32),
        "ssd_conv_w": nrm((DEPTH, 3, SSD_CONV_CH), 3 ** -0.5),
        "ssd_conv_b": nrm((DEPTH, SSD_CONV_CH), 0.02),
        "ssd_dt_bias": ssd_dt_bias,
        "ssd_a_log": ssd_a_log,
        "ssd_d": gain((DEPTH, SSD_HEADS)),
        "ssd_norm": gain((DEPTH, SSD_WIDTH)),
        "w_branch": nrm((DEPTH, N_BRANCH, BRANCH_WIDTH, D_MODEL), BRANCH_WIDTH ** -0.5),
        "w_out": nrm((DEPTH, D_MODEL, D_MODEL), D_MODEL ** -0.5),
        "norm_ffn": gain((DEPTH, D_MODEL)),
        "w_up": nrm((DEPTH, D_MODEL, 2 * D_FF), D_MODEL ** -0.5),
        "ffn_conv": nrm((DEPTH, 3, 2 * D_FF), 3 ** -0.5),
        "w_down": nrm((DEPTH, D_FF, D_MODEL), D_FF ** -0.5),
        "final_norm": gain((D_MODEL,)),
    }


def reference(x, c, ctx, c_ctx, w_mod, b_mod, norm_mix, w_in, q_norm, k_norm, conv_short,
              pool_w, pool_scale, ssd_conv_w, ssd_conv_b, ssd_dt_bias, ssd_a_log, ssd_d,
              ssd_norm, w_branch, w_out, norm_ffn, w_up, ffn_conv, w_down, final_norm):
    xc = ctx
    for i in range(DEPTH):
        with_ctx_out = i < DEPTH - 1
        sh1, sc1, g1, sh2, sc2, g2 = modulation(c, w_mod[i], b_mod[i])
        csh1, csc1, cg1, csh2, csc2, cg2 = modulation(c_ctx, w_mod[i], b_mod[i])
        h = modulate(rms_norm(x, norm_mix[i]), sh1, sc1)
        hc = modulate(rms_norm(xc, norm_mix[i]), csh1, csc1)
        y, yc = mixing_sublayer(h, hc, w_in[i], q_norm[i], k_norm[i], conv_short[i], pool_w[i],
                                pool_scale[i], ssd_conv_w[i], ssd_conv_b[i], ssd_dt_bias[i],
                                ssd_a_log[i], ssd_d[i], ssd_norm[i], w_branch[i], w_out[i],
                                with_ctx_out)
        x = x + g1 * y
        h = modulate(rms_norm(x, norm_ffn[i]), sh2, sc2)
        x = x + g2 * conv_ffn(h, w_up[i], ffn_conv[i], w_down[i])
        if with_ctx_out:
            xc = xc + cg1 * yc
            hc = modulate(rms_norm(xc, norm_ffn[i]), csh2, csc2)
            xc = xc + cg2 * conv_ffn(hc, w_up[i], ffn_conv[i], w_down[i])
    return rms_norm(x, final_norm)
```

```python
import functools
import math

import jax
import jax.numpy as jnp
import numpy as np
from jax import lax
from jax.experimental import pallas as pl
from jax.experimental.pallas import tpu as pltpu

F32 = jnp.float32
BF16 = jnp.bfloat16

D_MODEL = 1024
GRID_W = 64
NORM_EPS = 1e-6
ATT_HEADS = 8
ATT_KV_HEADS = 2
HEAD_DIM = 64
ROPE_THETA = 10000.0
ROPE_FREQS = HEAD_DIM // 4
POOL_WINDOWS = (2, 4, 8, 16)
SSD_HEADS = 8
SSD_STATE = 64
SSD_CHUNK = 128
N_BRANCH = 4
BRANCH_WIDTH = 512
D_FF = 2816

C_GL, C_BG, C_CG, C_U, C_PU, C_Z, C_Q, C_XS, C_BC, C_K, C_V, C_DT = (
    0, 4096, 4608, 5120, 5632, 6144, 6656, 7168, 7680, 7936, 8064, 8192)
N_CAT = 8448
DT_PAD = 256

HALO = 16
LANES = 128
QK_SCALE = HEAD_DIM ** -0.5 * math.log2(math.e)
VMEM_LIMIT = 56 * 1024 * 1024


def _cparams(sem):
    return pltpu.CompilerParams(dimension_semantics=sem, vmem_limit_bytes=VMEM_LIMIT)


def _silu(x):
    return x * jax.nn.sigmoid(x)


def _split_dot(a, b_bf16):
    hi = a.astype(BF16)
    lo = (a - hi.astype(F32)).astype(BF16)
    return (jnp.dot(hi, b_bf16, preferred_element_type=F32)
            + jnp.dot(lo, b_bf16, preferred_element_type=F32))


def _dot_split_lhs(a_bf16, b):
    hi = b.astype(BF16)
    lo = (b - hi.astype(F32)).astype(BF16)
    return (jnp.dot(a_bf16, hi, preferred_element_type=F32)
            + jnp.dot(a_bf16, lo, preferred_element_type=F32))


def _rms_mod(x, g, scale, shift):
    ms = jnp.mean(x * x, axis=-1, keepdims=True)
    y = x * lax.rsqrt(ms + NORM_EPS) * g
    return y * (1.0 + scale) + shift


def _mod_body(c_ref, w_ref, b_ref, o_ref):
    s = _silu(c_ref[...])
    o_ref[...] = jnp.dot(s, w_ref[...], precision=lax.Precision.HIGHEST,
                         preferred_element_type=F32) + b_ref[...]


def _modulation(cond8, w_mod, b_mod):
    n = w_mod.shape[1]
    tn = 1536
    return pl.pallas_call(
        _mod_body, grid=(n // tn,),
        in_specs=[pl.BlockSpec((8, D_MODEL), lambda j: (0, 0)),
                  pl.BlockSpec((D_MODEL, tn), lambda j: (0, j)),
                  pl.BlockSpec((1, tn), lambda j: (0, j))],
        out_specs=pl.BlockSpec((8, tn), lambda j: (0, j)),
        out_shape=jax.ShapeDtypeStruct((8, n), F32),
        compiler_params=_cparams(("arbitrary",)), name="modulation",
    )(cond8, w_mod, b_mod.reshape(1, n))


def _inproj_body(x_ref, mod_ref, g_ref, w_ref, o_ref, h_ref):
    @pl.when(pl.program_id(2) == 0)
    def _():
        h = _rms_mod(x_ref[0], g_ref[...], mod_ref[0, 1:2, :], mod_ref[0, 0:1, :])
        h_ref[...] = h.astype(BF16)

    o_ref[0] = jnp.dot(h_ref[...], w_ref[...], preferred_element_type=F32).astype(BF16)


def _inproj(x, mod, g, w_cat, tm):
    b, l, _ = x.shape
    n = w_cat.shape[1]
    tn = 768
    per_batch = mod.shape[0] > 1
    return pl.pallas_call(
        _inproj_body, grid=(b, l // tm, n // tn),
        in_specs=[pl.BlockSpec((1, tm, D_MODEL), lambda bi, i, j: (bi, i, 0)),
                  pl.BlockSpec((1, 6, D_MODEL), lambda bi, i, j: (bi if per_batch else 0, 0, 0)),
                  pl.BlockSpec((1, D_MODEL), lambda bi, i, j: (0, 0)),
                  pl.BlockSpec((D_MODEL, tn), lambda bi, i, j: (0, j))],
        out_specs=pl.BlockSpec((1, tm, tn), lambda bi, i, j: (bi, i, j)),
        out_shape=jax.ShapeDtypeStruct((b, l, n), BF16),
        scratch_shapes=[pltpu.VMEM((tm, D_MODEL), BF16)],
        compiler_params=_cparams(("parallel", "parallel", "arbitrary")), name="inproj",
    )(x, mod, g.reshape(1, D_MODEL), w_cat)


def _head_norm_rope(t, cos, sin, gain, bd):
    ss = _split_dot(t * t, bd)
    y = t * lax.rsqrt(ss * (1.0 / HEAD_DIM) + NORM_EPS) * gain
    lane = lax.broadcasted_iota(jnp.int32, y.shape, 1)
    partner = jnp.where((lane % 32) < 16, pltpu.roll(y, LANES - 16, 1), pltpu.roll(y, 16, 1))
    return y * cos + partner * sin


def _qprep_body(q_ref, cos_ref, sin_ref, g_ref, bd_ref, o_ref):
    cos, sin, gain, bd = cos_ref[...], sin_ref[...], g_ref[...], bd_ref[...]
    lane = lax.broadcasted_iota(jnp.int32, cos.shape, 1)
    for c in range(ATT_HEADS // 2):
        t = q_ref[0, :, c * LANES:(c + 1) * LANES].astype(F32)
        r = _head_norm_rope(t, cos, sin, gain, bd)
        o_ref[0, 2 * c] = jnp.where(lane < HEAD_DIM, r, 0.0).astype(BF16)
        o_ref[0, 2 * c + 1] = jnp.where(lane < HEAD_DIM, pltpu.roll(r, HEAD_DIM, 1), 0.0).astype(BF16)


def _qprep(proj, cos, sin, gain, bd, tm):
    b, l, _ = proj.shape
    return pl.pallas_call(
        _qprep_body, grid=(b, l // tm),
        in_specs=[pl.BlockSpec((1, tm, 512), lambda bi, i: (bi, i, C_Q // 512)),
                  pl.BlockSpec((tm, LANES), lambda bi, i: (i, 0)),
                  pl.BlockSpec((tm, LANES), lambda bi, i: (i, 0)),
                  pl.BlockSpec((1, LANES), lambda bi, i: (0, 0)),
                  pl.BlockSpec((LANES, LANES), lambda bi, i: (0, 0))],
        out_specs=pl.BlockSpec((1, ATT_HEADS, tm, LANES), lambda bi, i: (bi, 0, i, 0)),
        out_shape=jax.ShapeDtypeStruct((b, ATT_HEADS, l, LANES), BF16),
        compiler_params=_cparams(("parallel", "parallel")), name="q_prep",
    )(proj, cos, sin, gain, bd)


def _kvprep_body(kl_ref, vl_ref, kc_ref, vc_ref, cos_ref, sin_ref, g_ref, bd_ref, kt_ref, v1_ref, *, nt_lat):
    is_ctx = pl.program_id(1) == nt_lat
    k = jnp.where(is_ctx, kc_ref[0], kl_ref[0]).astype(F32)
    v = jnp.where(is_ctx, vc_ref[0], vl_ref[0]).astype(F32)
    r = _head_norm_rope(k, cos_ref[...], sin_ref[...], g_ref[...], bd_ref[...])
    lane = lax.broadcasted_iota(jnp.int32, r.shape, 1)
    ones_col = jnp.where(lane == HEAD_DIM, 1.0, 0.0)
    for g in range(ATT_KV_HEADS):
        rg = r if g == 0 else pltpu.roll(r, HEAD_DIM, 1)
        vg = v if g == 0 else pltpu.roll(v, HEAD_DIM, 1)
        kt_ref[0, g] = jnp.where(lane < HEAD_DIM, rg, 0.0).T.astype(BF16)
        v1_ref[0, g] = jnp.where(lane < HEAD_DIM, vg, ones_col).astype(BF16)


def _kvprep(proj, proj_c, cos, sin, gain, bd):
    b, l, _ = proj.shape
    lc = proj_c.shape[1]
    tm = lc
    nt_lat = l // tm
    ltot = l + lc
    lat = lambda bi, t: (bi, jnp.minimum(t, nt_lat - 1))
    return pl.pallas_call(
        functools.partial(_kvprep_body, nt_lat=nt_lat), grid=(b, nt_lat + 1),
        in_specs=[pl.BlockSpec((1, tm, LANES), lambda bi, t: lat(bi, t) + (C_K // LANES,)),
                  pl.BlockSpec((1, tm, LANES), lambda bi, t: lat(bi, t) + (C_V // LANES,)),
                  pl.BlockSpec((1, tm, LANES), lambda bi, t: (bi, 0, C_K // LANES)),
                  pl.BlockSpec((1, tm, LANES), lambda bi, t: (bi, 0, C_V // LANES)),
                  pl.BlockSpec((tm, LANES), lambda bi, t: (t, 0)),
                  pl.BlockSpec((tm, LANES), lambda bi, t: (t, 0)),
                  pl.BlockSpec((1, LANES), lambda bi, t: (0, 0)),
                  pl.BlockSpec((LANES, LANES), lambda bi, t: (0, 0))],
        out_specs=[pl.BlockSpec((1, ATT_KV_HEADS, LANES, tm), lambda bi, t: (bi, 0, 0, t)),
                   pl.BlockSpec((1, ATT_KV_HEADS, tm, LANES), lambda bi, t: (bi, 0, t, 0))],
        out_shape=[jax.ShapeDtypeStruct((b, ATT_KV_HEADS, LANES, ltot), BF16),
                   jax.ShapeDtypeStruct((b, ATT_KV_HEADS, ltot, LANES), BF16)],
        compiler_params=_cparams(("parallel", "parallel")), name="kv_prep",
    )(proj, proj, proj_c, proj_c, cos, sin, gain, bd)


def _flash_body(q_ref, kt_ref, v_ref, o_ref, acc_ref, m_ref, *, tq, tk, nk):
    rep = ATT_HEADS // ATT_KV_HEADS
    q = q_ref[0].reshape(rep * tq, LANES)
    m_ref[...] = jnp.full(m_ref.shape, -jnp.inf, F32)
    acc_ref[...] = jnp.zeros(acc_ref.shape, F32)

    def step(c, carry):
        off = pl.multiple_of(c * tk, tk)
        s = jnp.dot(q, kt_ref[0, 0, :, pl.ds(off, tk)], preferred_element_type=F32)
        m_prev = m_ref[...]
        m_new = jnp.maximum(m_prev, jnp.max(s, axis=1, keepdims=True))
        alpha = jnp.exp2(m_prev - m_new)
        p = jnp.exp2(s - jnp.tile(m_new, (1, tk // LANES))).astype(BF16)
        acc_ref[...] = acc_ref[...] * alpha + jnp.dot(p, v_ref[0, 0, pl.ds(off, tk), :],
                                                      preferred_element_type=F32)
        m_ref[...] = m_new
        return carry

    lax.fori_loop(0, nk, step, 0)
    acc = acc_ref[...]
    o = acc / acc[:, HEAD_DIM:HEAD_DIM + 1]
    lane = lax.broadcasted_iota(jnp.int32, (tq, LANES), 1)
    for j in range(rep // 2):
        a = o[(2 * j) * tq:(2 * j + 1) * tq]
        bb = o[(2 * j + 1) * tq:(2 * j + 2) * tq]
        o_ref[0, :, j * LANES:(j + 1) * LANES] = jnp.where(
            lane < HEAD_DIM, a, pltpu.roll(bb, HEAD_DIM, 1)).astype(BF16)


def _flash(qh, kt, v1, lk, k_off, tq, tk):
    b, _, l, _ = qh.shape
    rep = ATT_HEADS // ATT_KV_HEADS
    kb = k_off // lk
    return pl.pallas_call(
        functools.partial(_flash_body, tq=tq, tk=tk, nk=lk // tk),
        grid=(b, ATT_KV_HEADS, l // tq),
        in_specs=[pl.BlockSpec((1, rep, tq, LANES), lambda bi, g, i: (bi, g, i, 0)),
                  pl.BlockSpec((1, 1, LANES, lk), lambda bi, g, i: (bi, g, 0, kb)),
                  pl.BlockSpec((1, 1, lk, LANES), lambda bi, g, i: (bi, g, kb, 0))],
        out_specs=pl.BlockSpec((1, tq, rep * HEAD_DIM), lambda bi, g, i: (bi, i, g)),
        out_shape=jax.ShapeDtypeStruct((b, l, ATT_HEADS * HEAD_DIM), BF16),
        scratch_shapes=[pltpu.VMEM((rep * tq, LANES), F32), pltpu.VMEM((rep * tq, LANES), F32)],
        compiler_params=_cparams(("parallel", "parallel", "arbitrary")), name="flash_attention",
    )(qh, kt, v1)


def _halo_specs(tm, l, width, col_block, row_off_blocks=0):
    r = tm // HALO
    last = l // HALO - 1
    main = pl.BlockSpec((1, tm, width), lambda bi, i: (bi, i + row_off_blocks, col_block))
    prev = pl.BlockSpec((1, HALO, width), lambda bi, i: (bi, jnp.maximum(i * r - 1, 0), col_block))
    nxt = pl.BlockSpec((1, HALO, width), lambda bi, i: (bi, jnp.minimum((i + 1) * r, last), col_block))
    return [main, prev, nxt]


def _ext_rows(main_ref, prev_ref, next_ref, valid):
    e = jnp.concatenate([prev_ref[0].astype(F32), main_ref[0].astype(F32), next_ref[0].astype(F32)], axis=0)
    return jnp.where(valid, e, 0.0)


def _valid_rows(i, tm, l, width):
    pos = i * tm - HALO + lax.broadcasted_iota(jnp.int32, (tm + 2 * HALO, width), 0)
    return pos, (pos >= 0) & (pos < l)


def _conv3_rows(e, w):
    n = e.shape[0]
    return w[0:1] * pltpu.roll(e, 1, 0) + w[1:2] * e + w[2:3] * pltpu.roll(e, n - 1, 0)


def _ssdprep_body(xs_ref, xsp_ref, xsn_ref, bc_ref, bcp_ref, bcn_ref, dt_ref,
                  xsc_ref, bcc_ref, dtc_ref, w_ref, b_ref, dtb_ref, ex_ref,
                  xo_ref, bco_ref, dtx_ref, *, nt_lat, tm, l):
    t = pl.program_id(1)
    is_ctx = t == nt_lat

    def conv_silu(main_ref, prev_ref, next_ref, ctx_ref, lo, hi):
        width = hi - lo
        pos, valid = _valid_rows(t, tm, l, width)
        e_lat = _ext_rows(main_ref, prev_ref, next_ref, valid)
        zeros = jnp.zeros((HALO, width), F32)
        e_ctx = jnp.concatenate([zeros, ctx_ref[0].astype(F32), zeros], axis=0)
        e = jnp.where(is_ctx, e_ctx, e_lat)
        y = _conv3_rows(e, w_ref[:, lo:hi])[HALO:HALO + tm] + b_ref[:, lo:hi]
        return _silu(y)

    xo_ref[0] = conv_silu(xs_ref, xsp_ref, xsn_ref, xsc_ref, 0, 512).astype(BF16)
    bco_ref[0] = conv_silu(bc_ref, bcp_ref, bcn_ref, bcc_ref, 512, 768).astype(BF16)
    raw = jnp.where(is_ctx, dtc_ref[0], dt_ref[0]).astype(F32) + dtb_ref[...]
    dt = jnp.maximum(raw, 0.0) + jnp.log1p(jnp.exp(-jnp.abs(raw)))
    for d in range(2):
        dtx_ref[0, d] = _split_dot(dt, ex_ref[d])


def _ssdprep(proj, proj_c, conv_w, conv_b, dt_bias_pad, expand):
    b, l, _ = proj.shape
    lc = proj_c.shape[1]
    tm = lc
    nt_lat = l // tm
    ltot = l + lc
    r = tm // HALO
    last = l // HALO - 1
    cl = lambda t: jnp.minimum(t, nt_lat - 1)

    def lat_specs(width, col):
        return [pl.BlockSpec((1, tm, width), lambda bi, t: (bi, cl(t), col)),
                pl.BlockSpec((1, HALO, width), lambda bi, t: (bi, jnp.maximum(cl(t) * r - 1, 0), col)),
                pl.BlockSpec((1, HALO, width), lambda bi, t: (bi, jnp.minimum((cl(t) + 1) * r, last), col))]

    return pl.pallas_call(
        functools.partial(_ssdprep_body, nt_lat=nt_lat, tm=tm, l=l), grid=(b, nt_lat + 1),
        in_specs=(lat_specs(512, C_XS // 512) + lat_specs(256, C_BC // 256)
                  + [pl.BlockSpec((1, tm, DT_PAD), lambda bi, t: (bi, cl(t), C_DT // DT_PAD)),
                     pl.BlockSpec((1, tm, 512), lambda bi, t: (bi, 0, C_XS // 512)),
                     pl.BlockSpec((1, tm, 256), lambda bi, t: (bi, 0, C_BC // 256)),
                     pl.BlockSpec((1, tm, DT_PAD), lambda bi, t: (bi, 0, C_DT // DT_PAD)),
                     pl.BlockSpec((3, 768), lambda bi, t: (0, 0)),
                     pl.BlockSpec((1, 768), lambda bi, t: (0, 0)),
                     pl.BlockSpec((1, DT_PAD), lambda bi, t: (0, 0)),
                     pl.BlockSpec((2, DT_PAD, 512), lambda bi, t: (0, 0, 0))]),
        out_specs=[pl.BlockSpec((1, tm, 512), lambda bi, t: (bi, t, 0)),
                   pl.BlockSpec((1, tm, 256), lambda bi, t: (bi, t, 0)),
                   pl.BlockSpec((1, 2, tm, 512), lambda bi, t: (bi, 0, t, 0))],
        out_shape=[jax.ShapeDtypeStruct((b, ltot, 512), BF16),
                   jax.ShapeDtypeStruct((b, ltot, 256), BF16),
                   jax.ShapeDtypeStruct((b, 2, ltot, 512), F32)],
        compiler_params=_cparams(("parallel", "parallel")), name="ssd_prep",
    )(proj, proj, proj, proj, proj, proj, proj, proj_c, proj_c, proj_c,
      conv_w, conv_b, dt_bias_pad, expand)


def _ssd_body(xs_ref, bc_ref, dtx_ref, ar_ref, tri_ref, neg_ref, hm_ref, y_ref, ht_ref):
    t = SSD_CHUNK
    fwd = pl.program_id(1) == 0

    @pl.when(pl.program_id(2) == 0)
    def _():
        ht_ref[...] = jnp.zeros(ht_ref.shape, F32)

    xs = xs_ref[0].astype(F32)
    dtx = dtx_ref[0, 0]
    cum = _dot_split_lhs(tri_ref[0], dtx * ar_ref[0])
    tot = jnp.where(fwd, cum[t - 1:t, :], cum[0:1, :])
    xdt = xs * dtx
    xdt_b = xdt.astype(BF16)
    bc = bc_ref[0]
    b_all, c_all = bc[:, :LANES], bc[:, LANES:]
    neg = neg_ref[0]
    ht = ht_ref[...]
    lane = lax.broadcasted_iota(jnp.int32, (t, LANES), 1)
    low = lane < HEAD_DIM

    y_off = jnp.dot(c_all, ht.astype(BF16), preferred_element_type=F32) * jnp.exp(cum)
    cb = []
    for g in range(2):
        cg = jnp.where(low if g == 0 else ~low, c_all, jnp.zeros_like(c_all))
        cb.append(lax.dot_general(cg, b_all, (((1,), (1,)), ((), ())), preferred_element_type=F32))
    ys = []
    for j in range(SSD_HEADS // 2):
        cp = cum[:, j * LANES:(j + 1) * LANES]
        cps = pltpu.roll(cp, HEAD_DIM, 1)
        cpt = cp.T
        xp = xdt_b[:, j * LANES:(j + 1) * LANES]
        acc = None
        for e in range(2):
            col = jnp.where(low, cp, cps) if e == 0 else jnp.where(low, cps, cp)
            row = jnp.broadcast_to(cpt[e * HEAD_DIM:e * HEAD_DIM + 1, :], (t, t))
            w = (cb[j // 2] * jnp.exp(col - row + neg)).astype(BF16)
            xm = jnp.where(low if e == 0 else ~low, xp, jnp.zeros_like(xp))
            part = jnp.dot(w, xm, preferred_element_type=F32)
            acc = part if acc is None else acc + part
        ys.append(acc)
    y_ref[0, 0] = (y_off + jnp.concatenate(ys, axis=1)).astype(BF16)

    bt = b_all.astype(F32).T.astype(BF16)
    st = jnp.dot(bt, (xdt * jnp.exp(tot - cum)).astype(BF16), preferred_element_type=F32)
    ht_ref[...] = ht * jnp.exp(tot) + st * hm_ref[...]


def _ssd_scan(xs, bc, dtx, a_rate_x, tri, neg, hmask, l_lat):
    b, ltot, _ = xs.shape
    t = SSD_CHUNK
    nc = ltot // t
    nl = l_lat // t
    nct = nc - nl

    def cid(d, c):
        f = jnp.where(c < nct, nl + c, c - nct)
        r = jnp.where(c < nct, nl + nct - 1 - c, nl - 1 - (c - nct))
        return jnp.where(d == 0, f, r)

    return pl.pallas_call(
        _ssd_body, grid=(b, 2, nc),
        in_specs=[pl.BlockSpec((1, t, 512), lambda bi, d, c: (bi, cid(d, c), 0)),
                  pl.BlockSpec((1, t, 256), lambda bi, d, c: (bi, cid(d, c), 0)),
                  pl.BlockSpec((1, 1, t, 512), lambda bi, d, c: (bi, d, cid(d, c), 0)),
                  pl.BlockSpec((1, 1, 512), lambda bi, d, c: (d, 0, 0)),
                  pl.BlockSpec((1, t, t), lambda bi, d, c: (d, 0, 0)),
                  pl.BlockSpec((1, t, t), lambda bi, d, c: (d, 0, 0)),
                  pl.BlockSpec((LANES, 512), lambda bi, d, c: (0, 0))],
        out_specs=pl.BlockSpec((1, 1, t, 512), lambda bi, d, c: (bi, d, cid(d, c), 0)),
        out_shape=jax.ShapeDtypeStruct((b, 2, ltot, 512), BF16),
        scratch_shapes=[pltpu.VMEM((LANES, 512), F32)],
        compiler_params=_cparams(("parallel", "parallel", "arbitrary")), name="ssd_scan",
    )(xs, bc, dtx, a_rate_x, tri, neg, hmask)


def _merge_body(gl_ref, bg_ref, cg_ref, cgp_ref, cgn_ref, u_ref, up_ref, un_ref,
                pu_ref, pup_ref, pun_ref, z_ref, att_ref, ys_ref, xs_ref, x_ref, mod_ref,
                cw_ref, pw_ref, ps_ref, dsk_ref, sn_ref, wb_ref, wo_ref, o_ref, *, tm, l):
    i = pl.program_id(1)
    pos, valid = _valid_rows(i, tm, l, 512)
    main = slice(HALO, HALO + tm)
    n_ext = tm + 2 * HALO

    v = _ext_rows(cg_ref, cgp_ref, cgn_ref, valid) * _ext_rows(u_ref, up_ref, un_ref, valid)
    br_conv = bg_ref[0].astype(F32) * _conv3_rows(v, cw_ref[...])[main]

    pe = _ext_rows(pu_ref, pup_ref, pun_ref, valid)
    posm = pos[main, :LANES]
    pooled = []
    for g, win in enumerate(POOL_WINDOWS):
        e = pe[:, g * LANES:(g + 1) * LANES]
        s = e + pltpu.roll(e, 1, 0)
        span = 2
        while span < win:
            s = pltpu.roll(s, span // 2, 0) + pltpu.roll(s, n_ext - span // 2, 0)
            span *= 2
        cnt = (jnp.minimum(posm + win // 2, l) - jnp.maximum(posm - win // 2, 0)).astype(F32)
        pm = (s[main] / cnt - e[main]).astype(BF16)
        pooled.append(jnp.dot(pm, pw_ref[g], preferred_element_type=F32))
    br_pool = jnp.concatenate(pooled, axis=1) * ps_ref[...]

    y = ys_ref[0, 0].astype(F32) + ys_ref[0, 1].astype(F32) + dsk_ref[...] * xs_ref[0].astype(F32)
    gz = y * _silu(z_ref[0].astype(F32))
    br_ssd = gz * lax.rsqrt(jnp.mean(gz * gz, axis=-1, keepdims=True) + NORM_EPS) * sn_ref[...]

    branches = (att_ref[0], br_conv.astype(BF16), br_pool.astype(BF16), br_ssd.astype(BF16))
    acc = None
    for k, br in enumerate(branches):
        gate = jax.nn.sigmoid(gl_ref[0, :, k * D_MODEL:(k + 1) * D_MODEL].astype(F32))
        term = gate * jnp.dot(br, wb_ref[k], preferred_element_type=F32)
        acc = term if acc is None else acc + term
    yo = jnp.dot(acc.astype(BF16), wo_ref[...], preferred_element_type=F32)
    o_ref[0] = x_ref[0] + mod_ref[0, 2:3, :] * yo


def _merge(proj, att, ys, xs_c, x, mod, conv_short, pool_w, pool_scale, dskip_x, ssd_norm,
           w_branch, w_out, tm, ssd_row_off):
    b, l, _ = x.shape
    per_batch = mod.shape[0] > 1
    off = ssd_row_off // tm
    const2 = lambda bi, i: (0, 0)
    const3 = lambda bi, i: (0, 0, 0)
    in_specs = (
        [pl.BlockSpec((1, tm, 4096), lambda bi, i: (bi, i, 0)),
         pl.BlockSpec((1, tm, 512), lambda bi, i: (bi, i, C_BG // 512))]
        + _halo_specs(tm, l, 512, C_CG // 512) + _halo_specs(tm, l, 512, C_U // 512)
        + _halo_specs(tm, l, 512, C_PU // 512)
        + [pl.BlockSpec((1, tm, 512), lambda bi, i: (bi, i, C_Z // 512)),
           pl.BlockSpec((1, tm, 512), lambda bi, i: (bi, i, 0)),
           pl.BlockSpec((1, 2, tm, 512), lambda bi, i: (bi, 0, i + off, 0)),
           pl.BlockSpec((1, tm, 512), lambda bi, i: (bi, i + off, 0)),
           pl.BlockSpec((1, tm, D_MODEL), lambda bi, i: (bi, i, 0)),
           pl.BlockSpec((1, 6, D_MODEL), lambda bi, i: (bi if per_batch else 0, 0, 0)),
           pl.BlockSpec((3, 512), const2),
           pl.BlockSpec((4, LANES, LANES), const3),
           pl.BlockSpec((1, 512), const2),
           pl.BlockSpec((1, 512), const2),
           pl.BlockSpec((1, 512), const2),
           pl.BlockSpec((N_BRANCH, BRANCH_WIDTH, D_MODEL), const3),
           pl.BlockSpec((D_MODEL, D_MODEL), const2)])
    return pl.pallas_call(
        functools.partial(_merge_body, tm=tm, l=l), grid=(b, l // tm),
        in_specs=in_specs,
        out_specs=pl.BlockSpec((1, tm, D_MODEL), lambda bi, i: (bi, i, 0)),
        out_shape=jax.ShapeDtypeStruct((b, l, D_MODEL), F32),
        compiler_params=_cparams(("parallel", "parallel")), name="merge",
    )(proj, proj, proj, proj, proj, proj, proj, proj, proj, proj, proj, proj,
      att, ys, xs_c, x, mod, conv_short, pool_w, pool_scale, dskip_x, ssd_norm, w_branch, w_out)


def _ffn_body(x_ref, xp_ref, xn_ref, mod_ref, g_ref, wg_ref, wv_ref, cwg_ref, cwv_ref, wd_ref, fn_ref,
              o_ref, h_ref, acc_ref, *, tm, l, nf, final):
    i = pl.program_id(1)
    j = pl.program_id(2)

    @pl.when(j == 0)
    def _():
        _, valid = _valid_rows(i, tm, l, D_MODEL)
        xe = jnp.concatenate([xp_ref[0], x_ref[0], xn_ref[0]], axis=0)
        h = _rms_mod(xe, g_ref[...], mod_ref[0, 4:5, :], mod_ref[0, 3:4, :])
        h_ref[...] = jnp.where(valid, h, 0.0).astype(BF16)
        acc_ref[...] = jnp.zeros(acc_ref.shape, F32)

    h = h_ref[...]
    main = slice(HALO, HALO + tm)
    ug = _conv3_rows(jnp.dot(h, wg_ref[...], preferred_element_type=F32), cwg_ref[...])[main]
    uv = _conv3_rows(jnp.dot(h, wv_ref[...], preferred_element_type=F32), cwv_ref[...])[main]
    a = (_silu(ug) * uv).astype(BF16)
    acc_ref[...] += jnp.dot(a, wd_ref[...], preferred_element_type=F32)

    @pl.when(j == nf - 1)
    def _():
        out = x_ref[0] + mod_ref[0, 5:6, :] * acc_ref[...]
        if final:
            out = out * lax.rsqrt(jnp.mean(out * out, axis=-1, keepdims=True) + NORM_EPS) * fn_ref[...]
        o_ref[0] = out


def _ffn(x, mod, g, w_up, ffn_conv, w_down, final_norm, tm, final):
    b, l, _ = x.shape
    tf = 256
    nf = D_FF // tf
    per_batch = mod.shape[0] > 1
    r = tm // HALO
    last = l // HALO - 1
    return pl.pallas_call(
        functools.partial(_ffn_body, tm=tm, l=l, nf=nf, final=final), grid=(b, l // tm, nf),
        in_specs=[pl.BlockSpec((1, tm, D_MODEL), lambda bi, i, j: (bi, i, 0)),
                  pl.BlockSpec((1, HALO, D_MODEL), lambda bi, i, j: (bi, jnp.maximum(i * r - 1, 0), 0)),
                  pl.BlockSpec((1, HALO, D_MODEL), lambda bi, i, j: (bi, jnp.minimum((i + 1) * r, last), 0)),
                  pl.BlockSpec((1, 6, D_MODEL), lambda bi, i, j: (bi if per_batch else 0, 0, 0)),
                  pl.BlockSpec((1, D_MODEL), lambda bi, i, j: (0, 0)),
                  pl.BlockSpec((D_MODEL, tf), lambda bi, i, j: (0, j)),
                  pl.BlockSpec((D_MODEL, tf), lambda bi, i, j: (0, nf + j)),
                  pl.BlockSpec((3, tf), lambda bi, i, j: (0, j)),
                  pl.BlockSpec((3, tf), lambda bi, i, j: (0, nf + j)),
                  pl.BlockSpec((tf, D_MODEL), lambda bi, i, j: (j, 0)),
                  pl.BlockSpec((1, D_MODEL), lambda bi, i, j: (0, 0))],
        out_specs=pl.BlockSpec((1, tm, D_MODEL), lambda bi, i, j: (bi, i, 0)),
        out_shape=jax.ShapeDtypeStruct((b, l, D_MODEL), F32),
        scratch_shapes=[pltpu.VMEM((tm + 2 * HALO, D_MODEL), BF16), pltpu.VMEM((tm, D_MODEL), F32)],
        compiler_params=_cparams(("parallel", "parallel", "arbitrary")), name="conv_ffn",
    )(x, x, x, mod, g.reshape(1, D_MODEL), w_up, w_up, ffn_conv, ffn_conv, w_down,
      final_norm.reshape(1, D_MODEL))


def _rope_tables(n_tokens):
    rows = n_tokens // GRID_W
    row = jnp.repeat(jnp.arange(rows), GRID_W).astype(F32)
    col = jnp.tile(jnp.arange(GRID_W), rows).astype(F32)
    inv = ROPE_THETA ** (-jnp.arange(ROPE_FREQS, dtype=F32) / ROPE_FREQS)
    ang_r, ang_c = row[:, None] * inv, col[:, None] * inv
    ang = jnp.concatenate([ang_r, ang_r, ang_c, ang_c], axis=1)
    sign = jnp.tile(jnp.concatenate([-jnp.ones(ROPE_FREQS), jnp.ones(ROPE_FREQS)]), 2).astype(F32)
    cos = jnp.tile(jnp.cos(ang), (1, 2))
    sin = jnp.tile(jnp.sin(ang) * sign, (1, 2))
    return cos, sin


def _identity_tables(n):
    return jnp.ones((n, LANES), F32), jnp.zeros((n, LANES), F32)


def _cat_weights(w_in):
    o = np.cumsum([0, 512, 128, 128, 512, 512, 512, 512, 512, 768, 16, 4096])
    q, k, v, bg, cg, u, pu, z, xbc, dt, gl = [w_in[:, o[i]:o[i + 1]] for i in range(11)]
    dt = jnp.pad(dt, ((0, 0), (0, DT_PAD - dt.shape[1])))
    return jnp.concatenate([gl, bg, cg, u, pu, z, q, xbc[:, :512], xbc[:, 512:], k, v, dt],
                           axis=1).astype(BF16)


def kernel(x, c, ctx, c_ctx, w_mod, b_mod, norm_mix, w_in, q_norm, k_norm, conv_short, pool_w, pool_scale,
           ssd_conv_w, ssd_conv_b, ssd_dt_bias, ssd_a_log, ssd_d, ssd_norm, w_branch, w_out, norm_ffn,
           w_up, ffn_conv, w_down, final_norm):
    depth = w_mod.shape[0]
    b, l, _ = x.shape
    lc = ctx.shape[1]
    ltot = l + lc
    t = SSD_CHUNK

    tm_in = min(1024, l)
    tm_mid = min(512, l)
    tq = 128
    tk_lat = 768 if ltot % 768 == 0 else lc
    tk_ctx = lc

    cos_l, sin_l = _rope_tables(l)
    cos_c, sin_c = _identity_tables(lc)
    cos_all = jnp.concatenate([cos_l, cos_c], axis=0)
    sin_all = jnp.concatenate([sin_l, sin_c], axis=0)
    lane = np.arange(LANES)
    bd = jnp.asarray((lane[:, None] // HEAD_DIM) == (lane[None, :] // HEAD_DIM), BF16)
    col512 = np.arange(512)
    expand = np.zeros((2, DT_PAD, 512), np.float32)
    for d in range(2):
        expand[d, d * SSD_HEADS + col512 // HEAD_DIM, col512] = 1.0
    expand = jnp.asarray(expand, BF16)
    pos = np.arange(t)
    tri = jnp.asarray(np.stack([pos[:, None] >= pos[None, :], pos[:, None] <= pos[None, :]]), BF16)
    neg = jnp.asarray(np.where(np.stack([pos[:, None] >= pos[None, :], pos[:, None] <= pos[None, :]]),
                               0.0, -np.inf), F32)
    hmask = jnp.asarray((lane[:, None] // SSD_STATE) == (col512[None, :] // 256), F32)

    cond8 = jnp.zeros((8, D_MODEL), F32).at[:b].set(c).at[b].set(c_ctx)
    xc = ctx
    for i in range(depth):
        last = i == depth - 1
        mod = _modulation(cond8, w_mod[i], b_mod[i]).reshape(8, 6, D_MODEL)
        mod_l, mod_c = mod[:b], mod[b:b + 1]
        w_cat = _cat_weights(w_in[i])
        gq = (jnp.tile(q_norm[i], 2) * QK_SCALE).reshape(1, LANES)
        gk = jnp.tile(k_norm[i], 2).reshape(1, LANES)
        a_rate_x = jnp.repeat(-jnp.exp(ssd_a_log[i].astype(F32)), HEAD_DIM, axis=1).reshape(2, 1, 512)
        dskip_x = jnp.repeat(ssd_d[i], HEAD_DIM).reshape(1, 512)
        dt_bias_pad = jnp.pad(ssd_dt_bias[i].reshape(1, 2 * SSD_HEADS), ((0, 0), (0, DT_PAD - 2 * SSD_HEADS)))
        wb = w_branch[i].astype(BF16)
        wo = w_out[i].astype(BF16)
        pw = pool_w[i].astype(BF16)
        wu = w_up[i].astype(BF16)
        wd = w_down[i].astype(BF16)
        ps = pool_scale[i].reshape(1, 512)
        sn = ssd_norm[i].reshape(1, 512)
        scb = ssd_conv_b[i].reshape(1, 768)

        proj = _inproj(x, mod_l, norm_mix[i], w_cat, tm_in)
        proj_c = _inproj(xc, mod_c, norm_mix[i], w_cat, lc)

        qh = _qprep(proj, cos_l, sin_l, gq, bd, tm_mid)
        kt, v1 = _kvprep(proj, proj_c, cos_all, sin_all, gk, bd)
        att = _flash(qh, kt, v1, ltot, 0, tq, tk_lat)

        xs_s, bc_s, dtx = _ssdprep(proj, proj_c, ssd_conv_w[i], scb, dt_bias_pad, expand)
        ys = _ssd_scan(xs_s, bc_s, dtx, a_rate_x, tri, neg, hmask, l)

        x_mid = _merge(proj, att, ys, xs_s, x, mod_l, conv_short[i], pw, ps, dskip_x, sn, wb, wo, tm_mid, 0)
        if not last:
            qh_c = _qprep(proj_c, cos_c, sin_c, gq, bd, lc)
            att_c = _flash(qh_c, kt, v1, lc, l, tq, tk_ctx)
            xc_mid = _merge(proj_c, att_c, ys, xs_s, xc, mod_c, conv_short[i], pw, ps, dskip_x, sn, wb, wo, lc, l)
            xc = _ffn(xc_mid, mod_c, norm_ffn[i], wu, ffn_conv[i], wd, final_norm, lc, False)
        x = _ffn(x_mid, mod_l, norm_ffn[i], wu, ffn_conv[i], wd, final_norm, tm_mid, last)
    return x
```

```python
import functools
import math

import jax
import jax.numpy as jnp
import numpy as np
from jax import lax
from jax.experimental import pallas as pl
from jax.experimental.pallas import tpu as pltpu

F32 = jnp.float32
BF16 = jnp.bfloat16

D_MODEL = 1024
GRID_W = 64
NORM_EPS = 1e-6
ATT_HEADS = 8
ATT_KV_HEADS = 2
HEAD_DIM = 64
ROPE_THETA = 10000.0
ROPE_FREQS = HEAD_DIM // 4
POOL_WINDOWS = (2, 4, 8, 16)
SSD_HEADS = 8
SSD_STATE = 64
SSD_CHUNK = 128
N_BRANCH = 4
BRANCH_WIDTH = 512
D_FF = 2816

C_GL, C_BG, C_CG, C_U, C_PU, C_Z, C_Q, C_XS, C_BC, C_K, C_V, C_DT = (
    0, 4096, 4608, 5120, 5632, 6144, 6656, 7168, 7680, 7936, 8064, 8192)
N_CAT = 8448
DT_PAD = 256

HALO = 16
POOL_BLOCK = 128
LANES = 128
QK_SCALE = HEAD_DIM ** -0.5 * math.log2(math.e)
SHIFT_SAMPLE = 256
P_LIMIT = 2.0 ** 60
VMEM_LIMIT = 56 * 1024 * 1024


def _cparams(sem):
    return pltpu.CompilerParams(dimension_semantics=sem, vmem_limit_bytes=VMEM_LIMIT)


def _silu(x):
    hx = 0.5 * x
    return hx * (jnp.tanh(hx) + 1.0)


def _split_dot(a, b_bf16):
    hi = a.astype(BF16)
    lo = (a - hi.astype(F32)).astype(BF16)
    return (jnp.dot(hi, b_bf16, preferred_element_type=F32)
            + jnp.dot(lo, b_bf16, preferred_element_type=F32))


def _dot_split_lhs(a_bf16, b):
    hi = b.astype(BF16)
    lo = (b - hi.astype(F32)).astype(BF16)
    return (jnp.dot(a_bf16, hi, preferred_element_type=F32)
            + jnp.dot(a_bf16, lo, preferred_element_type=F32))


def _rms_mod(x, g, scale, shift):
    ms = jnp.mean(x * x, axis=-1, keepdims=True)
    y = x * lax.rsqrt(ms + NORM_EPS) * g
    return y * (1.0 + scale) + shift


def _mod_body(c_ref, w_ref, b_ref, o_ref):
    s = _silu(c_ref[...])
    o_ref[...] = jnp.dot(s, w_ref[...], precision=lax.Precision.HIGHEST,
                         preferred_element_type=F32) + b_ref[...]


def _modulation(cond8, w_mod, b_mod):
    n = w_mod.shape[1]
    tn = 1536
    return pl.pallas_call(
        _mod_body, grid=(n // tn,),
        in_specs=[pl.BlockSpec((8, D_MODEL), lambda j: (0, 0)),
                  pl.BlockSpec((D_MODEL, tn), lambda j: (0, j)),
                  pl.BlockSpec((1, tn), lambda j: (0, j))],
        out_specs=pl.BlockSpec((8, tn), lambda j: (0, j)),
        out_shape=jax.ShapeDtypeStruct((8, n), F32),
        compiler_params=_cparams(("arbitrary",)), name="modulation",
    )(cond8, w_mod, b_mod.reshape(1, n))


def _inproj_body(x_ref, mod_ref, g_ref, w_ref, o_ref, h_ref):
    @pl.when(pl.program_id(2) == 0)
    def _():
        h = _rms_mod(x_ref[0], g_ref[...], mod_ref[0, 1:2, :], mod_ref[0, 0:1, :])
        h_ref[...] = h.astype(BF16)

    o_ref[0] = jnp.dot(h_ref[...], w_ref[...], preferred_element_type=F32).astype(BF16)


def _inproj(x, mod, g, w_cat, tm):
    b, l, _ = x.shape
    n = w_cat.shape[1]
    tn = 2816
    per_batch = mod.shape[0] > 1
    return pl.pallas_call(
        _inproj_body, grid=(b, l // tm, n // tn),
        in_specs=[pl.BlockSpec((1, tm, D_MODEL), lambda bi, i, j: (bi, i, 0)),
                  pl.BlockSpec((1, 6, D_MODEL), lambda bi, i, j: (bi if per_batch else 0, 0, 0)),
                  pl.BlockSpec((1, D_MODEL), lambda bi, i, j: (0, 0)),
                  pl.BlockSpec((D_MODEL, tn), lambda bi, i, j: (0, j))],
        out_specs=pl.BlockSpec((1, tm, tn), lambda bi, i, j: (bi, i, j)),
        out_shape=jax.ShapeDtypeStruct((b, l, n), BF16),
        scratch_shapes=[pltpu.VMEM((tm, D_MODEL), BF16)],
        compiler_params=_cparams(("parallel", "parallel", "arbitrary")), name="inproj",
    )(x, mod, g.reshape(1, D_MODEL), w_cat)


def _head_norm_rope(t, cos, sin, gain, bd):
    ss = _split_dot(t * t, bd)
    y = t * lax.rsqrt(ss * (1.0 / HEAD_DIM) + NORM_EPS) * gain
    lane = lax.broadcasted_iota(jnp.int32, y.shape, 1)
    partner = jnp.where((lane % 32) < 16, pltpu.roll(y, LANES - 16, 1), pltpu.roll(y, 16, 1))
    return y * cos + partner * sin


def _qprep_body(q_ref, cos_ref, sin_ref, g_ref, bd_ref, o_ref):
    cos, sin, gain, bd = cos_ref[...], sin_ref[...], g_ref[...], bd_ref[...]
    lane = lax.broadcasted_iota(jnp.int32, cos.shape, 1)
    for c in range(ATT_HEADS // 2):
        t = q_ref[0, :, c * LANES:(c + 1) * LANES].astype(F32)
        r = _head_norm_rope(t, cos, sin, gain, bd)
        o_ref[0, 2 * c] = jnp.where(lane < HEAD_DIM, r, 0.0).astype(BF16)
        o_ref[0, 2 * c + 1] = jnp.where(lane < HEAD_DIM, pltpu.roll(r, HEAD_DIM, 1), 0.0).astype(BF16)


def _qprep(proj, cos, sin, gain, bd, tm):
    b, l, _ = proj.shape
    return pl.pallas_call(
        _qprep_body, grid=(b, l // tm),
        in_specs=[pl.BlockSpec((1, tm, 512), lambda bi, i: (bi, i, C_Q // 512)),
                  pl.BlockSpec((tm, LANES), lambda bi, i: (i, 0)),
                  pl.BlockSpec((tm, LANES), lambda bi, i: (i, 0)),
                  pl.BlockSpec((1, LANES), lambda bi, i: (0, 0)),
                  pl.BlockSpec((LANES, LANES), lambda bi, i: (0, 0))],
        out_specs=pl.BlockSpec((1, ATT_HEADS, tm, LANES), lambda bi, i: (bi, 0, i, 0)),
        out_shape=jax.ShapeDtypeStruct((b, ATT_HEADS, l, LANES), BF16),
        compiler_params=_cparams(("parallel", "parallel")), name="q_prep",
    )(proj, cos, sin, gain, bd)


def _kvprep_body(kl_ref, vl_ref, kc_ref, vc_ref, cos_ref, sin_ref, g_ref, bd_ref, kt_ref, v1_ref, *, nt_lat):
    is_ctx = pl.program_id(1) == nt_lat
    k = jnp.where(is_ctx, kc_ref[0], kl_ref[0]).astype(F32)
    v = jnp.where(is_ctx, vc_ref[0], vl_ref[0]).astype(F32)
    r = _head_norm_rope(k, cos_ref[...], sin_ref[...], g_ref[...], bd_ref[...])
    lane = lax.broadcasted_iota(jnp.int32, r.shape, 1)
    ones_col = jnp.where(lane == HEAD_DIM, 1.0, 0.0)
    for g in range(ATT_KV_HEADS):
        rg = r if g == 0 else pltpu.roll(r, HEAD_DIM, 1)
        vg = v if g == 0 else pltpu.roll(v, HEAD_DIM, 1)
        kt_ref[0, g] = jnp.where(lane < HEAD_DIM, rg, ones_col).T.astype(BF16)
        v1_ref[0, g] = jnp.where(lane < HEAD_DIM, vg, ones_col).astype(BF16)


def _kvprep(proj, proj_c, cos, sin, gain, bd):
    b, l, _ = proj.shape
    lc = proj_c.shape[1]
    tm = lc
    nt_lat = l // tm
    ltot = l + lc
    lat = lambda bi, t: (bi, jnp.minimum(t, nt_lat - 1))
    return pl.pallas_call(
        functools.partial(_kvprep_body, nt_lat=nt_lat), grid=(b, nt_lat + 1),
        in_specs=[pl.BlockSpec((1, tm, LANES), lambda bi, t: lat(bi, t) + (C_K // LANES,)),
                  pl.BlockSpec((1, tm, LANES), lambda bi, t: lat(bi, t) + (C_V // LANES,)),
                  pl.BlockSpec((1, tm, LANES), lambda bi, t: (bi, 0, C_K // LANES)),
                  pl.BlockSpec((1, tm, LANES), lambda bi, t: (bi, 0, C_V // LANES)),
                  pl.BlockSpec((tm, LANES), lambda bi, t: (t, 0)),
                  pl.BlockSpec((tm, LANES), lambda bi, t: (t, 0)),
                  pl.BlockSpec((1, LANES), lambda bi, t: (0, 0)),
                  pl.BlockSpec((LANES, LANES), lambda bi, t: (0, 0))],
        out_specs=[pl.BlockSpec((1, ATT_KV_HEADS, LANES, tm), lambda bi, t: (bi, 0, 0, t)),
                   pl.BlockSpec((1, ATT_KV_HEADS, tm, LANES), lambda bi, t: (bi, 0, t, 0))],
        out_shape=[jax.ShapeDtypeStruct((b, ATT_KV_HEADS, LANES, ltot), BF16),
                   jax.ShapeDtypeStruct((b, ATT_KV_HEADS, ltot, LANES), BF16)],
        compiler_params=_cparams(("parallel", "parallel")), name="kv_prep",
    )(proj, proj, proj_c, proj_c, cos, sin, gain, bd)


def _flash_body(q_ref, kt_ref, v_ref, o_ref, qa_ref, s_ref, acc_ref, m_ref, *, tq, tk, nk):
    rep = ATT_HEADS // ATT_KV_HEADS
    rows = rep * tq
    q = q_ref[0].reshape(rows, LANES)

    def keys(c):
        return kt_ref[0, 0, :, pl.ds(pl.multiple_of(c * tk, tk), tk)]

    def vals(c):
        return v_ref[0, 0, pl.ds(pl.multiple_of(c * tk, tk), tk), :]

    s0 = jnp.dot(q, kt_ref[0, 0, :, 0:SHIFT_SAMPLE], preferred_element_type=F32)
    shift = jnp.max(s0, axis=1, keepdims=True)
    lane_q = lax.broadcasted_iota(jnp.int32, (rows, LANES), 1)
    qa_ref[...] = jnp.where(lane_q == HEAD_DIM, jnp.broadcast_to(-shift, (rows, LANES)).astype(BF16), q)
    qa = qa_ref[...]

    def scores(c):
        return jnp.dot(qa, keys(c), preferred_element_type=F32)

    def consume(buf, c, acc, pmax):
        p = jnp.exp2(s_ref[buf]).astype(BF16)
        for j in range(tk // LANES):
            pmax = jnp.maximum(pmax, p[:, j * LANES:(j + 1) * LANES])
        return acc + jnp.dot(p, vals(c), preferred_element_type=F32), pmax

    def pair(i, carry):
        acc, pmax = carry
        c = 2 * i
        s_ref[1] = scores(c + 1)
        acc, pmax = consume(0, c, acc, pmax)
        s_ref[0] = scores(c + 2)
        return consume(1, c + 1, acc, pmax)

    s_ref[0] = scores(0)
    n_pairs = (nk - 1) // 2
    acc, pmax = lax.fori_loop(0, n_pairs, pair,
                              (jnp.zeros((rows, LANES), F32), jnp.zeros((rows, LANES), BF16)))
    c = 2 * n_pairs
    if c == nk - 2:
        s_ref[1] = scores(c + 1)
        acc, pmax = consume(0, c, acc, pmax)
        acc, pmax = consume(1, c + 1, acc, pmax)
    else:
        acc, pmax = consume(0, c, acc, pmax)
    acc_ref[...] = acc

    @pl.when(jnp.max(pmax.astype(F32)) > P_LIMIT)
    def _():
        m_ref[...] = jnp.full(m_ref.shape, -jnp.inf, F32)
        acc_ref[...] = jnp.zeros(acc_ref.shape, F32)

        def step(c, carry):
            s = jnp.dot(q, keys(c), preferred_element_type=F32)
            m_prev = m_ref[...]
            m_new = jnp.maximum(m_prev, jnp.max(s, axis=1, keepdims=True))
            alpha = jnp.exp2(m_prev - m_new)
            p = jnp.exp2(s - jnp.tile(m_new, (1, tk // LANES))).astype(BF16)
            acc_ref[...] = acc_ref[...] * alpha + jnp.dot(p, vals(c), preferred_element_type=F32)
            m_ref[...] = m_new
            return carry

        lax.fori_loop(0, nk, step, 0)

    acc = acc_ref[...]
    o = acc / acc[:, HEAD_DIM:HEAD_DIM + 1]
    lane = lax.broadcasted_iota(jnp.int32, (tq, LANES), 1)
    for j in range(rep // 2):
        a = o[(2 * j) * tq:(2 * j + 1) * tq]
        bb = o[(2 * j + 1) * tq:(2 * j + 2) * tq]
        o_ref[0, :, j * LANES:(j + 1) * LANES] = jnp.where(
            lane < HEAD_DIM, a, pltpu.roll(bb, HEAD_DIM, 1)).astype(BF16)


def _flash(qh, kt, v1, lk, k_off, tq, tk):
    b, _, l, _ = qh.shape
    rep = ATT_HEADS // ATT_KV_HEADS
    kb = k_off // lk
    return pl.pallas_call(
        functools.partial(_flash_body, tq=tq, tk=tk, nk=lk // tk),
        grid=(b, ATT_KV_HEADS, l // tq),
        in_specs=[pl.BlockSpec((1, rep, tq, LANES), lambda bi, g, i: (bi, g, i, 0)),
                  pl.BlockSpec((1, 1, LANES, lk), lambda bi, g, i: (bi, g, 0, kb)),
                  pl.BlockSpec((1, 1, lk, LANES), lambda bi, g, i: (bi, g, kb, 0))],
        out_specs=pl.BlockSpec((1, tq, rep * HEAD_DIM), lambda bi, g, i: (bi, i, g)),
        out_shape=jax.ShapeDtypeStruct((b, l, ATT_HEADS * HEAD_DIM), BF16),
        scratch_shapes=[pltpu.VMEM((rep * tq, LANES), BF16), pltpu.VMEM((2, rep * tq, tk), F32),
                        pltpu.VMEM((rep * tq, LANES), F32), pltpu.VMEM((rep * tq, LANES), F32)],
        compiler_params=_cparams(("parallel", "parallel", "arbitrary")), name="flash_attention",
    )(qh, kt, v1)


def _halo_specs(tm, l, width, col_block, row_off_blocks=0):
    r = tm // HALO
    last = l // HALO - 1
    main = pl.BlockSpec((1, tm, width), lambda bi, i: (bi, i + row_off_blocks, col_block))
    prev = pl.BlockSpec((1, HALO, width), lambda bi, i: (bi, jnp.maximum(i * r - 1, 0), col_block))
    nxt = pl.BlockSpec((1, HALO, width), lambda bi, i: (bi, jnp.minimum((i + 1) * r, last), col_block))
    return [main, prev, nxt]


def _ext_rows(main_ref, prev_ref, next_ref, valid):
    e = jnp.concatenate([prev_ref[0].astype(F32), main_ref[0].astype(F32), next_ref[0].astype(F32)], axis=0)
    return jnp.where(valid, e, 0.0)


def _valid_rows(i, tm, l, width):
    pos = i * tm - HALO + lax.broadcasted_iota(jnp.int32, (tm + 2 * HALO, width), 0)
    return pos, (pos >= 0) & (pos < l)


def _conv3_rows(e, w):
    n = e.shape[0]
    return w[0:1] * pltpu.roll(e, 1, 0) + w[1:2] * e + w[2:3] * pltpu.roll(e, n - 1, 0)


def _ssdprep_body(xs_ref, xsp_ref, xsn_ref, bc_ref, bcp_ref, bcn_ref, dt_ref,
                  xsc_ref, bcc_ref, dtc_ref, w_ref, b_ref, dtb_ref, ex_ref,
                  xo_ref, bco_ref, dtx_ref, *, nt_lat, tm, l):
    t = pl.program_id(1)
    is_ctx = t == nt_lat

    def conv_silu(main_ref, prev_ref, next_ref, ctx_ref, lo, hi):
        width = hi - lo
        pos, valid = _valid_rows(t, tm, l, width)
        e_lat = _ext_rows(main_ref, prev_ref, next_ref, valid)
        zeros = jnp.zeros((HALO, width), F32)
        e_ctx = jnp.concatenate([zeros, ctx_ref[0].astype(F32), zeros], axis=0)
        e = jnp.where(is_ctx, e_ctx, e_lat)
        y = _conv3_rows(e, w_ref[:, lo:hi])[HALO:HALO + tm] + b_ref[:, lo:hi]
        return _silu(y)

    xo_ref[0] = conv_silu(xs_ref, xsp_ref, xsn_ref, xsc_ref, 0, 512).astype(BF16)
    bco_ref[0] = conv_silu(bc_ref, bcp_ref, bcn_ref, bcc_ref, 512, 768).astype(BF16)
    raw = jnp.where(is_ctx, dtc_ref[0], dt_ref[0]).astype(F32) + dtb_ref[...]
    dt = jnp.maximum(raw, 0.0) + jnp.log1p(jnp.exp(-jnp.abs(raw)))
    for d in range(2):
        dtx_ref[0, d] = _split_dot(dt, ex_ref[d])


def _ssdprep(proj, proj_c, conv_w, conv_b, dt_bias_pad, expand):
    b, l, _ = proj.shape
    lc = proj_c.shape[1]
    tm = lc
    nt_lat = l // tm
    ltot = l + lc
    r = tm // HALO
    last = l // HALO - 1
    cl = lambda t: jnp.minimum(t, nt_lat - 1)

    def lat_specs(width, col):
        return [pl.BlockSpec((1, tm, width), lambda bi, t: (bi, cl(t), col)),
                pl.BlockSpec((1, HALO, width), lambda bi, t: (bi, jnp.maximum(cl(t) * r - 1, 0), col)),
                pl.BlockSpec((1, HALO, width), lambda bi, t: (bi, jnp.minimum((cl(t) + 1) * r, last), col))]

    return pl.pallas_call(
        functools.partial(_ssdprep_body, nt_lat=nt_lat, tm=tm, l=l), grid=(b, nt_lat + 1),
        in_specs=(lat_specs(512, C_XS // 512) + lat_specs(256, C_BC // 256)
                  + [pl.BlockSpec((1, tm, DT_PAD), lambda bi, t: (bi, cl(t), C_DT // DT_PAD)),
                     pl.BlockSpec((1, tm, 512), lambda bi, t: (bi, 0, C_XS // 512)),
                     pl.BlockSpec((1, tm, 256), lambda bi, t: (bi, 0, C_BC // 256)),
                     pl.BlockSpec((1, tm, DT_PAD), lambda bi, t: (bi, 0, C_DT // DT_PAD)),
                     pl.BlockSpec((3, 768), lambda bi, t: (0, 0)),
                     pl.BlockSpec((1, 768), lambda bi, t: (0, 0)),
                     pl.BlockSpec((1, DT_PAD), lambda bi, t: (0, 0)),
                     pl.BlockSpec((2, DT_PAD, 512), lambda bi, t: (0, 0, 0))]),
        out_specs=[pl.BlockSpec((1, tm, 512), lambda bi, t: (bi, t, 0)),
                   pl.BlockSpec((1, tm, 256), lambda bi, t: (bi, t, 0)),
                   pl.BlockSpec((1, 2, tm, 512), lambda bi, t: (bi, 0, t, 0))],
        out_shape=[jax.ShapeDtypeStruct((b, ltot, 512), BF16),
                   jax.ShapeDtypeStruct((b, ltot, 256), BF16),
                   jax.ShapeDtypeStruct((b, 2, ltot, 512), F32)],
        compiler_params=_cparams(("parallel", "parallel")), name="ssd_prep",
    )(proj, proj, proj, proj, proj, proj, proj, proj_c, proj_c, proj_c,
      conv_w, conv_b, dt_bias_pad, expand)


SSD_STEP_CHUNKS = 2


def _ssd_body(xs_ref, bc_ref, dtx_ref, ar_ref, tri_ref, neg_ref, hm_ref, y_ref, ht_ref):
    t = SSD_CHUNK
    fwd = pl.program_id(1) == 0

    @pl.when(pl.program_id(2) == 0)
    def _():
        ht_ref[...] = jnp.zeros(ht_ref.shape, F32)

    lane = lax.broadcasted_iota(jnp.int32, (t, LANES), 1)
    low = lane < HEAD_DIM
    neg = neg_ref[0]

    def within_chunk(rows):
        xs = xs_ref[0, rows, :].astype(F32)
        dtx = dtx_ref[0, 0, rows, :]
        cum = _dot_split_lhs(tri_ref[0], dtx * ar_ref[0])
        tot = jnp.where(fwd, cum[t - 1:t, :], cum[0:1, :])
        xdt = xs * dtx
        xdt_b = xdt.astype(BF16)
        bc = bc_ref[0, rows, :]
        b_all, c_all = bc[:, :LANES], bc[:, LANES:]
        cb = []
        for g in range(2):
            cg = jnp.where(low if g == 0 else ~low, c_all, jnp.zeros_like(c_all))
            cb.append(lax.dot_general(cg, b_all, (((1,), (1,)), ((), ())), preferred_element_type=F32))
        ys = []
        for j in range(SSD_HEADS // 2):
            cp = cum[:, j * LANES:(j + 1) * LANES]
            cps = pltpu.roll(cp, HEAD_DIM, 1)
            cpt = cp.T
            xp = xdt_b[:, j * LANES:(j + 1) * LANES]
            acc = None
            for e in range(2):
                col = jnp.where(low, cp, cps) if e == 0 else jnp.where(low, cps, cp)
                row = jnp.broadcast_to(cpt[e * HEAD_DIM:e * HEAD_DIM + 1, :], (t, t))
                w = (cb[j // 2] * jnp.exp(col - row + neg)).astype(BF16)
                xm = jnp.where(low if e == 0 else ~low, xp, jnp.zeros_like(xp))
                part = jnp.dot(w, xm, preferred_element_type=F32)
                acc = part if acc is None else acc + part
            ys.append(acc)
        bt = b_all.astype(F32).T.astype(BF16)
        st = jnp.dot(bt, (xdt * jnp.exp(tot - cum)).astype(BF16), preferred_element_type=F32)
        return jnp.concatenate(ys, axis=1), c_all, jnp.exp(cum), jnp.exp(tot), st * hm_ref[...]

    order = [jnp.where(fwd, k, SSD_STEP_CHUNKS - 1 - k) for k in range(SSD_STEP_CHUNKS)]
    rows = [pl.ds(pl.multiple_of(o * t, t), t) for o in order]
    local = [within_chunk(r) for r in rows]
    ht = ht_ref[...]
    for r, (y_diag, c_all, ecum, etot, st) in zip(rows, local):
        y_off = jnp.dot(c_all, ht.astype(BF16), preferred_element_type=F32) * ecum
        y_ref[0, 0, r, :] = (y_diag + y_off).astype(BF16)
        ht = ht * etot + st
    ht_ref[...] = ht


def _ssd_scan(xs, bc, dtx, a_rate_x, tri, neg, hmask, l_lat):
    b, ltot, _ = xs.shape
    t = SSD_CHUNK
    ts = t * SSD_STEP_CHUNKS
    ns = ltot // ts
    nl = l_lat // ts
    nct = ns - nl

    def sid(d, c):
        f = jnp.where(c < nct, nl + c, c - nct)
        r = jnp.where(c < nct, nl + nct - 1 - c, nl - 1 - (c - nct))
        return jnp.where(d == 0, f, r)

    return pl.pallas_call(
        _ssd_body, grid=(b, 2, ns),
        in_specs=[pl.BlockSpec((1, ts, 512), lambda bi, d, c: (bi, sid(d, c), 0)),
                  pl.BlockSpec((1, ts, 256), lambda bi, d, c: (bi, sid(d, c), 0)),
                  pl.BlockSpec((1, 1, ts, 512), lambda bi, d, c: (bi, d, sid(d, c), 0)),
                  pl.BlockSpec((1, 1, 512), lambda bi, d, c: (d, 0, 0)),
                  pl.BlockSpec((1, t, t), lambda bi, d, c: (d, 0, 0)),
                  pl.BlockSpec((1, t, t), lambda bi, d, c: (d, 0, 0)),
                  pl.BlockSpec((LANES, 512), lambda bi, d, c: (0, 0))],
        out_specs=pl.BlockSpec((1, 1, ts, 512), lambda bi, d, c: (bi, d, sid(d, c), 0)),
        out_shape=jax.ShapeDtypeStruct((b, 2, ltot, 512), BF16),
        scratch_shapes=[pltpu.VMEM((LANES, 512), F32)],
        compiler_params=_cparams(("parallel", "parallel", "arbitrary")), name="ssd_scan",
    )(xs, bc, dtx, a_rate_x, tri, neg, hmask)


def _merge_body(gl_ref, bg_ref, cg_ref, cgp_ref, cgn_ref, u_ref, up_ref, un_ref,
                pu_ref, pup_ref, pun_ref, z_ref, att_ref, ys_ref, xs_ref, x_ref, mod_ref,
                cw_ref, band_ref, pw_ref, ps_ref, dsk_ref, sn_ref, wb_ref, wo_ref, o_ref, v_ref, *, tm, l):
    i = pl.program_id(1)
    has_prev = i > 0
    has_next = i < l // tm - 1

    def ext(main_ref, prev_ref, next_ref):
        prev = jnp.where(has_prev, prev_ref[0], jnp.zeros_like(prev_ref[0]))
        nxt = jnp.where(has_next, next_ref[0], jnp.zeros_like(next_ref[0]))
        return jnp.concatenate([prev, main_ref[0], nxt], axis=0)

    v_ref[...] = ext(cg_ref, cgp_ref, cgn_ref).astype(F32) * ext(u_ref, up_ref, un_ref).astype(F32)
    cw = cw_ref[...]
    conv = (cw[0:1] * v_ref[pl.ds(HALO - 1, tm), :] + cw[1:2] * v_ref[pl.ds(HALO, tm), :]
            + cw[2:3] * v_ref[pl.ds(HALO + 1, tm), :])
    br_conv = bg_ref[0].astype(F32) * conv

    pe = ext(pu_ref, pup_ref, pun_ref)
    posm = i * tm + lax.broadcasted_iota(jnp.int32, (tm, LANES), 0)
    pooled = []
    for g, win in enumerate(POOL_WINDOWS):
        cols = slice(g * LANES, (g + 1) * LANES)
        sums = jnp.concatenate(
            [jnp.dot(band_ref[g], pe[k * POOL_BLOCK:k * POOL_BLOCK + POOL_BLOCK + 2 * HALO, cols],
                     preferred_element_type=F32) for k in range(tm // POOL_BLOCK)], axis=0)
        cnt = (jnp.minimum(posm, win // 2) + jnp.minimum(l - posm, win // 2)).astype(F32)
        pm = (sums / cnt - pu_ref[0, :, cols].astype(F32)).astype(BF16)
        pooled.append(jnp.dot(pm, pw_ref[g], preferred_element_type=F32))
    br_pool = jnp.concatenate(pooled, axis=1) * ps_ref[...]

    y = ys_ref[0, 0].astype(F32) + ys_ref[0, 1].astype(F32) + dsk_ref[...] * xs_ref[0].astype(F32)
    gz = y * _silu(z_ref[0].astype(F32))
    br_ssd = gz * lax.rsqrt(jnp.mean(gz * gz, axis=-1, keepdims=True) + NORM_EPS) * sn_ref[...]

    branches = (att_ref[0], br_conv.astype(BF16), br_pool.astype(BF16), br_ssd.astype(BF16))
    acc = None
    for k, br in enumerate(branches):
        gate2 = jnp.tanh(gl_ref[0, :, k * D_MODEL:(k + 1) * D_MODEL].astype(F32)) + 1.0
        term = gate2 * jnp.dot(br, wb_ref[k], preferred_element_type=F32)
        acc = term if acc is None else acc + term
    yo = jnp.dot((0.5 * acc).astype(BF16), wo_ref[...], preferred_element_type=F32)
    o_ref[0] = x_ref[0] + mod_ref[0, 2:3, :] * yo


def _merge(proj, att, ys, xs_c, x, mod, conv_short, band, pool_w, pool_scale, dskip_x, ssd_norm,
           w_branch, w_out, tm, ssd_row_off):
    b, l, _ = x.shape
    per_batch = mod.shape[0] > 1
    off = ssd_row_off // tm
    const2 = lambda bi, i: (0, 0)
    const3 = lambda bi, i: (0, 0, 0)
    in_specs = (
        [pl.BlockSpec((1, tm, 4096), lambda bi, i: (bi, i, 0)),
         pl.BlockSpec((1, tm, 512), lambda bi, i: (bi, i, C_BG // 512))]
        + _halo_specs(tm, l, 512, C_CG // 512) + _halo_specs(tm, l, 512, C_U // 512)
        + _halo_specs(tm, l, 512, C_PU // 512)
        + [pl.BlockSpec((1, tm, 512), lambda bi, i: (bi, i, C_Z // 512)),
           pl.BlockSpec((1, tm, 512), lambda bi, i: (bi, i, 0)),
           pl.BlockSpec((1, 2, tm, 512), lambda bi, i: (bi, 0, i + off, 0)),
           pl.BlockSpec((1, tm, 512), lambda bi, i: (bi, i + off, 0)),
           pl.BlockSpec((1, tm, D_MODEL), lambda bi, i: (bi, i, 0)),
           pl.BlockSpec((1, 6, D_MODEL), lambda bi, i: (bi if per_batch else 0, 0, 0)),
           pl.BlockSpec((3, 512), const2),
           pl.BlockSpec((4, POOL_BLOCK, POOL_BLOCK + 2 * HALO), const3),
           pl.BlockSpec((4, LANES, LANES), const3),
           pl.BlockSpec((1, 512), const2),
           pl.BlockSpec((1, 512), const2),
           pl.BlockSpec((1, 512), const2),
           pl.BlockSpec((N_BRANCH, BRANCH_WIDTH, D_MODEL), const3),
           pl.BlockSpec((D_MODEL, D_MODEL), const2)])
    return pl.pallas_call(
        functools.partial(_merge_body, tm=tm, l=l), grid=(b, l // tm),
        in_specs=in_specs,
        out_specs=pl.BlockSpec((1, tm, D_MODEL), lambda bi, i: (bi, i, 0)),
        out_shape=jax.ShapeDtypeStruct((b, l, D_MODEL), F32),
        scratch_shapes=[pltpu.VMEM((tm + 2 * HALO, 512), F32)],
        compiler_params=_cparams(("parallel", "parallel")), name="merge",
    )(proj, proj, proj, proj, proj, proj, proj, proj, proj, proj, proj, proj,
      att, ys, xs_c, x, mod, conv_short, band, pool_w, pool_scale, dskip_x, ssd_norm, w_branch, w_out)


FFN_HALO = 8
FFN_CHUNK = 256


def _ffn_body(x_ref, xp_ref, xn_ref, mod_ref, g_ref, wu_ref, cw_ref, wd_ref, fn_ref,
              o_ref, h_ref, ug_ref, uv_ref, a_ref, *, tm, l, final):
    i = pl.program_id(1)
    n_ext = tm + 2 * FFN_HALO
    pos = i * tm - FFN_HALO + lax.broadcasted_iota(jnp.int32, (n_ext, D_MODEL), 0)
    xe = jnp.concatenate([xp_ref[0], x_ref[0], xn_ref[0]], axis=0)
    h = _rms_mod(xe, g_ref[...], mod_ref[0, 4:5, :], mod_ref[0, 3:4, :])
    h_ref[...] = jnp.where((pos >= 0) & (pos < l), h, 0.0).astype(BF16)

    def conv3(u_ref, s, w):
        return (w[0:1] * u_ref[s, pl.ds(FFN_HALO - 1, tm), :] + w[1:2] * u_ref[s, pl.ds(FFN_HALO, tm), :]
                + w[2:3] * u_ref[s, pl.ds(FFN_HALO + 1, tm), :])

    for j in range(D_FF // FFN_CHUNK):
        s = j % 2
        cg = slice(j * FFN_CHUNK, (j + 1) * FFN_CHUNK)
        cv = slice(D_FF + j * FFN_CHUNK, D_FF + (j + 1) * FFN_CHUNK)
        ug_ref[s] = jnp.dot(h_ref[...], wu_ref[:, cg], preferred_element_type=F32)
        uv_ref[s] = jnp.dot(h_ref[...], wu_ref[:, cv], preferred_element_type=F32)
        a_ref[:, cg] = (_silu(conv3(ug_ref, s, cw_ref[:, cg]))
                        * conv3(uv_ref, s, cw_ref[:, cv])).astype(BF16)

    out = x_ref[0] + mod_ref[0, 5:6, :] * jnp.dot(a_ref[...], wd_ref[...], preferred_element_type=F32)
    if final:
        out = out * lax.rsqrt(jnp.mean(out * out, axis=-1, keepdims=True) + NORM_EPS) * fn_ref[...]
    o_ref[0] = out


def _ffn(x, mod, g, w_up, ffn_conv, w_down, final_norm, tm, final):
    b, l, _ = x.shape
    per_batch = mod.shape[0] > 1
    r = tm // FFN_HALO
    last = l // FFN_HALO - 1
    n_ext = tm + 2 * FFN_HALO
    const = lambda bi, i: (0, 0)
    resident = pl.Buffered(1)
    return pl.pallas_call(
        functools.partial(_ffn_body, tm=tm, l=l, final=final), grid=(b, l // tm),
        in_specs=[pl.BlockSpec((1, tm, D_MODEL), lambda bi, i: (bi, i, 0)),
                  pl.BlockSpec((1, FFN_HALO, D_MODEL), lambda bi, i: (bi, jnp.maximum(i * r - 1, 0), 0)),
                  pl.BlockSpec((1, FFN_HALO, D_MODEL), lambda bi, i: (bi, jnp.minimum((i + 1) * r, last), 0)),
                  pl.BlockSpec((1, 6, D_MODEL), lambda bi, i: (bi if per_batch else 0, 0, 0)),
                  pl.BlockSpec((1, D_MODEL), const),
                  pl.BlockSpec((D_MODEL, 2 * D_FF), const, pipeline_mode=resident),
                  pl.BlockSpec((3, 2 * D_FF), const, pipeline_mode=resident),
                  pl.BlockSpec((D_FF, D_MODEL), const, pipeline_mode=resident),
                  pl.BlockSpec((1, D_MODEL), const)],
        out_specs=pl.BlockSpec((1, tm, D_MODEL), lambda bi, i: (bi, i, 0)),
        out_shape=jax.ShapeDtypeStruct((b, l, D_MODEL), F32),
        scratch_shapes=[pltpu.VMEM((n_ext, D_MODEL), BF16), pltpu.VMEM((2, n_ext, FFN_CHUNK), F32),
                        pltpu.VMEM((2, n_ext, FFN_CHUNK), F32), pltpu.VMEM((tm, D_FF), BF16)],
        compiler_params=_cparams(("parallel", "parallel")), name="conv_ffn",
    )(x, x, x, mod, g.reshape(1, D_MODEL), w_up, ffn_conv, w_down, final_norm.reshape(1, D_MODEL))


def _rope_tables(n_tokens):
    rows = n_tokens // GRID_W
    row = jnp.repeat(jnp.arange(rows), GRID_W).astype(F32)
    col = jnp.tile(jnp.arange(GRID_W), rows).astype(F32)
    inv = ROPE_THETA ** (-jnp.arange(ROPE_FREQS, dtype=F32) / ROPE_FREQS)
    ang_r, ang_c = row[:, None] * inv, col[:, None] * inv
    ang = jnp.concatenate([ang_r, ang_r, ang_c, ang_c], axis=1)
    sign = jnp.tile(jnp.concatenate([-jnp.ones(ROPE_FREQS), jnp.ones(ROPE_FREQS)]), 2).astype(F32)
    cos = jnp.tile(jnp.cos(ang), (1, 2))
    sin = jnp.tile(jnp.sin(ang) * sign, (1, 2))
    return cos, sin


def _identity_tables(n):
    return jnp.ones((n, LANES), F32), jnp.zeros((n, LANES), F32)


def _cat_weights(w_in):
    o = np.cumsum([0, 512, 128, 128, 512, 512, 512, 512, 512, 768, 16, 4096])
    q, k, v, bg, cg, u, pu, z, xbc, dt, gl = [w_in[:, o[i]:o[i + 1]] for i in range(11)]
    dt = jnp.pad(dt, ((0, 0), (0, DT_PAD - dt.shape[1])))
    gl = 0.5 * gl
    return jnp.concatenate([gl, bg, cg, u, pu, z, q, xbc[:, :512], xbc[:, 512:], k, v, dt],
                           axis=1).astype(BF16)


def kernel(x, c, ctx, c_ctx, w_mod, b_mod, norm_mix, w_in, q_norm, k_norm, conv_short, pool_w, pool_scale,
           ssd_conv_w, ssd_conv_b, ssd_dt_bias, ssd_a_log, ssd_d, ssd_norm, w_branch, w_out, norm_ffn,
           w_up, ffn_conv, w_down, final_norm):
    depth = w_mod.shape[0]
    b, l, _ = x.shape
    lc = ctx.shape[1]
    ltot = l + lc
    t = SSD_CHUNK

    tm_in = min(1024, l)
    tm_mid = min(512, l)
    tq = 128
    tk_lat = 768 if ltot % 768 == 0 else lc
    tk_ctx = lc

    cos_l, sin_l = _rope_tables(l)
    cos_c, sin_c = _identity_tables(lc)
    cos_all = jnp.concatenate([cos_l, cos_c], axis=0)
    sin_all = jnp.concatenate([sin_l, sin_c], axis=0)
    lane = np.arange(LANES)
    bd = jnp.asarray((lane[:, None] // HEAD_DIM) == (lane[None, :] // HEAD_DIM), BF16)
    col512 = np.arange(512)
    expand = np.zeros((2, DT_PAD, 512), np.float32)
    for d in range(2):
        expand[d, d * SSD_HEADS + col512 // HEAD_DIM, col512] = 1.0
    expand = jnp.asarray(expand, BF16)
    pos = np.arange(t)
    tri = jnp.asarray(np.stack([pos[:, None] >= pos[None, :], pos[:, None] <= pos[None, :]]), BF16)
    neg = jnp.asarray(np.where(np.stack([pos[:, None] >= pos[None, :], pos[:, None] <= pos[None, :]]),
                               0.0, -np.inf), F32)
    brow, bcol = np.arange(POOL_BLOCK)[:, None], np.arange(POOL_BLOCK + 2 * HALO)[None, :]
    band = jnp.asarray(np.stack([(bcol >= HALO + brow - w // 2) & (bcol < HALO + brow + w // 2)
                                 for w in POOL_WINDOWS]), BF16)
    hmask = jnp.asarray((lane[:, None] // SSD_STATE) == (col512[None, :] // 256), F32)

    cond8 = jnp.zeros((8, D_MODEL), F32).at[:b].set(c).at[b].set(c_ctx)
    xc = ctx
    for i in range(depth):
        last = i == depth - 1
        mod = _modulation(cond8, w_mod[i], b_mod[i]).reshape(8, 6, D_MODEL)
        mod_l, mod_c = mod[:b], mod[b:b + 1]
        w_cat = _cat_weights(w_in[i])
        gq = (jnp.tile(q_norm[i], 2) * QK_SCALE).reshape(1, LANES)
        gk = jnp.tile(k_norm[i], 2).reshape(1, LANES)
        a_rate_x = jnp.repeat(-jnp.exp(ssd_a_log[i].astype(F32)), HEAD_DIM, axis=1).reshape(2, 1, 512)
        dskip_x = jnp.repeat(ssd_d[i], HEAD_DIM).reshape(1, 512)
        dt_bias_pad = jnp.pad(ssd_dt_bias[i].reshape(1, 2 * SSD_HEADS), ((0, 0), (0, DT_PAD - 2 * SSD_HEADS)))
        wb = w_branch[i].astype(BF16)
        wo = w_out[i].astype(BF16)
        pw = pool_w[i].astype(BF16)
        wu = w_up[i].astype(BF16)
        wd = w_down[i].astype(BF16)
        ps = pool_scale[i].reshape(1, 512)
        sn = ssd_norm[i].reshape(1, 512)
        scb = ssd_conv_b[i].reshape(1, 768)

        proj = _inproj(x, mod_l, norm_mix[i], w_cat, tm_in)
        proj_c = _inproj(xc, mod_c, norm_mix[i], w_cat, lc)

        qh = _qprep(proj, cos_l, sin_l, gq, bd, tm_mid)
        kt, v1 = _kvprep(proj, proj_c, cos_all, sin_all, gk, bd)
        att = _flash(qh, kt, v1, ltot, 0, tq, tk_lat)

        xs_s, bc_s, dtx = _ssdprep(proj, proj_c, ssd_conv_w[i], scb, dt_bias_pad, expand)
        ys = _ssd_scan(xs_s, bc_s, dtx, a_rate_x, tri, neg, hmask, l)

        x_mid = _merge(proj, att, ys, xs_s, x, mod_l, conv_short[i], band, pw, ps, dskip_x, sn, wb, wo, tm_mid, 0)
        if not last:
            qh_c = _qprep(proj_c, cos_c, sin_c, gq, bd, lc)
            att_c = _flash(qh_c, kt, v1, lc, l, tq, tk_ctx)
            xc_mid = _merge(proj_c, att_c, ys, xs_s, xc, mod_c, conv_short[i], band, pw, ps, dskip_x, sn, wb, wo, lc, l)
            xc = _ffn(xc_mid, mod_c, norm_ffn[i], wu, ffn_conv[i], wd, final_norm, lc, False)
        x = _ffn(x_mid, mod_l, norm_ffn[i], wu, ffn_conv[i], wd, final_norm, tm_mid, last)
    return x
```

```python
import functools
import math

import jax
import jax.numpy as jnp
import numpy as np
from jax import lax
from jax.experimental import pallas as pl
from jax.experimental.pallas import tpu as pltpu

F32 = jnp.float32
BF16 = jnp.bfloat16

D_MODEL = 1024
GRID_W = 64
NORM_EPS = 1e-6
ATT_HEADS = 8
ATT_KV_HEADS = 2
HEAD_DIM = 64
ROPE_THETA = 10000.0
ROPE_FREQS = HEAD_DIM // 4
POOL_WINDOWS = (2, 4, 8, 16)
SSD_HEADS = 8
SSD_STATE = 64
SSD_CHUNK = 128
N_BRANCH = 4
BRANCH_WIDTH = 512
D_FF = 2816

C_GL, C_BG, C_CG, C_U, C_PU, C_Z, C_Q, C_XS, C_BC, C_K, C_V, C_DT = (
    0, 4096, 4608, 5120, 5632, 6144, 6656, 7168, 7680, 7936, 8064, 8192)
N_CAT = 8448
DT_PAD = 256

HALO = 16
POOL_BLOCK = 128
LANES = 128
QK_SCALE = HEAD_DIM ** -0.5 * math.log2(math.e)
ROW_SUM_MIN = 2.0 ** -47
VMEM_LIMIT = 56 * 1024 * 1024


def _cparams(sem):
    return pltpu.CompilerParams(dimension_semantics=sem, vmem_limit_bytes=VMEM_LIMIT)


def _silu(x):
    hx = 0.5 * x
    return hx * (jnp.tanh(hx) + 1.0)


def _split_dot(a, b_bf16):
    hi = a.astype(BF16)
    lo = (a - hi.astype(F32)).astype(BF16)
    return (jnp.dot(hi, b_bf16, preferred_element_type=F32)
            + jnp.dot(lo, b_bf16, preferred_element_type=F32))


def _dot_split_lhs(a_bf16, b):
    hi = b.astype(BF16)
    lo = (b - hi.astype(F32)).astype(BF16)
    return (jnp.dot(a_bf16, hi, preferred_element_type=F32)
            + jnp.dot(a_bf16, lo, preferred_element_type=F32))


def _rms_mod(x, g, scale, shift):
    ms = jnp.mean(x * x, axis=-1, keepdims=True)
    y = x * lax.rsqrt(ms + NORM_EPS) * g
    return y * (1.0 + scale) + shift


def _mod_body(c_ref, w_ref, b_ref, o_ref):
    s = _silu(c_ref[...])
    o_ref[...] = jnp.dot(s, w_ref[...], precision=lax.Precision.HIGHEST,
                         preferred_element_type=F32) + b_ref[...]


def _modulation(cond8, w_mod, b_mod):
    depth, _, n = w_mod.shape
    tn = 1536
    return pl.pallas_call(
        _mod_body, grid=(depth, n // tn),
        in_specs=[pl.BlockSpec((8, D_MODEL), lambda d, j: (0, 0)),
                  pl.BlockSpec((None, D_MODEL, tn), lambda d, j: (d, 0, j)),
                  pl.BlockSpec((None, 1, tn), lambda d, j: (d, 0, j))],
        out_specs=pl.BlockSpec((None, 8, tn), lambda d, j: (d, 0, j)),
        out_shape=jax.ShapeDtypeStruct((depth, 8, n), F32),
        compiler_params=_cparams(("arbitrary", "arbitrary")), name="modulation",
    )(cond8, w_mod, b_mod.reshape(depth, 1, n))


def _inproj_body(x_ref, mod_ref, g_ref, w_ref, o_ref, h_ref):
    @pl.when(pl.program_id(2) == 0)
    def _():
        h = _rms_mod(x_ref[0], g_ref[...], mod_ref[0, 1:2, :], mod_ref[0, 0:1, :])
        h_ref[...] = h.astype(BF16)

    o_ref[0] = jnp.dot(h_ref[...], w_ref[...], preferred_element_type=F32).astype(BF16)


def _inproj(x, mod, g, w_cat, layer, tm):
    b, l, _ = x.shape
    n = w_cat.shape[2]
    tn = 2816
    per_batch = mod.shape[0] > 1
    return pl.pallas_call(
        _inproj_body, grid=(b, l // tm, n // tn),
        in_specs=[pl.BlockSpec((1, tm, D_MODEL), lambda bi, i, j: (bi, i, 0)),
                  pl.BlockSpec((1, 6, D_MODEL), lambda bi, i, j: (bi if per_batch else 0, 0, 0)),
                  pl.BlockSpec((1, D_MODEL), lambda bi, i, j: (0, 0)),
                  pl.BlockSpec((None, D_MODEL, tn), lambda bi, i, j: (layer, 0, j))],
        out_specs=pl.BlockSpec((1, tm, tn), lambda bi, i, j: (bi, i, j)),
        out_shape=jax.ShapeDtypeStruct((b, l, n), BF16),
        scratch_shapes=[pltpu.VMEM((tm, D_MODEL), BF16)],
        compiler_params=_cparams(("parallel", "parallel", "arbitrary")), name="inproj",
    )(x, mod, g.reshape(1, D_MODEL), w_cat)


def _head_norm_rope(t, cos, sin, gain, bd):
    ss = _split_dot(t * t, bd)
    y = t * lax.rsqrt(ss * (1.0 / HEAD_DIM) + NORM_EPS) * gain
    lane = lax.broadcasted_iota(jnp.int32, y.shape, 1)
    partner = jnp.where((lane % 32) < 16, pltpu.roll(y, LANES - 16, 1), pltpu.roll(y, 16, 1))
    return y * cos + partner * sin


def _qprep_body(q_ref, cos_ref, sin_ref, g_ref, negb_ref, bd_ref, o_ref):
    cos, sin, gain, bd = cos_ref[...], sin_ref[...], g_ref[...], bd_ref[...]
    lane = lax.broadcasted_iota(jnp.int32, cos.shape, 1)
    for c in range(ATT_HEADS // 2):
        t = q_ref[0, :, c * LANES:(c + 1) * LANES].astype(F32)
        r = _head_norm_rope(t, cos, sin, gain, bd)
        tail = jnp.where(lane == HEAD_DIM, negb_ref[...], 0.0)
        o_ref[0, 2 * c] = jnp.where(lane < HEAD_DIM, r, tail).astype(BF16)
        o_ref[0, 2 * c + 1] = jnp.where(lane < HEAD_DIM, pltpu.roll(r, HEAD_DIM, 1), tail).astype(BF16)


def _qprep(proj, cos, sin, gain, neg_bound, bd, tm):
    b, l, _ = proj.shape
    return pl.pallas_call(
        _qprep_body, grid=(b, l // tm),
        in_specs=[pl.BlockSpec((1, tm, 512), lambda bi, i: (bi, i, C_Q // 512)),
                  pl.BlockSpec((tm, LANES), lambda bi, i: (i, 0)),
                  pl.BlockSpec((tm, LANES), lambda bi, i: (i, 0)),
                  pl.BlockSpec((1, LANES), lambda bi, i: (0, 0)),
                  pl.BlockSpec((1, LANES), lambda bi, i: (0, 0)),
                  pl.BlockSpec((LANES, LANES), lambda bi, i: (0, 0))],
        out_specs=pl.BlockSpec((1, ATT_HEADS, tm, LANES), lambda bi, i: (bi, 0, i, 0)),
        out_shape=jax.ShapeDtypeStruct((b, ATT_HEADS, l, LANES), BF16),
        compiler_params=_cparams(("parallel", "parallel")), name="q_prep",
    )(proj, cos, sin, gain, neg_bound, bd)


def _kvprep_body(kl_ref, vl_ref, kc_ref, vc_ref, cos_ref, sin_ref, g_ref, bd_ref, kt_ref, v1_ref, *, nt_lat):
    is_ctx = pl.program_id(1) == nt_lat
    k = jnp.where(is_ctx, kc_ref[0], kl_ref[0]).astype(F32)
    v = jnp.where(is_ctx, vc_ref[0], vl_ref[0]).astype(F32)
    r = _head_norm_rope(k, cos_ref[...], sin_ref[...], g_ref[...], bd_ref[...])
    lane = lax.broadcasted_iota(jnp.int32, r.shape, 1)
    ones_col = jnp.where(lane == HEAD_DIM, 1.0, 0.0)
    for g in range(ATT_KV_HEADS):
        rg = r if g == 0 else pltpu.roll(r, HEAD_DIM, 1)
        vg = v if g == 0 else pltpu.roll(v, HEAD_DIM, 1)
        kt_ref[0, g] = jnp.where(lane < HEAD_DIM, rg, ones_col).T.astype(BF16)
        v1_ref[0, g] = jnp.where(lane < HEAD_DIM, vg, ones_col).astype(BF16)


def _kvprep(proj, proj_c, cos, sin, gain, bd):
    b, l, _ = proj.shape
    lc = proj_c.shape[1]
    tm = lc
    nt_lat = l // tm
    ltot = l + lc
    lat = lambda bi, t: (bi, jnp.minimum(t, nt_lat - 1))
    return pl.pallas_call(
        functools.partial(_kvprep_body, nt_lat=nt_lat), grid=(b, nt_lat + 1),
        in_specs=[pl.BlockSpec((1, tm, LANES), lambda bi, t: lat(bi, t) + (C_K // LANES,)),
                  pl.BlockSpec((1, tm, LANES), lambda bi, t: lat(bi, t) + (C_V // LANES,)),
                  pl.BlockSpec((1, tm, LANES), lambda bi, t: (bi, 0, C_K // LANES)),
                  pl.BlockSpec((1, tm, LANES), lambda bi, t: (bi, 0, C_V // LANES)),
                  pl.BlockSpec((tm, LANES), lambda bi, t: (t, 0)),
                  pl.BlockSpec((tm, LANES), lambda bi, t: (t, 0)),
                  pl.BlockSpec((1, LANES), lambda bi, t: (0, 0)),
                  pl.BlockSpec((LANES, LANES), lambda bi, t: (0, 0))],
        out_specs=[pl.BlockSpec((1, ATT_KV_HEADS, LANES, tm), lambda bi, t: (bi, 0, 0, t)),
                   pl.BlockSpec((1, ATT_KV_HEADS, tm, LANES), lambda bi, t: (bi, 0, t, 0))],
        out_shape=[jax.ShapeDtypeStruct((b, ATT_KV_HEADS, LANES, ltot), BF16),
                   jax.ShapeDtypeStruct((b, ATT_KV_HEADS, ltot, LANES), BF16)],
        compiler_params=_cparams(("parallel", "parallel")), name="kv_prep",
    )(proj, proj, proj_c, proj_c, cos, sin, gain, bd)


def _flash_body(q_ref, kt_ref, v_ref, o_ref, s_ref, acc_ref, m_ref, *, tq, tk, nk):
    rep = ATT_HEADS // ATT_KV_HEADS
    rows = rep * tq
    q = q_ref[0].reshape(rows, LANES)

    def keys(c):
        return kt_ref[0, 0, :, pl.ds(pl.multiple_of(c * tk, tk), tk)]

    def vals(c):
        return v_ref[0, 0, pl.ds(pl.multiple_of(c * tk, tk), tk), :]

    s_ref[0] = jnp.dot(q, keys(0), preferred_element_type=F32)
    acc = jnp.zeros((rows, LANES), F32)
    for c in range(nk):
        if c + 1 < nk:
            s_ref[(c + 1) % 2] = jnp.dot(q, keys(c + 1), preferred_element_type=F32)
        p = jnp.exp2(s_ref[c % 2]).astype(BF16)
        acc = acc + jnp.dot(p, vals(c), preferred_element_type=F32)
    acc_ref[...] = acc

    @pl.when(jnp.min(acc[:, HEAD_DIM:HEAD_DIM + 1]) < ROW_SUM_MIN)
    def _():
        m_ref[...] = jnp.full(m_ref.shape, -jnp.inf, F32)
        acc_ref[...] = jnp.zeros(acc_ref.shape, F32)

        def step(c, carry):
            s = jnp.dot(q, keys(c), preferred_element_type=F32)
            m_prev = m_ref[...]
            m_new = jnp.maximum(m_prev, jnp.max(s, axis=1, keepdims=True))
            alpha = jnp.exp2(m_prev - m_new)
            p = jnp.exp2(s - jnp.tile(m_new, (1, tk // LANES))).astype(BF16)
            acc_ref[...] = acc_ref[...] * alpha + jnp.dot(p, vals(c), preferred_element_type=F32)
            m_ref[...] = m_new
            return carry

        lax.fori_loop(0, nk, step, 0)

    acc = acc_ref[...]
    o = acc / acc[:, HEAD_DIM:HEAD_DIM + 1]
    lane = lax.broadcasted_iota(jnp.int32, (tq, LANES), 1)
    for j in range(rep // 2):
        a = o[(2 * j) * tq:(2 * j + 1) * tq]
        bb = o[(2 * j + 1) * tq:(2 * j + 2) * tq]
        o_ref[0, :, j * LANES:(j + 1) * LANES] = jnp.where(
            lane < HEAD_DIM, a, pltpu.roll(bb, HEAD_DIM, 1)).astype(BF16)


def _flash(qh, kt, v1, lk, k_off, tq, tk):
    b, _, l, _ = qh.shape
    rep = ATT_HEADS // ATT_KV_HEADS
    kb = k_off // lk
    return pl.pallas_call(
        functools.partial(_flash_body, tq=tq, tk=tk, nk=lk // tk),
        grid=(b, ATT_KV_HEADS, l // tq),
        in_specs=[pl.BlockSpec((1, rep, tq, LANES), lambda bi, g, i: (bi, g, i, 0)),
                  pl.BlockSpec((1, 1, LANES, lk), lambda bi, g, i: (bi, g, 0, kb)),
                  pl.BlockSpec((1, 1, lk, LANES), lambda bi, g, i: (bi, g, kb, 0))],
        out_specs=pl.BlockSpec((1, tq, rep * HEAD_DIM), lambda bi, g, i: (bi, i, g)),
        out_shape=jax.ShapeDtypeStruct((b, l, ATT_HEADS * HEAD_DIM), BF16),
        scratch_shapes=[pltpu.VMEM((2, rep * tq, tk), F32),
                        pltpu.VMEM((rep * tq, LANES), F32), pltpu.VMEM((rep * tq, LANES), F32)],
        compiler_params=_cparams(("parallel", "parallel", "arbitrary")), name="flash_attention",
    )(qh, kt, v1)


def _halo_specs(tm, l, width, col_block):
    r = tm // HALO
    last = l // HALO - 1
    main = pl.BlockSpec((1, tm, width), lambda bi, i: (bi, i, col_block))
    prev = pl.BlockSpec((1, HALO, width), lambda bi, i: (bi, jnp.maximum(i * r - 1, 0), col_block))
    nxt = pl.BlockSpec((1, HALO, width), lambda bi, i: (bi, jnp.minimum((i + 1) * r, last), col_block))
    return [main, prev, nxt]


def _ssdprep_body(xs_ref, xsp_ref, xsn_ref, bc_ref, bcp_ref, bcn_ref, dt_ref,
                  xsc_ref, bcc_ref, dtc_ref, w_ref, b_ref, dtb_ref, ex_ref,
                  xo_ref, bco_ref, dtx_ref, ex_scr, eb_scr, *, nt_lat, tm):
    t = pl.program_id(1)
    is_ctx = t == nt_lat
    has_prev = (t > 0) & (t < nt_lat)
    has_next = t < nt_lat - 1

    def conv_silu(main_ref, prev_ref, next_ref, ctx_ref, scr, lo, hi):
        main = jnp.where(is_ctx, ctx_ref[0], main_ref[0])
        prev = jnp.where(has_prev, prev_ref[0], jnp.zeros_like(prev_ref[0]))
        nxt = jnp.where(has_next, next_ref[0], jnp.zeros_like(next_ref[0]))
        scr[...] = jnp.concatenate([prev, main, nxt], axis=0).astype(F32)
        w = w_ref[:, lo:hi]
        y = (w[0:1] * scr[pl.ds(HALO - 1, tm), :] + w[1:2] * scr[pl.ds(HALO, tm), :]
             + w[2:3] * scr[pl.ds(HALO + 1, tm), :]) + b_ref[:, lo:hi]
        return _silu(y)

    xo_ref[0] = conv_silu(xs_ref, xsp_ref, xsn_ref, xsc_ref, ex_scr, 0, 512).astype(BF16)
    bco_ref[0] = conv_silu(bc_ref, bcp_ref, bcn_ref, bcc_ref, eb_scr, 512, 768).astype(BF16)
    raw = jnp.where(is_ctx, dtc_ref[0], dt_ref[0]).astype(F32) + dtb_ref[...]
    dt = jnp.maximum(raw, 0.0) + jnp.log1p(jnp.exp(-jnp.abs(raw)))
    for d in range(2):
        dtx_ref[0, d] = _split_dot(dt, ex_ref[d])


def _ssdprep(proj, proj_c, conv_w, conv_b, dt_bias_pad, expand):
    b, l, _ = proj.shape
    lc = proj_c.shape[1]
    tm = lc
    nt_lat = l // tm
    ltot = l + lc
    r = tm // HALO
    last = l // HALO - 1
    cl = lambda t: jnp.minimum(t, nt_lat - 1)

    def lat_specs(width, col):
        return [pl.BlockSpec((1, tm, width), lambda bi, t: (bi, cl(t), col)),
                pl.BlockSpec((1, HALO, width), lambda bi, t: (bi, jnp.maximum(cl(t) * r - 1, 0), col)),
                pl.BlockSpec((1, HALO, width), lambda bi, t: (bi, jnp.minimum((cl(t) + 1) * r, last), col))]

    return pl.pallas_call(
        functools.partial(_ssdprep_body, nt_lat=nt_lat, tm=tm), grid=(b, nt_lat + 1),
        in_specs=(lat_specs(512, C_XS // 512) + lat_specs(256, C_BC // 256)
                  + [pl.BlockSpec((1, tm, DT_PAD), lambda bi, t: (bi, cl(t), C_DT // DT_PAD)),
                     pl.BlockSpec((1, tm, 512), lambda bi, t: (bi, 0, C_XS // 512)),
                     pl.BlockSpec((1, tm, 256), lambda bi, t: (bi, 0, C_BC // 256)),
                     pl.BlockSpec((1, tm, DT_PAD), lambda bi, t: (bi, 0, C_DT // DT_PAD)),
                     pl.BlockSpec((3, 768), lambda bi, t: (0, 0)),
                     pl.BlockSpec((1, 768), lambda bi, t: (0, 0)),
                     pl.BlockSpec((1, DT_PAD), lambda bi, t: (0, 0)),
                     pl.BlockSpec((2, DT_PAD, 512), lambda bi, t: (0, 0, 0))]),
        out_specs=[pl.BlockSpec((1, tm, 512), lambda bi, t: (bi, t, 0)),
                   pl.BlockSpec((1, tm, 256), lambda bi, t: (bi, t, 0)),
                   pl.BlockSpec((1, 2, tm, 512), lambda bi, t: (bi, 0, t, 0))],
        out_shape=[jax.ShapeDtypeStruct((b, ltot, 512), BF16),
                   jax.ShapeDtypeStruct((b, ltot, 256), BF16),
                   jax.ShapeDtypeStruct((b, 2, ltot, 512), F32)],
        scratch_shapes=[pltpu.VMEM((tm + 2 * HALO, 512), F32), pltpu.VMEM((tm + 2 * HALO, 256), F32)],
        compiler_params=_cparams(("parallel", "parallel")), name="ssd_prep",
    )(proj, proj, proj, proj, proj, proj, proj, proj_c, proj_c, proj_c,
      conv_w, conv_b, dt_bias_pad, expand)


SSD_STEP_CHUNKS = 2


def _ssd_body(xs_ref, bc_ref, dtx_ref, ar_ref, tri_ref, neg_ref, hm_ref, y_ref, ht_ref):
    t = SSD_CHUNK
    fwd = pl.program_id(1) == 0

    @pl.when(pl.program_id(2) == 0)
    def _():
        ht_ref[...] = jnp.zeros(ht_ref.shape, F32)

    lane = lax.broadcasted_iota(jnp.int32, (t, LANES), 1)
    low = lane < HEAD_DIM
    neg = neg_ref[0]

    def within_chunk(rows):
        xs = xs_ref[0, rows, :].astype(F32)
        dtx = dtx_ref[0, 0, rows, :]
        cum = _dot_split_lhs(tri_ref[0], dtx * ar_ref[0])
        tot = jnp.where(fwd, cum[t - 1:t, :], cum[0:1, :])
        xdt = xs * dtx
        xdt_b = xdt.astype(BF16)
        bc = bc_ref[0, rows, :]
        b_all, c_all = bc[:, :LANES], bc[:, LANES:]
        cb = []
        for g in range(2):
            cg = jnp.where(low if g == 0 else ~low, c_all, jnp.zeros_like(c_all))
            cb.append(lax.dot_general(cg, b_all, (((1,), (1,)), ((), ())), preferred_element_type=F32))
        ys = []
        for j in range(SSD_HEADS // 2):
            cp = cum[:, j * LANES:(j + 1) * LANES]
            cps = pltpu.roll(cp, HEAD_DIM, 1)
            cpt = cp.T
            xp = xdt_b[:, j * LANES:(j + 1) * LANES]
            acc = None
            for e in range(2):
                col = jnp.where(low, cp, cps) if e == 0 else jnp.where(low, cps, cp)
                row = jnp.broadcast_to(cpt[e * HEAD_DIM:e * HEAD_DIM + 1, :], (t, t))
                w = (cb[j // 2] * jnp.exp(col - row + neg)).astype(BF16)
                xm = jnp.where(low if e == 0 else ~low, xp, jnp.zeros_like(xp))
                part = jnp.dot(w, xm, preferred_element_type=F32)
                acc = part if acc is None else acc + part
            ys.append(acc)
        bt = b_all.astype(F32).T.astype(BF16)
        st = jnp.dot(bt, (xdt * jnp.exp(tot - cum)).astype(BF16), preferred_element_type=F32)
        return jnp.concatenate(ys, axis=1), c_all, jnp.exp(cum), jnp.exp(tot), st * hm_ref[...]

    order = [jnp.where(fwd, k, SSD_STEP_CHUNKS - 1 - k) for k in range(SSD_STEP_CHUNKS)]
    rows = [pl.ds(pl.multiple_of(o * t, t), t) for o in order]
    local = [within_chunk(r) for r in rows]
    ht = ht_ref[...]
    for r, (y_diag, c_all, ecum, etot, st) in zip(rows, local):
        y_off = jnp.dot(c_all, ht.astype(BF16), preferred_element_type=F32) * ecum
        y_ref[0, 0, r, :] = (y_diag + y_off).astype(BF16)
        ht = ht * etot + st
    ht_ref[...] = ht


def _ssd_scan(xs, bc, dtx, a_rate_x, tri, neg, hmask, l_lat):
    b, ltot, _ = xs.shape
    t = SSD_CHUNK
    ts = t * SSD_STEP_CHUNKS
    ns = ltot // ts
    nl = l_lat // ts
    nct = ns - nl

    def sid(d, c):
        f = jnp.where(c < nct, nl + c, c - nct)
        r = jnp.where(c < nct, nl + nct - 1 - c, nl - 1 - (c - nct))
        return jnp.where(d == 0, f, r)

    return pl.pallas_call(
        _ssd_body, grid=(b, 2, ns),
        in_specs=[pl.BlockSpec((1, ts, 512), lambda bi, d, c: (bi, sid(d, c), 0)),
                  pl.BlockSpec((1, ts, 256), lambda bi, d, c: (bi, sid(d, c), 0)),
                  pl.BlockSpec((1, 1, ts, 512), lambda bi, d, c: (bi, d, sid(d, c), 0)),
                  pl.BlockSpec((1, 1, 512), lambda bi, d, c: (d, 0, 0)),
                  pl.BlockSpec((1, t, t), lambda bi, d, c: (d, 0, 0)),
                  pl.BlockSpec((1, t, t), lambda bi, d, c: (d, 0, 0)),
                  pl.BlockSpec((LANES, 512), lambda bi, d, c: (0, 0))],
        out_specs=pl.BlockSpec((1, 1, ts, 512), lambda bi, d, c: (bi, d, sid(d, c), 0)),
        out_shape=jax.ShapeDtypeStruct((b, 2, ltot, 512), BF16),
        scratch_shapes=[pltpu.VMEM((LANES, 512), F32)],
        compiler_params=_cparams(("parallel", "parallel", "arbitrary")), name="ssd_scan",
    )(xs, bc, dtx, a_rate_x, tri, neg, hmask)


def _merge_body(gl_ref, bg_ref, cg_ref, cgp_ref, cgn_ref, u_ref, up_ref, un_ref,
                pu_ref, pup_ref, pun_ref, z_ref, att_ref, ys_ref, xs_ref, x_ref, mod_ref,
                cw_ref, band_ref, pw_ref, ps_ref, dsk_ref, sn_ref, wb_ref, wo_ref, o_ref, v_ref, *, tm, l):
    i = pl.program_id(1)
    has_prev = i > 0
    has_next = i < l // tm - 1

    def ext(main_ref, prev_ref, next_ref):
        prev = jnp.where(has_prev, prev_ref[0], jnp.zeros_like(prev_ref[0]))
        nxt = jnp.where(has_next, next_ref[0], jnp.zeros_like(next_ref[0]))
        return jnp.concatenate([prev, main_ref[0], nxt], axis=0)

    v_ref[...] = ext(cg_ref, cgp_ref, cgn_ref).astype(F32) * ext(u_ref, up_ref, un_ref).astype(F32)
    cw = cw_ref[...]
    conv = (cw[0:1] * v_ref[pl.ds(HALO - 1, tm), :] + cw[1:2] * v_ref[pl.ds(HALO, tm), :]
            + cw[2:3] * v_ref[pl.ds(HALO + 1, tm), :])
    br_conv = bg_ref[0].astype(F32) * conv

    pe = ext(pu_ref, pup_ref, pun_ref)
    posm = i * tm + lax.broadcasted_iota(jnp.int32, (tm, LANES), 0)
    pooled = []
    for g, win in enumerate(POOL_WINDOWS):
        cols = slice(g * LANES, (g + 1) * LANES)
        sums = jnp.concatenate(
            [jnp.dot(band_ref[g], pe[k * POOL_BLOCK:k * POOL_BLOCK + POOL_BLOCK + 2 * HALO, cols],
                     preferred_element_type=F32) for k in range(tm // POOL_BLOCK)], axis=0)
        cnt = (jnp.minimum(posm, win // 2) + jnp.minimum(l - posm, win // 2)).astype(F32)
        pm = (sums / cnt - pu_ref[0, :, cols].astype(F32)).astype(BF16)
        pooled.append(jnp.dot(pm, pw_ref[g], preferred_element_type=F32))
    br_pool = jnp.concatenate(pooled, axis=1) * ps_ref[...]

    y = ys_ref[0, 0].astype(F32) + ys_ref[0, 1].astype(F32) + dsk_ref[...] * xs_ref[0].astype(F32)
    gz = y * _silu(z_ref[0].astype(F32))
    br_ssd = gz * lax.rsqrt(jnp.mean(gz * gz, axis=-1, keepdims=True) + NORM_EPS) * sn_ref[...]

    branches = (att_ref[0], br_conv.astype(BF16), br_pool.astype(BF16), br_ssd.astype(BF16))
    acc = None
    for k, br in enumerate(branches):
        gate2 = jnp.tanh(gl_ref[0, :, k * D_MODEL:(k + 1) * D_MODEL].astype(F32)) + 1.0
        term = gate2 * jnp.dot(br, wb_ref[k], preferred_element_type=F32)
        acc = term if acc is None else acc + term
    yo = jnp.dot((0.5 * acc).astype(BF16), wo_ref[...], preferred_element_type=F32)
    o_ref[0] = x_ref[0] + mod_ref[0, 2:3, :] * yo


def _merge(proj, att, ys, xs_c, x, mod, conv_short, band, pool_w, pool_scale, dskip_x, ssd_norm,
           w_branch, w_out, layer, tm, ssd_row_off):
    b, l, _ = x.shape
    per_batch = mod.shape[0] > 1
    off = ssd_row_off // tm
    const2 = lambda bi, i: (0, 0)
    const3 = lambda bi, i: (0, 0, 0)
    in_specs = (
        [pl.BlockSpec((1, tm, 4096), lambda bi, i: (bi, i, 0)),
         pl.BlockSpec((1, tm, 512), lambda bi, i: (bi, i, C_BG // 512))]
        + _halo_specs(tm, l, 512, C_CG // 512) + _halo_specs(tm, l, 512, C_U // 512)
        + _halo_specs(tm, l, 512, C_PU // 512)
        + [pl.BlockSpec((1, tm, 512), lambda bi, i: (bi, i, C_Z // 512)),
           pl.BlockSpec((1, tm, 512), lambda bi, i: (bi, i, 0)),
           pl.BlockSpec((1, 2, tm, 512), lambda bi, i: (bi, 0, i + off, 0)),
           pl.BlockSpec((1, tm, 512), lambda bi, i: (bi, i + off, 0)),
           pl.BlockSpec((1, tm, D_MODEL), lambda bi, i: (bi, i, 0)),
           pl.BlockSpec((1, 6, D_MODEL), lambda bi, i: (bi if per_batch else 0, 0, 0)),
           pl.BlockSpec((3, 512), const2),
           pl.BlockSpec((4, POOL_BLOCK, POOL_BLOCK + 2 * HALO), const3),
           pl.BlockSpec((None, 4, LANES, LANES), lambda bi, i: (layer, 0, 0, 0)),
           pl.BlockSpec((1, 512), const2),
           pl.BlockSpec((1, 512), const2),
           pl.BlockSpec((1, 512), const2),
           pl.BlockSpec((None, N_BRANCH, BRANCH_WIDTH, D_MODEL), lambda bi, i: (layer, 0, 0, 0)),
           pl.BlockSpec((None, D_MODEL, D_MODEL), lambda bi, i: (layer, 0, 0))])
    return pl.pallas_call(
        functools.partial(_merge_body, tm=tm, l=l), grid=(b, l // tm),
        in_specs=in_specs,
        out_specs=pl.BlockSpec((1, tm, D_MODEL), lambda bi, i: (bi, i, 0)),
        out_shape=jax.ShapeDtypeStruct((b, l, D_MODEL), F32),
        scratch_shapes=[pltpu.VMEM((tm + 2 * HALO, 512), F32)],
        compiler_params=_cparams(("parallel", "parallel")), name="merge",
    )(proj, proj, proj, proj, proj, proj, proj, proj, proj, proj, proj, proj,
      att, ys, xs_c, x, mod, conv_short, band, pool_w, pool_scale, dskip_x, ssd_norm, w_branch, w_out)


FFN_HALO = 8
FFN_CHUNK = 256


def _ffn_body(x_ref, xp_ref, xn_ref, mod_ref, g_ref, wu_ref, cw_ref, wd_ref, fn_ref,
              o_ref, h_ref, ug_ref, uv_ref, a_ref, *, tm, l, final):
    i = pl.program_id(1)
    n_ext = tm + 2 * FFN_HALO
    pos = i * tm - FFN_HALO + lax.broadcasted_iota(jnp.int32, (n_ext, D_MODEL), 0)
    xe = jnp.concatenate([xp_ref[0], x_ref[0], xn_ref[0]], axis=0)
    h = _rms_mod(xe, g_ref[...], mod_ref[0, 4:5, :], mod_ref[0, 3:4, :])
    h_ref[...] = jnp.where((pos >= 0) & (pos < l), h, 0.0).astype(BF16)

    def conv3(u_ref, s, w):
        return (w[0:1] * u_ref[s, pl.ds(FFN_HALO - 1, tm), :] + w[1:2] * u_ref[s, pl.ds(FFN_HALO, tm), :]
                + w[2:3] * u_ref[s, pl.ds(FFN_HALO + 1, tm), :])

    for j in range(D_FF // FFN_CHUNK):
        s = j % 2
        cg = slice(j * FFN_CHUNK, (j + 1) * FFN_CHUNK)
        cv = slice(D_FF + j * FFN_CHUNK, D_FF + (j + 1) * FFN_CHUNK)
        ug_ref[s] = jnp.dot(h_ref[...], wu_ref[:, cg], preferred_element_type=F32)
        uv_ref[s] = jnp.dot(h_ref[...], wu_ref[:, cv], preferred_element_type=F32)
        a_ref[:, cg] = (_silu(conv3(ug_ref, s, cw_ref[:, cg]))
                        * conv3(uv_ref, s, cw_ref[:, cv])).astype(BF16)

    out = x_ref[0] + mod_ref[0, 5:6, :] * jnp.dot(a_ref[...], wd_ref[...], preferred_element_type=F32)
    if final:
        out = out * lax.rsqrt(jnp.mean(out * out, axis=-1, keepdims=True) + NORM_EPS) * fn_ref[...]
    o_ref[0] = out


def _ffn(x, mod, g, w_up, ffn_conv, w_down, final_norm, layer, tm, final):
    b, l, _ = x.shape
    per_batch = mod.shape[0] > 1
    r = tm // FFN_HALO
    last = l // FFN_HALO - 1
    n_ext = tm + 2 * FFN_HALO
    const = lambda bi, i: (0, 0)
    resident = pl.Buffered(1)
    return pl.pallas_call(
        functools.partial(_ffn_body, tm=tm, l=l, final=final), grid=(b, l // tm),
        in_specs=[pl.BlockSpec((1, tm, D_MODEL), lambda bi, i: (bi, i, 0)),
                  pl.BlockSpec((1, FFN_HALO, D_MODEL), lambda bi, i: (bi, jnp.maximum(i * r - 1, 0), 0)),
                  pl.BlockSpec((1, FFN_HALO, D_MODEL), lambda bi, i: (bi, jnp.minimum((i + 1) * r, last), 0)),
                  pl.BlockSpec((1, 6, D_MODEL), lambda bi, i: (bi if per_batch else 0, 0, 0)),
                  pl.BlockSpec((1, D_MODEL), const),
                  pl.BlockSpec((None, D_MODEL, 2 * D_FF), lambda bi, i: (layer, 0, 0), pipeline_mode=resident),
                  pl.BlockSpec((3, 2 * D_FF), const, pipeline_mode=resident),
                  pl.BlockSpec((None, D_FF, D_MODEL), lambda bi, i: (layer, 0, 0), pipeline_mode=resident),
                  pl.BlockSpec((1, D_MODEL), const)],
        out_specs=pl.BlockSpec((1, tm, D_MODEL), lambda bi, i: (bi, i, 0)),
        out_shape=jax.ShapeDtypeStruct((b, l, D_MODEL), F32),
        scratch_shapes=[pltpu.VMEM((n_ext, D_MODEL), BF16), pltpu.VMEM((2, n_ext, FFN_CHUNK), F32),
                        pltpu.VMEM((2, n_ext, FFN_CHUNK), F32), pltpu.VMEM((tm, D_FF), BF16)],
        compiler_params=_cparams(("parallel", "parallel")), name="conv_ffn",
    )(x, x, x, mod, g.reshape(1, D_MODEL), w_up, ffn_conv, w_down, final_norm.reshape(1, D_MODEL))


def _rope_tables(n_tokens):
    rows = n_tokens // GRID_W
    row = jnp.repeat(jnp.arange(rows), GRID_W).astype(F32)
    col = jnp.tile(jnp.arange(GRID_W), rows).astype(F32)
    inv = ROPE_THETA ** (-jnp.arange(ROPE_FREQS, dtype=F32) / ROPE_FREQS)
    ang_r, ang_c = row[:, None] * inv, col[:, None] * inv
    ang = jnp.concatenate([ang_r, ang_r, ang_c, ang_c], axis=1)
    sign = jnp.tile(jnp.concatenate([-jnp.ones(ROPE_FREQS), jnp.ones(ROPE_FREQS)]), 2).astype(F32)
    cos = jnp.tile(jnp.cos(ang), (1, 2))
    sin = jnp.tile(jnp.sin(ang) * sign, (1, 2))
    return cos, sin


def _identity_tables(n):
    return jnp.ones((n, LANES), F32), jnp.zeros((n, LANES), F32)


def _cat_weights(w_in):
    o = np.cumsum([0, 512, 128, 128, 512, 512, 512, 512, 512, 768, 16, 4096])
    q, k, v, bg, cg, u, pu, z, xbc, dt, gl = [w_in[..., o[i]:o[i + 1]] for i in range(11)]
    dt = jnp.pad(dt, ((0, 0), (0, 0), (0, DT_PAD - dt.shape[-1])))
    gl = 0.5 * gl
    return jnp.concatenate([gl, bg, cg, u, pu, z, q, xbc[..., :512], xbc[..., 512:], k, v, dt],
                           axis=-1).astype(BF16)


def kernel(x, c, ctx, c_ctx, w_mod, b_mod, norm_mix, w_in, q_norm, k_norm, conv_short, pool_w, pool_scale,
           ssd_conv_w, ssd_conv_b, ssd_dt_bias, ssd_a_log, ssd_d, ssd_norm, w_branch, w_out, norm_ffn,
           w_up, ffn_conv, w_down, final_norm):
    depth = w_mod.shape[0]
    b, l, _ = x.shape
    lc = ctx.shape[1]
    ltot = l + lc
    t = SSD_CHUNK

    tm_in = min(1024, l)
    tm_mid = min(512, l)
    tq = 256
    tk_lat = 768 if ltot % 768 == 0 else lc
    tk_ctx = lc

    cos_l, sin_l = _rope_tables(l)
    cos_c, sin_c = _identity_tables(lc)
    cos_all = jnp.concatenate([cos_l, cos_c], axis=0)
    sin_all = jnp.concatenate([sin_l, sin_c], axis=0)
    lane = np.arange(LANES)
    bd = jnp.asarray((lane[:, None] // HEAD_DIM) == (lane[None, :] // HEAD_DIM), BF16)
    col512 = np.arange(512)
    expand = np.zeros((2, DT_PAD, 512), np.float32)
    for d in range(2):
        expand[d, d * SSD_HEADS + col512 // HEAD_DIM, col512] = 1.0
    expand = jnp.asarray(expand, BF16)
    pos = np.arange(t)
    tri = jnp.asarray(np.stack([pos[:, None] >= pos[None, :], pos[:, None] <= pos[None, :]]), BF16)
    neg = jnp.asarray(np.where(np.stack([pos[:, None] >= pos[None, :], pos[:, None] <= pos[None, :]]),
                               0.0, -np.inf), F32)
    brow, bcol = np.arange(POOL_BLOCK)[:, None], np.arange(POOL_BLOCK + 2 * HALO)[None, :]
    band = jnp.asarray(np.stack([(bcol >= HALO + brow - w // 2) & (bcol < HALO + brow + w // 2)
                                 for w in POOL_WINDOWS]), BF16)
    hmask = jnp.asarray((lane[:, None] // SSD_STATE) == (col512[None, :] // 256), F32)

    cond8 = jnp.zeros((8, D_MODEL), F32).at[:b].set(c).at[b].set(c_ctx)
    mod_all = _modulation(cond8, w_mod, b_mod).reshape(depth, 8, 6, D_MODEL)
    w_cat = _cat_weights(w_in)
    wb, wo, pw = w_branch.astype(BF16), w_out.astype(BF16), pool_w.astype(BF16)
    wu, wd = w_up.astype(BF16), w_down.astype(BF16)
    xc = ctx
    for i in range(depth):
        last = i == depth - 1
        mod_l, mod_c = mod_all[i, :b], mod_all[i, b:b + 1]
        gq = (jnp.tile(q_norm[i], 2) * QK_SCALE).reshape(1, LANES)
        gk = jnp.tile(k_norm[i], 2).reshape(1, LANES)
        neg_bound = jnp.full((1, LANES), -HEAD_DIM * QK_SCALE, F32) * (jnp.max(jnp.abs(q_norm[i]))
                                                                  * jnp.max(jnp.abs(k_norm[i])))
        a_rate_x = jnp.repeat(-jnp.exp(ssd_a_log[i].astype(F32)), HEAD_DIM, axis=1).reshape(2, 1, 512)
        dskip_x = jnp.repeat(ssd_d[i], HEAD_DIM).reshape(1, 512)
        dt_bias_pad = jnp.pad(ssd_dt_bias[i].reshape(1, 2 * SSD_HEADS), ((0, 0), (0, DT_PAD - 2 * SSD_HEADS)))
        ps = pool_scale[i].reshape(1, 512)
        sn = ssd_norm[i].reshape(1, 512)
        scb = ssd_conv_b[i].reshape(1, 768)

        proj = _inproj(x, mod_l, norm_mix[i], w_cat, i, tm_in)
        proj_c = _inproj(xc, mod_c, norm_mix[i], w_cat, i, lc)

        qh = _qprep(proj, cos_l, sin_l, gq, neg_bound, bd, tm_mid)
        kt, v1 = _kvprep(proj, proj_c, cos_all, sin_all, gk, bd)
        att = _flash(qh, kt, v1, ltot, 0, tq, tk_lat)

        xs_s, bc_s, dtx = _ssdprep(proj, proj_c, ssd_conv_w[i], scb, dt_bias_pad, expand)
        ys = _ssd_scan(xs_s, bc_s, dtx, a_rate_x, tri, neg, hmask, l)

        x_mid = _merge(proj, att, ys, xs_s, x, mod_l, conv_short[i], band, pw, ps, dskip_x, sn, wb, wo, i, tm_mid, 0)
        if not last:
            qh_c = _qprep(proj_c, cos_c, sin_c, gq, neg_bound, bd, lc)
            att_c = _flash(qh_c, kt, v1, lc, l, tq, tk_ctx)
            xc_mid = _merge(proj_c, att_c, ys, xs_s, xc, mod_c, conv_short[i], band, pw, ps, dskip_x, sn, wb, wo, i, lc, l)
            xc = _ffn(xc_mid, mod_c, norm_ffn[i], wu, ffn_conv[i], wd, final_norm, i, lc, False)
        x = _ffn(x_mid, mod_l, norm_ffn[i], wu, ffn_conv[i], wd, final_norm, i, tm_mid, last)
    return x
```

```python
import functools
import math

import jax
import jax.numpy as jnp
import numpy as np
from jax import lax
from jax.experimental import pallas as pl
from jax.experimental.pallas import tpu as pltpu

F32 = jnp.float32
BF16 = jnp.bfloat16

D_MODEL = 1024
GRID_W = 64
NORM_EPS = 1e-6
ATT_HEADS = 8
ATT_KV_HEADS = 2
HEAD_DIM = 64
ROPE_THETA = 10000.0
ROPE_FREQS = HEAD_DIM // 4
POOL_WINDOWS = (2, 4, 8, 16)
SSD_HEADS = 8
SSD_STATE = 64
SSD_CHUNK = 128
N_BRANCH = 4
BRANCH_WIDTH = 512
D_FF = 2816

C_GL, C_BG, C_CG, C_U, C_PU, C_Z, C_Q, C_XS, C_BC, C_K, C_V, C_DT = (
    0, 4096, 4608, 5120, 5632, 6144, 6656, 7168, 7680, 7936, 8064, 8192)
N_CAT = 8448
DT_PAD = 256

HALO = 16
POOL_BLOCK = 128
LANES = 128
QK_SCALE = HEAD_DIM ** -0.5 * math.log2(math.e)
ROW_SUM_MIN = 2.0 ** -47
VMEM_LIMIT = 56 * 1024 * 1024


def _cparams(sem):
    return pltpu.CompilerParams(dimension_semantics=sem, vmem_limit_bytes=VMEM_LIMIT)


def _silu(x):
    hx = 0.5 * x
    return hx * (jnp.tanh(hx) + 1.0)


def _split_dot(a, b_bf16):
    hi = a.astype(BF16)
    lo = (a - hi.astype(F32)).astype(BF16)
    return (jnp.dot(hi, b_bf16, preferred_element_type=F32)
            + jnp.dot(lo, b_bf16, preferred_element_type=F32))


def _dot_split_lhs(a_bf16, b):
    hi = b.astype(BF16)
    lo = (b - hi.astype(F32)).astype(BF16)
    return (jnp.dot(a_bf16, hi, preferred_element_type=F32)
            + jnp.dot(a_bf16, lo, preferred_element_type=F32))


def _rms_mod(x, g, scale, shift):
    ms = jnp.mean(x * x, axis=-1, keepdims=True)
    y = x * lax.rsqrt(ms + NORM_EPS) * g
    return y * (1.0 + scale) + shift


def _mod_body(c_ref, w_ref, b_ref, o_ref):
    s = _silu(c_ref[...])
    o_ref[...] = jnp.dot(s, w_ref[...], precision=lax.Precision.HIGHEST,
                         preferred_element_type=F32) + b_ref[...]


def _modulation(cond8, w_mod, b_mod):
    depth, _, n = w_mod.shape
    tn = 1536
    return pl.pallas_call(
        _mod_body, grid=(depth, n // tn),
        in_specs=[pl.BlockSpec((8, D_MODEL), lambda d, j: (0, 0)),
                  pl.BlockSpec((None, D_MODEL, tn), lambda d, j: (d, 0, j)),
                  pl.BlockSpec((None, 1, tn), lambda d, j: (d, 0, j))],
        out_specs=pl.BlockSpec((None, 8, tn), lambda d, j: (d, 0, j)),
        out_shape=jax.ShapeDtypeStruct((depth, 8, n), F32),
        compiler_params=_cparams(("arbitrary", "arbitrary")), name="modulation",
    )(cond8, w_mod, b_mod.reshape(depth, 1, n))


def _inproj_body(x_ref, mod_ref, g_ref, w_ref, o_ref, h_ref):
    @pl.when(pl.program_id(2) == 0)
    def _():
        h = _rms_mod(x_ref[0], g_ref[...], mod_ref[0, 1:2, :], mod_ref[0, 0:1, :])
        h_ref[...] = h.astype(BF16)

    o_ref[0] = lax.dot_general(h_ref[...], w_ref[...], (((1,), (1,)), ((), ())),
                               preferred_element_type=F32).astype(BF16)


def _inproj(x, mod, g, w_cat, layer, tm):
    b, l, _ = x.shape
    n = w_cat.shape[1]
    tn = 2816
    per_batch = mod.shape[0] > 1
    return pl.pallas_call(
        _inproj_body, grid=(b, l // tm, n // tn),
        in_specs=[pl.BlockSpec((1, tm, D_MODEL), lambda bi, i, j: (bi, i, 0)),
                  pl.BlockSpec((1, 6, D_MODEL), lambda bi, i, j: (bi if per_batch else 0, 0, 0)),
                  pl.BlockSpec((1, D_MODEL), lambda bi, i, j: (0, 0)),
                  pl.BlockSpec((None, tn, D_MODEL), lambda bi, i, j: (layer, j, 0))],
        out_specs=pl.BlockSpec((1, tm, tn), lambda bi, i, j: (bi, i, j)),
        out_shape=jax.ShapeDtypeStruct((b, l, n), BF16),
        scratch_shapes=[pltpu.VMEM((tm, D_MODEL), BF16)],
        compiler_params=_cparams(("parallel", "parallel", "arbitrary")), name="inproj",
    )(x, mod, g.reshape(1, D_MODEL), w_cat)


def _head_norm_rope(t, cos, sin, gain, bd):
    ss = _split_dot(t * t, bd)
    y = t * lax.rsqrt(ss * (1.0 / HEAD_DIM) + NORM_EPS) * gain
    lane = lax.broadcasted_iota(jnp.int32, y.shape, 1)
    partner = jnp.where((lane % 32) < 16, pltpu.roll(y, LANES - 16, 1), pltpu.roll(y, 16, 1))
    return y * cos + partner * sin


def _qprep_body(q_ref, cos_ref, sin_ref, g_ref, negb_ref, bd_ref, o_ref):
    cos, sin, gain, bd = cos_ref[...], sin_ref[...], g_ref[...], bd_ref[...]
    lane = lax.broadcasted_iota(jnp.int32, cos.shape, 1)
    for c in range(ATT_HEADS // 2):
        t = q_ref[0, :, c * LANES:(c + 1) * LANES].astype(F32)
        r = _head_norm_rope(t, cos, sin, gain, bd)
        tail = jnp.where(lane == HEAD_DIM, negb_ref[...], 0.0)
        o_ref[0, 2 * c] = jnp.where(lane < HEAD_DIM, r, tail).astype(BF16)
        o_ref[0, 2 * c + 1] = jnp.where(lane < HEAD_DIM, pltpu.roll(r, HEAD_DIM, 1), tail).astype(BF16)


def _qprep(proj, cos, sin, gain, neg_bound, bd, tm):
    b, l, _ = proj.shape
    return pl.pallas_call(
        _qprep_body, grid=(b, l // tm),
        in_specs=[pl.BlockSpec((1, tm, 512), lambda bi, i: (bi, i, C_Q // 512)),
                  pl.BlockSpec((tm, LANES), lambda bi, i: (i, 0)),
                  pl.BlockSpec((tm, LANES), lambda bi, i: (i, 0)),
                  pl.BlockSpec((1, LANES), lambda bi, i: (0, 0)),
                  pl.BlockSpec((1, LANES), lambda bi, i: (0, 0)),
                  pl.BlockSpec((LANES, LANES), lambda bi, i: (0, 0))],
        out_specs=pl.BlockSpec((1, ATT_HEADS, tm, LANES), lambda bi, i: (bi, 0, i, 0)),
        out_shape=jax.ShapeDtypeStruct((b, ATT_HEADS, l, LANES), BF16),
        compiler_params=_cparams(("parallel", "parallel")), name="q_prep",
    )(proj, cos, sin, gain, neg_bound, bd)


def _kvprep_body(kl_ref, vl_ref, kc_ref, vc_ref, cos_ref, sin_ref, g_ref, bd_ref, kt_ref, v1_ref, *, nt_lat):
    is_ctx = pl.program_id(1) == nt_lat
    k = jnp.where(is_ctx, kc_ref[0], kl_ref[0]).astype(F32)
    v = jnp.where(is_ctx, vc_ref[0], vl_ref[0]).astype(F32)
    r = _head_norm_rope(k, cos_ref[...], sin_ref[...], g_ref[...], bd_ref[...])
    lane = lax.broadcasted_iota(jnp.int32, r.shape, 1)
    ones_col = jnp.where(lane == HEAD_DIM, 1.0, 0.0)
    for g in range(ATT_KV_HEADS):
        rg = r if g == 0 else pltpu.roll(r, HEAD_DIM, 1)
        vg = v if g == 0 else pltpu.roll(v, HEAD_DIM, 1)
        kt_ref[0, g] = jnp.where(lane < HEAD_DIM, rg, ones_col).T.astype(BF16)
        v1_ref[0, g] = jnp.where(lane < HEAD_DIM, vg, ones_col).astype(BF16)


def _kvprep(proj, proj_c, cos, sin, gain, bd):
    b, l, _ = proj.shape
    lc = proj_c.shape[1]
    tm = lc
    nt_lat = l // tm
    ltot = l + lc
    lat = lambda bi, t: (bi, jnp.minimum(t, nt_lat - 1))
    return pl.pallas_call(
        functools.partial(_kvprep_body, nt_lat=nt_lat), grid=(b, nt_lat + 1),
        in_specs=[pl.BlockSpec((1, tm, LANES), lambda bi, t: lat(bi, t) + (C_K // LANES,)),
                  pl.BlockSpec((1, tm, LANES), lambda bi, t: lat(bi, t) + (C_V // LANES,)),
                  pl.BlockSpec((1, tm, LANES), lambda bi, t: (bi, 0, C_K // LANES)),
                  pl.BlockSpec((1, tm, LANES), lambda bi, t: (bi, 0, C_V // LANES)),
                  pl.BlockSpec((tm, LANES), lambda bi, t: (t, 0)),
                  pl.BlockSpec((tm, LANES), lambda bi, t: (t, 0)),
                  pl.BlockSpec((1, LANES), lambda bi, t: (0, 0)),
                  pl.BlockSpec((LANES, LANES), lambda bi, t: (0, 0))],
        out_specs=[pl.BlockSpec((1, ATT_KV_HEADS, LANES, tm), lambda bi, t: (bi, 0, 0, t)),
                   pl.BlockSpec((1, ATT_KV_HEADS, tm, LANES), lambda bi, t: (bi, 0, t, 0))],
        out_shape=[jax.ShapeDtypeStruct((b, ATT_KV_HEADS, LANES, ltot), BF16),
                   jax.ShapeDtypeStruct((b, ATT_KV_HEADS, ltot, LANES), BF16)],
        compiler_params=_cparams(("parallel", "parallel")), name="kv_prep",
    )(proj, proj, proj_c, proj_c, cos, sin, gain, bd)


def _flash_body(q_ref, kt_ref, v_ref, o_ref, s_ref, acc_ref, m_ref, *, tq, tk, nk):
    rep = ATT_HEADS // ATT_KV_HEADS
    rows = rep * tq
    q = q_ref[0].reshape(rows, LANES)

    def keys(c):
        return kt_ref[0, 0, :, pl.ds(pl.multiple_of(c * tk, tk), tk)]

    def vals(c):
        return v_ref[0, 0, pl.ds(pl.multiple_of(c * tk, tk), tk), :]

    s_ref[0] = jnp.dot(q, keys(0), preferred_element_type=F32)
    acc = jnp.zeros((rows, LANES), F32)
    for c in range(nk):
        if c + 1 < nk:
            s_ref[(c + 1) % 2] = jnp.dot(q, keys(c + 1), preferred_element_type=F32)
        p = jnp.exp2(s_ref[c % 2]).astype(BF16)
        acc = acc + jnp.dot(p, vals(c), preferred_element_type=F32)
    acc_ref[...] = acc

    @pl.when(jnp.min(acc[:, HEAD_DIM:HEAD_DIM + 1]) < ROW_SUM_MIN)
    def _():
        m_ref[...] = jnp.full(m_ref.shape, -jnp.inf, F32)
        acc_ref[...] = jnp.zeros(acc_ref.shape, F32)

        def step(c, carry):
            s = jnp.dot(q, keys(c), preferred_element_type=F32)
            m_prev = m_ref[...]
            m_new = jnp.maximum(m_prev, jnp.max(s, axis=1, keepdims=True))
            alpha = jnp.exp2(m_prev - m_new)
            p = jnp.exp2(s - jnp.tile(m_new, (1, tk // LANES))).astype(BF16)
            acc_ref[...] = acc_ref[...] * alpha + jnp.dot(p, vals(c), preferred_element_type=F32)
            m_ref[...] = m_new
            return carry

        lax.fori_loop(0, nk, step, 0)

    acc = acc_ref[...]
    o = acc / acc[:, HEAD_DIM:HEAD_DIM + 1]
    lane = lax.broadcasted_iota(jnp.int32, (tq, LANES), 1)
    for j in range(rep // 2):
        a = o[(2 * j) * tq:(2 * j + 1) * tq]
        bb = o[(2 * j + 1) * tq:(2 * j + 2) * tq]
        o_ref[0, :, j * LANES:(j + 1) * LANES] = jnp.where(
            lane < HEAD_DIM, a, pltpu.roll(bb, HEAD_DIM, 1)).astype(BF16)


def _flash(qh, kt, v1, lk, k_off, tq, tk):
    b, _, l, _ = qh.shape
    rep = ATT_HEADS // ATT_KV_HEADS
    kb = k_off // lk
    return pl.pallas_call(
        functools.partial(_flash_body, tq=tq, tk=tk, nk=lk // tk),
        grid=(b, ATT_KV_HEADS, l // tq),
        in_specs=[pl.BlockSpec((1, rep, tq, LANES), lambda bi, g, i: (bi, g, i, 0)),
                  pl.BlockSpec((1, 1, LANES, lk), lambda bi, g, i: (bi, g, 0, kb)),
                  pl.BlockSpec((1, 1, lk, LANES), lambda bi, g, i: (bi, g, kb, 0))],
        out_specs=pl.BlockSpec((1, tq, rep * HEAD_DIM), lambda bi, g, i: (bi, i, g)),
        out_shape=jax.ShapeDtypeStruct((b, l, ATT_HEADS * HEAD_DIM), BF16),
        scratch_shapes=[pltpu.VMEM((2, rep * tq, tk), F32),
                        pltpu.VMEM((rep * tq, LANES), F32), pltpu.VMEM((rep * tq, LANES), F32)],
        compiler_params=_cparams(("parallel", "parallel", "arbitrary")), name="flash_attention",
    )(qh, kt, v1)


def _halo_specs(tm, l, width, col_block):
    r = tm // HALO
    last = l // HALO - 1
    main = pl.BlockSpec((1, tm, width), lambda bi, i: (bi, i, col_block))
    prev = pl.BlockSpec((1, HALO, width), lambda bi, i: (bi, jnp.maximum(i * r - 1, 0), col_block))
    nxt = pl.BlockSpec((1, HALO, width), lambda bi, i: (bi, jnp.minimum((i + 1) * r, last), col_block))
    return [main, prev, nxt]


def _ssdprep_body(xs_ref, xsp_ref, xsn_ref, bc_ref, bcp_ref, bcn_ref, dt_ref,
                  xsc_ref, bcc_ref, dtc_ref, w_ref, b_ref, dtb_ref, ex_ref,
                  xo_ref, bco_ref, dtx_ref, ex_scr, eb_scr, *, nt_lat, tm):
    t = pl.program_id(1)
    is_ctx = t == nt_lat
    has_prev = (t > 0) & (t < nt_lat)
    has_next = t < nt_lat - 1

    def conv_silu(main_ref, prev_ref, next_ref, ctx_ref, scr, lo, hi):
        main = jnp.where(is_ctx, ctx_ref[0], main_ref[0])
        prev = jnp.where(has_prev, prev_ref[0], jnp.zeros_like(prev_ref[0]))
        nxt = jnp.where(has_next, next_ref[0], jnp.zeros_like(next_ref[0]))
        scr[...] = jnp.concatenate([prev, main, nxt], axis=0).astype(F32)
        w = w_ref[:, lo:hi]
        y = (w[0:1] * scr[pl.ds(HALO - 1, tm), :] + w[1:2] * scr[pl.ds(HALO, tm), :]
             + w[2:3] * scr[pl.ds(HALO + 1, tm), :]) + b_ref[:, lo:hi]
        return _silu(y)

    xo_ref[0] = conv_silu(xs_ref, xsp_ref, xsn_ref, xsc_ref, ex_scr, 0, 512).astype(BF16)
    bco_ref[0] = conv_silu(bc_ref, bcp_ref, bcn_ref, bcc_ref, eb_scr, 512, 768).astype(BF16)
    raw = jnp.where(is_ctx, dtc_ref[0], dt_ref[0]).astype(F32) + dtb_ref[...]
    dt = jnp.maximum(raw, 0.0) + jnp.log1p(jnp.exp(-jnp.abs(raw)))
    for d in range(2):
        dtx_ref[0, d] = _split_dot(dt, ex_ref[d])


def _ssdprep(proj, proj_c, conv_w, conv_b, dt_bias_pad, expand):
    b, l, _ = proj.shape
    lc = proj_c.shape[1]
    tm = lc
    nt_lat = l // tm
    ltot = l + lc
    r = tm // HALO
    last = l // HALO - 1
    cl = lambda t: jnp.minimum(t, nt_lat - 1)

    def lat_specs(width, col):
        return [pl.BlockSpec((1, tm, width), lambda bi, t: (bi, cl(t), col)),
                pl.BlockSpec((1, HALO, width), lambda bi, t: (bi, jnp.maximum(cl(t) * r - 1, 0), col)),
                pl.BlockSpec((1, HALO, width), lambda bi, t: (bi, jnp.minimum((cl(t) + 1) * r, last), col))]

    return pl.pallas_call(
        functools.partial(_ssdprep_body, nt_lat=nt_lat, tm=tm), grid=(b, nt_lat + 1),
        in_specs=(lat_specs(512, C_XS // 512) + lat_specs(256, C_BC // 256)
                  + [pl.BlockSpec((1, tm, DT_PAD), lambda bi, t: (bi, cl(t), C_DT // DT_PAD)),
                     pl.BlockSpec((1, tm, 512), lambda bi, t: (bi, 0, C_XS // 512)),
                     pl.BlockSpec((1, tm, 256), lambda bi, t: (bi, 0, C_BC // 256)),
                     pl.BlockSpec((1, tm, DT_PAD), lambda bi, t: (bi, 0, C_DT // DT_PAD)),
                     pl.BlockSpec((3, 768), lambda bi, t: (0, 0)),
                     pl.BlockSpec((1, 768), lambda bi, t: (0, 0)),
                     pl.BlockSpec((1, DT_PAD), lambda bi, t: (0, 0)),
                     pl.BlockSpec((2, DT_PAD, 512), lambda bi, t: (0, 0, 0))]),
        out_specs=[pl.BlockSpec((1, tm, 512), lambda bi, t: (bi, t, 0)),
                   pl.BlockSpec((1, tm, 256), lambda bi, t: (bi, t, 0)),
                   pl.BlockSpec((1, 2, tm, 512), lambda bi, t: (bi, 0, t, 0))],
        out_shape=[jax.ShapeDtypeStruct((b, ltot, 512), BF16),
                   jax.ShapeDtypeStruct((b, ltot, 256), BF16),
                   jax.ShapeDtypeStruct((b, 2, ltot, 512), F32)],
        scratch_shapes=[pltpu.VMEM((tm + 2 * HALO, 512), F32), pltpu.VMEM((tm + 2 * HALO, 256), F32)],
        compiler_params=_cparams(("parallel", "parallel")), name="ssd_prep",
    )(proj, proj, proj, proj, proj, proj, proj, proj_c, proj_c, proj_c,
      conv_w, conv_b, dt_bias_pad, expand)


SSD_STEP_CHUNKS = 2


def _ssd_body(xsf_ref, bcf_ref, dtf_ref, xsb_ref, bcb_ref, dtb_ref, ar_ref, tri_ref, neg_ref, hm_ref,
              yf_ref, yb_ref, htf_ref, htb_ref):
    t = SSD_CHUNK

    @pl.when(pl.program_id(1) == 0)
    def _():
        htf_ref[...] = jnp.zeros(htf_ref.shape, F32)
        htb_ref[...] = jnp.zeros(htb_ref.shape, F32)

    lane = lax.broadcasted_iota(jnp.int32, (t, LANES), 1)
    low = lane < HEAD_DIM

    def within_chunk(d, xs_ref, bc_ref, dtx_ref, rows):
        xs = xs_ref[0, rows, :].astype(F32)
        dtx = dtx_ref[0, 0, rows, :]
        cum = _dot_split_lhs(tri_ref[d], dtx * ar_ref[d])
        tot = cum[t - 1:t, :] if d == 0 else cum[0:1, :]
        neg = neg_ref[d]
        xdt = xs * dtx
        xdt_b = xdt.astype(BF16)
        bc = bc_ref[0, rows, :]
        b_all, c_all = bc[:, :LANES], bc[:, LANES:]
        cb = []
        for g in range(2):
            cg = jnp.where(low if g == 0 else ~low, c_all, jnp.zeros_like(c_all))
            cb.append(lax.dot_general(cg, b_all, (((1,), (1,)), ((), ())), preferred_element_type=F32))
        ys = []
        for j in range(SSD_HEADS // 2):
            cp = cum[:, j * LANES:(j + 1) * LANES]
            cps = pltpu.roll(cp, HEAD_DIM, 1)
            cpt = cp.T
            xp = xdt_b[:, j * LANES:(j + 1) * LANES]
            acc = None
            for e in range(2):
                col = jnp.where(low, cp, cps) if e == 0 else jnp.where(low, cps, cp)
                row = jnp.broadcast_to(cpt[e * HEAD_DIM:e * HEAD_DIM + 1, :], (t, t))
                w = (cb[j // 2] * jnp.exp2(col - row + neg)).astype(BF16)
                xm = jnp.where(low if e == 0 else ~low, xp, jnp.zeros_like(xp))
                part = jnp.dot(w, xm, preferred_element_type=F32)
                acc = part if acc is None else acc + part
            ys.append(acc)
        bt = b_all.astype(F32).T.astype(BF16)
        st = jnp.dot(bt, (xdt * jnp.exp2(tot - cum)).astype(BF16), preferred_element_type=F32)
        return jnp.concatenate(ys, axis=1), c_all, jnp.exp2(cum), jnp.exp2(tot), st * hm_ref[...]

    dirs = ((xsf_ref, bcf_ref, dtf_ref, yf_ref, htf_ref), (xsb_ref, bcb_ref, dtb_ref, yb_ref, htb_ref))
    local = []
    for d, (xs_ref, bc_ref, dtx_ref, _, _) in enumerate(dirs):
        order = range(SSD_STEP_CHUNKS) if d == 0 else range(SSD_STEP_CHUNKS - 1, -1, -1)
        rows = [pl.ds(o * t, t) for o in order]
        local.append((rows, [within_chunk(d, xs_ref, bc_ref, dtx_ref, r) for r in rows]))
    for (_, _, _, y_ref, ht_ref), (rows, chunks) in zip(dirs, local):
        ht = ht_ref[...]
        for r, (y_diag, c_all, ecum, etot, st) in zip(rows, chunks):
            y_off = jnp.dot(c_all, ht.astype(BF16), preferred_element_type=F32) * ecum
            y_ref[0, r, :] = (y_diag + y_off).astype(BF16)
            ht = ht * etot + st
        ht_ref[...] = ht


def _ssd_scan(xs, bc, dtx, a_rate_x, tri, neg, hmask, l_lat):
    b, ltot, _ = xs.shape
    t = SSD_CHUNK
    ts = t * SSD_STEP_CHUNKS
    ns = ltot // ts
    nl = l_lat // ts
    nct = ns - nl

    fwd = lambda c: jnp.where(c < nct, nl + c, c - nct)
    bwd = lambda c: jnp.where(c < nct, nl + nct - 1 - c, nl - 1 - (c - nct))
    whole = lambda bi, c: (0, 0, 0)

    def specs(sid, d):
        return [pl.BlockSpec((1, ts, 512), lambda bi, c: (bi, sid(c), 0)),
                pl.BlockSpec((1, ts, 256), lambda bi, c: (bi, sid(c), 0)),
                pl.BlockSpec((1, 1, ts, 512), lambda bi, c: (bi, d, sid(c), 0))]

    return pl.pallas_call(
        _ssd_body, grid=(b, ns),
        in_specs=specs(fwd, 0) + specs(bwd, 1) + [
            pl.BlockSpec((2, 1, 512), whole), pl.BlockSpec((2, t, t), whole), pl.BlockSpec((2, t, t), whole),
            pl.BlockSpec((LANES, 512), lambda bi, c: (0, 0))],
        out_specs=[pl.BlockSpec((1, ts, 512), lambda bi, c: (bi, fwd(c), 0)),
                   pl.BlockSpec((1, ts, 512), lambda bi, c: (bi, bwd(c), 0))],
        out_shape=[jax.ShapeDtypeStruct((b, ltot, 512), BF16), jax.ShapeDtypeStruct((b, ltot, 512), BF16)],
        scratch_shapes=[pltpu.VMEM((LANES, 512), F32), pltpu.VMEM((LANES, 512), F32)],
        compiler_params=_cparams(("parallel", "arbitrary")), name="ssd_scan",
    )(xs, bc, dtx, xs, bc, dtx, a_rate_x, tri, neg, hmask)


def _merge_body(gl_ref, bg_ref, cg_ref, cgp_ref, cgn_ref, u_ref, up_ref, un_ref,
                pu_ref, pup_ref, pun_ref, z_ref, att_ref, yf_ref, yb_ref, xs_ref, x_ref, mod_ref,
                cw_ref, band_ref, pw_ref, ps_ref, dsk_ref, sn_ref, wb_ref, wo_ref, o_ref, v_ref, *, tm, l):
    i = pl.program_id(1)
    has_prev = i > 0
    has_next = i < l // tm - 1

    def ext(main_ref, prev_ref, next_ref):
        prev = jnp.where(has_prev, prev_ref[0], jnp.zeros_like(prev_ref[0]))
        nxt = jnp.where(has_next, next_ref[0], jnp.zeros_like(next_ref[0]))
        return jnp.concatenate([prev, main_ref[0], nxt], axis=0)

    v_ref[...] = ext(cg_ref, cgp_ref, cgn_ref).astype(F32) * ext(u_ref, up_ref, un_ref).astype(F32)
    cw = cw_ref[...]
    conv = (cw[0:1] * v_ref[pl.ds(HALO - 1, tm), :] + cw[1:2] * v_ref[pl.ds(HALO, tm), :]
            + cw[2:3] * v_ref[pl.ds(HALO + 1, tm), :])
    br_conv = bg_ref[0].astype(F32) * conv

    pe = ext(pu_ref, pup_ref, pun_ref)
    posm = i * tm + lax.broadcasted_iota(jnp.int32, (tm, LANES), 0)
    pooled = []
    for g, win in enumerate(POOL_WINDOWS):
        cols = slice(g * LANES, (g + 1) * LANES)
        sums = jnp.concatenate(
            [jnp.dot(band_ref[g], pe[k * POOL_BLOCK:k * POOL_BLOCK + POOL_BLOCK + 2 * HALO, cols],
                     preferred_element_type=F32) for k in range(tm // POOL_BLOCK)], axis=0)
        cnt = (jnp.minimum(posm, win // 2) + jnp.minimum(l - posm, win // 2)).astype(F32)
        pm = (sums / cnt - pu_ref[0, :, cols].astype(F32)).astype(BF16)
        pooled.append(jnp.dot(pm, pw_ref[g], preferred_element_type=F32))
    br_pool = jnp.concatenate(pooled, axis=1) * ps_ref[...]

    y = yf_ref[0].astype(F32) + yb_ref[0].astype(F32) + dsk_ref[...] * xs_ref[0].astype(F32)
    gz = y * _silu(z_ref[0].astype(F32))
    br_ssd = gz * lax.rsqrt(jnp.mean(gz * gz, axis=-1, keepdims=True) + NORM_EPS) * sn_ref[...]

    branches = (att_ref[0], br_conv.astype(BF16), br_pool.astype(BF16), br_ssd.astype(BF16))
    acc = None
    for k, br in enumerate(branches):
        gate2 = jnp.tanh(gl_ref[0, :, k * D_MODEL:(k + 1) * D_MODEL].astype(F32)) + 1.0
        term = gate2 * jnp.dot(br, wb_ref[k], preferred_element_type=F32)
        acc = term if acc is None else acc + term
    yo = jnp.dot((0.5 * acc).astype(BF16), wo_ref[...], preferred_element_type=F32)
    o_ref[0] = x_ref[0] + mod_ref[0, 2:3, :] * yo


def _merge(proj, att, yf, yb, xs_c, x, mod, conv_short, band, pool_w, pool_scale, dskip_x, ssd_norm,
           w_branch, w_out, layer, tm, ssd_row_off):
    b, l, _ = x.shape
    per_batch = mod.shape[0] > 1
    off = ssd_row_off // tm
    const2 = lambda bi, i: (0, 0)
    const3 = lambda bi, i: (0, 0, 0)
    in_specs = (
        [pl.BlockSpec((1, tm, 4096), lambda bi, i: (bi, i, 0)),
         pl.BlockSpec((1, tm, 512), lambda bi, i: (bi, i, C_BG // 512))]
        + _halo_specs(tm, l, 512, C_CG // 512) + _halo_specs(tm, l, 512, C_U // 512)
        + _halo_specs(tm, l, 512, C_PU // 512)
        + [pl.BlockSpec((1, tm, 512), lambda bi, i: (bi, i, C_Z // 512)),
           pl.BlockSpec((1, tm, 512), lambda bi, i: (bi, i, 0)),
           pl.BlockSpec((1, tm, 512), lambda bi, i: (bi, i + off, 0)),
           pl.BlockSpec((1, tm, 512), lambda bi, i: (bi, i + off, 0)),
           pl.BlockSpec((1, tm, 512), lambda bi, i: (bi, i + off, 0)),
           pl.BlockSpec((1, tm, D_MODEL), lambda bi, i: (bi, i, 0)),
           pl.BlockSpec((1, 6, D_MODEL), lambda bi, i: (bi if per_batch else 0, 0, 0)),
           pl.BlockSpec((3, 512), const2),
           pl.BlockSpec((4, POOL_BLOCK, POOL_BLOCK + 2 * HALO), const3),
           pl.BlockSpec((None, 4, LANES, LANES), lambda bi, i: (layer, 0, 0, 0)),
           pl.BlockSpec((1, 512), const2),
           pl.BlockSpec((1, 512), const2),
           pl.BlockSpec((1, 512), const2),
           pl.BlockSpec((None, N_BRANCH, BRANCH_WIDTH, D_MODEL), lambda bi, i: (layer, 0, 0, 0)),
           pl.BlockSpec((None, D_MODEL, D_MODEL), lambda bi, i: (layer, 0, 0))])
    return pl.pallas_call(
        functools.partial(_merge_body, tm=tm, l=l), grid=(b, l // tm),
        in_specs=in_specs,
        out_specs=pl.BlockSpec((1, tm, D_MODEL), lambda bi, i: (bi, i, 0)),
        out_shape=jax.ShapeDtypeStruct((b, l, D_MODEL), F32),
        scratch_shapes=[pltpu.VMEM((tm + 2 * HALO, 512), F32)],
        compiler_params=_cparams(("parallel", "parallel")), name="merge",
    )(proj, proj, proj, proj, proj, proj, proj, proj, proj, proj, proj, proj,
      att, yf, yb, xs_c, x, mod, conv_short, band, pool_w, pool_scale, dskip_x, ssd_norm, w_branch, w_out)


FFN_HALO = 8
FFN_CHUNK = 256


def _ffn_body(x_ref, xp_ref, xn_ref, mod_ref, g_ref, wu_ref, cw_ref, wd_ref, fn_ref,
              o_ref, h_ref, ug_ref, uv_ref, a_ref, *, tm, l, final):
    i = pl.program_id(1)
    n_ext = tm + 2 * FFN_HALO
    pos = i * tm - FFN_HALO + lax.broadcasted_iota(jnp.int32, (n_ext, D_MODEL), 0)
    xe = jnp.concatenate([xp_ref[0], x_ref[0], xn_ref[0]], axis=0)
    h = _rms_mod(xe, g_ref[...], mod_ref[0, 4:5, :], mod_ref[0, 3:4, :])
    h_ref[...] = jnp.where((pos >= 0) & (pos < l), h, 0.0).astype(BF16)

    def conv3(u_ref, s, w):
        return (w[0:1] * u_ref[s, pl.ds(FFN_HALO - 1, tm), :] + w[1:2] * u_ref[s, pl.ds(FFN_HALO, tm), :]
                + w[2:3] * u_ref[s, pl.ds(FFN_HALO + 1, tm), :])

    for j in range(D_FF // FFN_CHUNK):
        s = j % 2
        cg = slice(j * FFN_CHUNK, (j + 1) * FFN_CHUNK)
        cv = slice(D_FF + j * FFN_CHUNK, D_FF + (j + 1) * FFN_CHUNK)
        ug_ref[s] = jnp.dot(h_ref[...], wu_ref[:, cg], preferred_element_type=F32)
        uv_ref[s] = jnp.dot(h_ref[...], wu_ref[:, cv], preferred_element_type=F32)
        a_ref[:, cg] = (_silu(conv3(ug_ref, s, cw_ref[:, cg]))
                        * conv3(uv_ref, s, cw_ref[:, cv])).astype(BF16)

    out = x_ref[0] + mod_ref[0, 5:6, :] * jnp.dot(a_ref[...], wd_ref[...], preferred_element_type=F32)
    if final:
        out = out * lax.rsqrt(jnp.mean(out * out, axis=-1, keepdims=True) + NORM_EPS) * fn_ref[...]
    o_ref[0] = out


def _ffn(x, mod, g, w_up, ffn_conv, w_down, final_norm, layer, tm, final):
    b, l, _ = x.shape
    per_batch = mod.shape[0] > 1
    r = tm // FFN_HALO
    last = l // FFN_HALO - 1
    n_ext = tm + 2 * FFN_HALO
    const = lambda bi, i: (0, 0)
    resident = pl.Buffered(1)
    return pl.pallas_call(
        functools.partial(_ffn_body, tm=tm, l=l, final=final), grid=(b, l // tm),
        in_specs=[pl.BlockSpec((1, tm, D_MODEL), lambda bi, i: (bi, i, 0)),
                  pl.BlockSpec((1, FFN_HALO, D_MODEL), lambda bi, i: (bi, jnp.maximum(i * r - 1, 0), 0)),
                  pl.BlockSpec((1, FFN_HALO, D_MODEL), lambda bi, i: (bi, jnp.minimum((i + 1) * r, last), 0)),
                  pl.BlockSpec((1, 6, D_MODEL), lambda bi, i: (bi if per_batch else 0, 0, 0)),
                  pl.BlockSpec((1, D_MODEL), const),
                  pl.BlockSpec((None, D_MODEL, 2 * D_FF), lambda bi, i: (layer, 0, 0), pipeline_mode=resident),
                  pl.BlockSpec((3, 2 * D_FF), const, pipeline_mode=resident),
                  pl.BlockSpec((None, D_FF, D_MODEL), lambda bi, i: (layer, 0, 0), pipeline_mode=resident),
                  pl.BlockSpec((1, D_MODEL), const)],
        out_specs=pl.BlockSpec((1, tm, D_MODEL), lambda bi, i: (bi, i, 0)),
        out_shape=jax.ShapeDtypeStruct((b, l, D_MODEL), F32),
        scratch_shapes=[pltpu.VMEM((n_ext, D_MODEL), BF16), pltpu.VMEM((2, n_ext, FFN_CHUNK), F32),
                        pltpu.VMEM((2, n_ext, FFN_CHUNK), F32), pltpu.VMEM((tm, D_FF), BF16)],
        compiler_params=_cparams(("parallel", "parallel")), name="conv_ffn",
    )(x, x, x, mod, g.reshape(1, D_MODEL), w_up, ffn_conv, w_down, final_norm.reshape(1, D_MODEL))


def _rope_tables(n_tokens):
    rows = n_tokens // GRID_W
    row = jnp.repeat(jnp.arange(rows), GRID_W).astype(F32)
    col = jnp.tile(jnp.arange(GRID_W), rows).astype(F32)
    inv = ROPE_THETA ** (-jnp.arange(ROPE_FREQS, dtype=F32) / ROPE_FREQS)
    ang_r, ang_c = row[:, None] * inv, col[:, None] * inv
    ang = jnp.concatenate([ang_r, ang_r, ang_c, ang_c], axis=1)
    sign = jnp.tile(jnp.concatenate([-jnp.ones(ROPE_FREQS), jnp.ones(ROPE_FREQS)]), 2).astype(F32)
    cos = jnp.tile(jnp.cos(ang), (1, 2))
    sin = jnp.tile(jnp.sin(ang) * sign, (1, 2))
    return cos, sin


def _identity_tables(n):
    return jnp.ones((n, LANES), F32), jnp.zeros((n, LANES), F32)


def _cat_weights(w_in):
    o = np.cumsum([0, 512, 128, 128, 512, 512, 512, 512, 512, 768, 16, 4096])
    wt = jnp.swapaxes(w_in, 1, 2)
    q, k, v, bg, cg, u, pu, z, xbc, dt, gl = [wt[:, o[i]:o[i + 1], :] for i in range(11)]
    dt = jnp.pad(dt, ((0, 0), (0, DT_PAD - dt.shape[1]), (0, 0)))
    gl = 0.5 * gl
    return jnp.concatenate([gl, bg, cg, u, pu, z, q, xbc[:, :512], xbc[:, 512:], k, v, dt],
                           axis=1).astype(BF16)


def kernel(x, c, ctx, c_ctx, w_mod, b_mod, norm_mix, w_in, q_norm, k_norm, conv_short, pool_w, pool_scale,
           ssd_conv_w, ssd_conv_b, ssd_dt_bias, ssd_a_log, ssd_d, ssd_norm, w_branch, w_out, norm_ffn,
           w_up, ffn_conv, w_down, final_norm):
    depth = w_mod.shape[0]
    b, l, _ = x.shape
    lc = ctx.shape[1]
    ltot = l + lc
    t = SSD_CHUNK

    tm_in = min(1024, l)
    tm_mid = min(512, l)
    tq = 256
    tk_lat = 768 if ltot % 768 == 0 else lc
    tk_ctx = lc

    cos_l, sin_l = _rope_tables(l)
    cos_c, sin_c = _identity_tables(lc)
    cos_all = jnp.concatenate([cos_l, cos_c], axis=0)
    sin_all = jnp.concatenate([sin_l, sin_c], axis=0)
    lane = np.arange(LANES)
    bd = jnp.asarray((lane[:, None] // HEAD_DIM) == (lane[None, :] // HEAD_DIM), BF16)
    col512 = np.arange(512)
    expand = np.zeros((2, DT_PAD, 512), np.float32)
    for d in range(2):
        expand[d, d * SSD_HEADS + col512 // HEAD_DIM, col512] = 1.0
    expand = jnp.asarray(expand, BF16)
    pos = np.arange(t)
    tri = jnp.asarray(np.stack([pos[:, None] >= pos[None, :], pos[:, None] <= pos[None, :]]), BF16)
    neg = jnp.asarray(np.where(np.stack([pos[:, None] >= pos[None, :], pos[:, None] <= pos[None, :]]),
                               0.0, -np.inf), F32)
    brow, bcol = np.arange(POOL_BLOCK)[:, None], np.arange(POOL_BLOCK + 2 * HALO)[None, :]
    band = jnp.asarray(np.stack([(bcol >= HALO + brow - w // 2) & (bcol < HALO + brow + w // 2)
                                 for w in POOL_WINDOWS]), BF16)
    hmask = jnp.asarray((lane[:, None] // SSD_STATE) == (col512[None, :] // 256), F32)

    cond8 = jnp.zeros((8, D_MODEL), F32).at[:b].set(c).at[b].set(c_ctx)
    mod_all = _modulation(cond8, w_mod, b_mod).reshape(depth, 8, 6, D_MODEL)
    w_cat = _cat_weights(w_in)
    wb, wo, pw = w_branch.astype(BF16), w_out.astype(BF16), pool_w.astype(BF16)
    wu, wd = w_up.astype(BF16), w_down.astype(BF16)
    xc = ctx
    for i in range(depth):
        last = i == depth - 1
        mod_l, mod_c = mod_all[i, :b], mod_all[i, b:b + 1]
        gq = (jnp.tile(q_norm[i], 2) * QK_SCALE).reshape(1, LANES)
        gk = jnp.tile(k_norm[i], 2).reshape(1, LANES)
        neg_bound = jnp.full((1, LANES), -HEAD_DIM * QK_SCALE, F32) * (jnp.max(jnp.abs(q_norm[i]))
                                                                  * jnp.max(jnp.abs(k_norm[i])))
        a_rate_x = jnp.repeat(-jnp.exp(ssd_a_log[i].astype(F32)) * math.log2(math.e),
                              HEAD_DIM, axis=1).reshape(2, 1, 512)
        dskip_x = jnp.repeat(ssd_d[i], HEAD_DIM).reshape(1, 512)
        dt_bias_pad = jnp.pad(ssd_dt_bias[i].reshape(1, 2 * SSD_HEADS), ((0, 0), (0, DT_PAD - 2 * SSD_HEADS)))
        ps = pool_scale[i].reshape(1, 512)
        sn = ssd_norm[i].reshape(1, 512)
        scb = ssd_conv_b[i].reshape(1, 768)

        proj = _inproj(x, mod_l, norm_mix[i], w_cat, i, tm_in)
        proj_c = _inproj(xc, mod_c, norm_mix[i], w_cat, i, lc)

        qh = _qprep(proj, cos_l, sin_l, gq, neg_bound, bd, tm_mid)
        kt, v1 = _kvprep(proj, proj_c, cos_all, sin_all, gk, bd)
        att = _flash(qh, kt, v1, ltot, 0, tq, tk_lat)

        xs_s, bc_s, dtx = _ssdprep(proj, proj_c, ssd_conv_w[i], scb, dt_bias_pad, expand)
        yf, yb = _ssd_scan(xs_s, bc_s, dtx, a_rate_x, tri, neg, hmask, l)

        x_mid = _merge(proj, att, yf, yb, xs_s, x, mod_l, conv_short[i], band, pw, ps, dskip_x, sn, wb, wo, i, tm_mid, 0)
        if not last:
            qh_c = _qprep(proj_c, cos_c, sin_c, gq, neg_bound, bd, lc)
            att_c = _flash(qh_c, kt, v1, lc, l, tq, tk_ctx)
            xc_mid = _merge(proj_c, att_c, yf, yb, xs_s, xc, mod_c, conv_short[i], band, pw, ps, dskip_x, sn, wb, wo, i, lc, l)
            xc = _ffn(xc_mid, mod_c, norm_ffn[i], wu, ffn_conv[i], wd, final_norm, i, lc, False)
        x = _ffn(x_mid, mod_l, norm_ffn[i], wu, ffn_conv[i], wd, final_norm, i, tm_mid, last)
    return x
```

```python
import functools
import math

import jax
import jax.numpy as jnp
import numpy as np
from jax import lax
from jax.experimental import pallas as pl
from jax.experimental.pallas import tpu as pltpu

F32 = jnp.float32
BF16 = jnp.bfloat16

D_MODEL = 1024
GRID_W = 64
NORM_EPS = 1e-6
ATT_HEADS = 8
ATT_KV_HEADS = 2
HEAD_DIM = 64
ROPE_THETA = 10000.0
ROPE_FREQS = HEAD_DIM // 4
POOL_WINDOWS = (2, 4, 8, 16)
SSD_HEADS = 8
SSD_STATE = 64
SSD_CHUNK = 128
N_BRANCH = 4
BRANCH_WIDTH = 512
D_FF = 2816

C_GL, C_BG, C_CG, C_U, C_PU, C_Z, C_Q, C_XS, C_BC, C_K, C_V, C_DT = (
    0, 4096, 4608, 5120, 5632, 6144, 6656, 7168, 7680, 7936, 8064, 8192)
N_CAT = 8448
DT_PAD = 256

HALO = 16
POOL_BLOCK = 128
LANES = 128
QK_SCALE = HEAD_DIM ** -0.5 * math.log2(math.e)
ROW_SUM_MIN = 2.0 ** -47
VMEM_LIMIT = 56 * 1024 * 1024


def _cparams(sem):
    return pltpu.CompilerParams(dimension_semantics=sem, vmem_limit_bytes=VMEM_LIMIT)


def _silu(x):
    hx = 0.5 * x
    return hx * (jnp.tanh(hx) + 1.0)


def _split_dot(a, b_bf16):
    hi = a.astype(BF16)
    lo = (a - hi.astype(F32)).astype(BF16)
    return (jnp.dot(hi, b_bf16, preferred_element_type=F32)
            + jnp.dot(lo, b_bf16, preferred_element_type=F32))


def _dot_split_lhs(a_bf16, b):
    hi = b.astype(BF16)
    lo = (b - hi.astype(F32)).astype(BF16)
    return (jnp.dot(a_bf16, hi, preferred_element_type=F32)
            + jnp.dot(a_bf16, lo, preferred_element_type=F32))


def _rms_mod(x, g, scale, shift):
    ms = jnp.mean(x * x, axis=-1, keepdims=True)
    y = x * lax.rsqrt(ms + NORM_EPS) * g
    return y * (1.0 + scale) + shift


def _mod_body(c_ref, w_ref, b_ref, o_ref):
    s = _silu(c_ref[...])
    o_ref[...] = jnp.dot(s, w_ref[...], precision=lax.Precision.HIGHEST,
                         preferred_element_type=F32) + b_ref[...]


def _modulation(cond8, w_mod, b_mod):
    depth, _, n = w_mod.shape
    tn = 1536
    return pl.pallas_call(
        _mod_body, grid=(depth, n // tn),
        in_specs=[pl.BlockSpec((8, D_MODEL), lambda d, j: (0, 0)),
                  pl.BlockSpec((None, D_MODEL, tn), lambda d, j: (d, 0, j)),
                  pl.BlockSpec((None, 1, tn), lambda d, j: (d, 0, j))],
        out_specs=pl.BlockSpec((None, 8, tn), lambda d, j: (d, 0, j)),
        out_shape=jax.ShapeDtypeStruct((depth, 8, n), F32),
        compiler_params=_cparams(("arbitrary", "arbitrary")), name="modulation",
    )(cond8, w_mod, b_mod.reshape(depth, 1, n))


def _inproj_body(x_ref, mod_ref, g_ref, w_ref, o_ref, h_ref):
    @pl.when(pl.program_id(2) == 0)
    def _():
        h = _rms_mod(x_ref[0], g_ref[...], mod_ref[0, 1:2, :], mod_ref[0, 0:1, :])
        h_ref[...] = h.astype(BF16)

    o_ref[0] = lax.dot_general(h_ref[...], w_ref[...], (((1,), (1,)), ((), ())),
                               preferred_element_type=F32).astype(BF16)


def _inproj(x, mod, g, w_cat, layer, tm):
    b, l, _ = x.shape
    n = w_cat.shape[1]
    tn = 2816
    per_batch = mod.shape[0] > 1
    return pl.pallas_call(
        _inproj_body, grid=(b, l // tm, n // tn),
        in_specs=[pl.BlockSpec((1, tm, D_MODEL), lambda bi, i, j: (bi, i, 0)),
                  pl.BlockSpec((1, 6, D_MODEL), lambda bi, i, j: (bi if per_batch else 0, 0, 0)),
                  pl.BlockSpec((1, D_MODEL), lambda bi, i, j: (0, 0)),
                  pl.BlockSpec((None, tn, D_MODEL), lambda bi, i, j: (layer, j, 0))],
        out_specs=pl.BlockSpec((1, tm, tn), lambda bi, i, j: (bi, i, j)),
        out_shape=jax.ShapeDtypeStruct((b, l, n), BF16),
        scratch_shapes=[pltpu.VMEM((tm, D_MODEL), BF16)],
        compiler_params=_cparams(("parallel", "parallel", "arbitrary")), name="inproj",
    )(x, mod, g.reshape(1, D_MODEL), w_cat)


def _head_norm_rope(t, cos, sin, gain, bd):
    ss = _split_dot(t * t, bd)
    y = t * lax.rsqrt(ss * (1.0 / HEAD_DIM) + NORM_EPS) * gain
    lane = lax.broadcasted_iota(jnp.int32, y.shape, 1)
    partner = jnp.where((lane % 32) < 16, pltpu.roll(y, LANES - 16, 1), pltpu.roll(y, 16, 1))
    return y * cos + partner * sin


def _qprep_body(q_ref, cos_ref, sin_ref, g_ref, negb_ref, bd_ref, o_ref):
    cos, sin, gain, bd = cos_ref[...], sin_ref[...], g_ref[...], bd_ref[...]
    lane = lax.broadcasted_iota(jnp.int32, cos.shape, 1)
    for c in range(ATT_HEADS // 2):
        t = q_ref[0, :, c * LANES:(c + 1) * LANES].astype(F32)
        r = _head_norm_rope(t, cos, sin, gain, bd)
        tail = jnp.where(lane == HEAD_DIM, negb_ref[...], 0.0)
        o_ref[0, 2 * c] = jnp.where(lane < HEAD_DIM, r, tail).astype(BF16)
        o_ref[0, 2 * c + 1] = jnp.where(lane < HEAD_DIM, pltpu.roll(r, HEAD_DIM, 1), tail).astype(BF16)


def _qprep(proj, cos, sin, gain, neg_bound, bd, tm):
    b, l, _ = proj.shape
    return pl.pallas_call(
        _qprep_body, grid=(b, l // tm),
        in_specs=[pl.BlockSpec((1, tm, 512), lambda bi, i: (bi, i, C_Q // 512)),
                  pl.BlockSpec((tm, LANES), lambda bi, i: (i, 0)),
                  pl.BlockSpec((tm, LANES), lambda bi, i: (i, 0)),
                  pl.BlockSpec((1, LANES), lambda bi, i: (0, 0)),
                  pl.BlockSpec((1, LANES), lambda bi, i: (0, 0)),
                  pl.BlockSpec((LANES, LANES), lambda bi, i: (0, 0))],
        out_specs=pl.BlockSpec((1, ATT_HEADS, tm, LANES), lambda bi, i: (bi, 0, i, 0)),
        out_shape=jax.ShapeDtypeStruct((b, ATT_HEADS, l, LANES), BF16),
        compiler_params=_cparams(("parallel", "parallel")), name="q_prep",
    )(proj, cos, sin, gain, neg_bound, bd)


def _kvprep_body(kl_ref, vl_ref, kc_ref, vc_ref, cos_ref, sin_ref, g_ref, bd_ref, kt_ref, v1_ref, *, nt_lat):
    is_ctx = pl.program_id(1) == nt_lat
    k = jnp.where(is_ctx, kc_ref[0], kl_ref[0]).astype(F32)
    v = jnp.where(is_ctx, vc_ref[0], vl_ref[0]).astype(F32)
    r = _head_norm_rope(k, cos_ref[...], sin_ref[...], g_ref[...], bd_ref[...])
    lane = lax.broadcasted_iota(jnp.int32, r.shape, 1)
    ones_col = jnp.where(lane == HEAD_DIM, 1.0, 0.0)
    for g in range(ATT_KV_HEADS):
        rg = r if g == 0 else pltpu.roll(r, HEAD_DIM, 1)
        vg = v if g == 0 else pltpu.roll(v, HEAD_DIM, 1)
        kt_ref[0, g] = jnp.where(lane < HEAD_DIM, rg, ones_col).T.astype(BF16)
        v1_ref[0, g] = jnp.where(lane < HEAD_DIM, vg, ones_col).astype(BF16)


def _kvprep(proj, proj_c, cos, sin, gain, bd):
    b, l, _ = proj.shape
    lc = proj_c.shape[1]
    tm = lc
    nt_lat = l // tm
    ltot = l + lc
    lat = lambda bi, t: (bi, jnp.minimum(t, nt_lat - 1))
    return pl.pallas_call(
        functools.partial(_kvprep_body, nt_lat=nt_lat), grid=(b, nt_lat + 1),
        in_specs=[pl.BlockSpec((1, tm, LANES), lambda bi, t: lat(bi, t) + (C_K // LANES,)),
                  pl.BlockSpec((1, tm, LANES), lambda bi, t: lat(bi, t) + (C_V // LANES,)),
                  pl.BlockSpec((1, tm, LANES), lambda bi, t: (bi, 0, C_K // LANES)),
                  pl.BlockSpec((1, tm, LANES), lambda bi, t: (bi, 0, C_V // LANES)),
                  pl.BlockSpec((tm, LANES), lambda bi, t: (t, 0)),
                  pl.BlockSpec((tm, LANES), lambda bi, t: (t, 0)),
                  pl.BlockSpec((1, LANES), lambda bi, t: (0, 0)),
                  pl.BlockSpec((LANES, LANES), lambda bi, t: (0, 0))],
        out_specs=[pl.BlockSpec((1, ATT_KV_HEADS, LANES, tm), lambda bi, t: (bi, 0, 0, t)),
                   pl.BlockSpec((1, ATT_KV_HEADS, tm, LANES), lambda bi, t: (bi, 0, t, 0))],
        out_shape=[jax.ShapeDtypeStruct((b, ATT_KV_HEADS, LANES, ltot), BF16),
                   jax.ShapeDtypeStruct((b, ATT_KV_HEADS, ltot, LANES), BF16)],
        compiler_params=_cparams(("parallel", "parallel")), name="kv_prep",
    )(proj, proj, proj_c, proj_c, cos, sin, gain, bd)


def _flash_body(q_ref, kt_ref, v_ref, o_ref, s_ref, acc_ref, m_ref, *, tq, tk, nk):
    rep = ATT_HEADS // ATT_KV_HEADS
    rows = rep * tq
    q = q_ref[0].reshape(rows, LANES)

    def keys(c):
        return kt_ref[0, 0, :, pl.ds(pl.multiple_of(c * tk, tk), tk)]

    def vals(c):
        return v_ref[0, 0, pl.ds(pl.multiple_of(c * tk, tk), tk), :]

    s_ref[0] = jnp.dot(q, keys(0), preferred_element_type=F32)
    acc = jnp.zeros((rows, LANES), F32)
    for c in range(nk):
        if c + 1 < nk:
            s_ref[(c + 1) % 2] = jnp.dot(q, keys(c + 1), preferred_element_type=F32)
        p = jnp.exp2(s_ref[c % 2]).astype(BF16)
        acc = acc + jnp.dot(p, vals(c), preferred_element_type=F32)
    acc_ref[...] = acc

    @pl.when(jnp.min(acc[:, HEAD_DIM:HEAD_DIM + 1]) < ROW_SUM_MIN)
    def _():
        m_ref[...] = jnp.full(m_ref.shape, -jnp.inf, F32)
        acc_ref[...] = jnp.zeros(acc_ref.shape, F32)

        def step(c, carry):
            s = jnp.dot(q, keys(c), preferred_element_type=F32)
            m_prev = m_ref[...]
            m_new = jnp.maximum(m_prev, jnp.max(s, axis=1, keepdims=True))
            alpha = jnp.exp2(m_prev - m_new)
            p = jnp.exp2(s - jnp.tile(m_new, (1, tk // LANES))).astype(BF16)
            acc_ref[...] = acc_ref[...] * alpha + jnp.dot(p, vals(c), preferred_element_type=F32)
            m_ref[...] = m_new
            return carry

        lax.fori_loop(0, nk, step, 0)

    acc = acc_ref[...]
    o = acc / acc[:, HEAD_DIM:HEAD_DIM + 1]
    lane = lax.broadcasted_iota(jnp.int32, (tq, LANES), 1)
    for j in range(rep // 2):
        a = o[(2 * j) * tq:(2 * j + 1) * tq]
        bb = o[(2 * j + 1) * tq:(2 * j + 2) * tq]
        o_ref[0, :, j * LANES:(j + 1) * LANES] = jnp.where(
            lane < HEAD_DIM, a, pltpu.roll(bb, HEAD_DIM, 1)).astype(BF16)


def _flash(qh, kt, v1, lk, k_off, tq, tk):
    b, _, l, _ = qh.shape
    rep = ATT_HEADS // ATT_KV_HEADS
    kb = k_off // lk
    return pl.pallas_call(
        functools.partial(_flash_body, tq=tq, tk=tk, nk=lk // tk),
        grid=(b, ATT_KV_HEADS, l // tq),
        in_specs=[pl.BlockSpec((1, rep, tq, LANES), lambda bi, g, i: (bi, g, i, 0)),
                  pl.BlockSpec((1, 1, LANES, lk), lambda bi, g, i: (bi, g, 0, kb)),
                  pl.BlockSpec((1, 1, lk, LANES), lambda bi, g, i: (bi, g, kb, 0))],
        out_specs=pl.BlockSpec((1, tq, rep * HEAD_DIM), lambda bi, g, i: (bi, i, g)),
        out_shape=jax.ShapeDtypeStruct((b, l, ATT_HEADS * HEAD_DIM), BF16),
        scratch_shapes=[pltpu.VMEM((2, rep * tq, tk), F32),
                        pltpu.VMEM((rep * tq, LANES), F32), pltpu.VMEM((rep * tq, LANES), F32)],
        compiler_params=_cparams(("parallel", "parallel", "arbitrary")), name="flash_attention",
    )(qh, kt, v1)


def _halo_specs(tm, l, width, col_block):
    r = tm // HALO
    last = l // HALO - 1
    main = pl.BlockSpec((1, tm, width), lambda bi, i: (bi, i, col_block))
    prev = pl.BlockSpec((1, HALO, width), lambda bi, i: (bi, jnp.maximum(i * r - 1, 0), col_block))
    nxt = pl.BlockSpec((1, HALO, width), lambda bi, i: (bi, jnp.minimum((i + 1) * r, last), col_block))
    return [main, prev, nxt]


def _ssdprep_body(xs_ref, xsp_ref, xsn_ref, bc_ref, bcp_ref, bcn_ref, dt_ref,
                  xsc_ref, bcc_ref, dtc_ref, w_ref, b_ref, dtb_ref, ex_ref,
                  xo_ref, bco_ref, dtx_ref, ex_scr, eb_scr, *, nt_lat, tm):
    t = pl.program_id(1)
    is_ctx = t == nt_lat
    has_prev = (t > 0) & (t < nt_lat)
    has_next = t < nt_lat - 1

    def conv_silu(main_ref, prev_ref, next_ref, ctx_ref, scr, lo, hi):
        main = jnp.where(is_ctx, ctx_ref[0], main_ref[0])
        prev = jnp.where(has_prev, prev_ref[0], jnp.zeros_like(prev_ref[0]))
        nxt = jnp.where(has_next, next_ref[0], jnp.zeros_like(next_ref[0]))
        scr[...] = jnp.concatenate([prev, main, nxt], axis=0).astype(F32)
        w = w_ref[:, lo:hi]
        y = (w[0:1] * scr[pl.ds(HALO - 1, tm), :] + w[1:2] * scr[pl.ds(HALO, tm), :]
             + w[2:3] * scr[pl.ds(HALO + 1, tm), :]) + b_ref[:, lo:hi]
        return _silu(y)

    xo_ref[0] = conv_silu(xs_ref, xsp_ref, xsn_ref, xsc_ref, ex_scr, 0, 512).astype(BF16)
    bco_ref[0] = conv_silu(bc_ref, bcp_ref, bcn_ref, bcc_ref, eb_scr, 512, 768).astype(BF16)
    raw = jnp.where(is_ctx, dtc_ref[0], dt_ref[0]).astype(F32) + dtb_ref[...]
    dt = jnp.maximum(raw, 0.0) + jnp.log1p(jnp.exp(-jnp.abs(raw)))
    for d in range(2):
        dtx_ref[0, d] = _split_dot(dt, ex_ref[d])


def _ssdprep(proj, proj_c, conv_w, conv_b, dt_bias_pad, expand):
    b, l, _ = proj.shape
    lc = proj_c.shape[1]
    tm = lc
    nt_lat = l // tm
    ltot = l + lc
    r = tm // HALO
    last = l // HALO - 1
    cl = lambda t: jnp.minimum(t, nt_lat - 1)

    def lat_specs(width, col):
        return [pl.BlockSpec((1, tm, width), lambda bi, t: (bi, cl(t), col)),
                pl.BlockSpec((1, HALO, width), lambda bi, t: (bi, jnp.maximum(cl(t) * r - 1, 0), col)),
                pl.BlockSpec((1, HALO, width), lambda bi, t: (bi, jnp.minimum((cl(t) + 1) * r, last), col))]

    return pl.pallas_call(
        functools.partial(_ssdprep_body, nt_lat=nt_lat, tm=tm), grid=(b, nt_lat + 1),
        in_specs=(lat_specs(512, C_XS // 512) + lat_specs(256, C_BC // 256)
                  + [pl.BlockSpec((1, tm, DT_PAD), lambda bi, t: (bi, cl(t), C_DT // DT_PAD)),
                     pl.BlockSpec((1, tm, 512), lambda bi, t: (bi, 0, C_XS // 512)),
                     pl.BlockSpec((1, tm, 256), lambda bi, t: (bi, 0, C_BC // 256)),
                     pl.BlockSpec((1, tm, DT_PAD), lambda bi, t: (bi, 0, C_DT // DT_PAD)),
                     pl.BlockSpec((3, 768), lambda bi, t: (0, 0)),
                     pl.BlockSpec((1, 768), lambda bi, t: (0, 0)),
                     pl.BlockSpec((1, DT_PAD), lambda bi, t: (0, 0)),
                     pl.BlockSpec((2, DT_PAD, 512), lambda bi, t: (0, 0, 0))]),
        out_specs=[pl.BlockSpec((1, tm, 512), lambda bi, t: (bi, t, 0)),
                   pl.BlockSpec((1, tm, 256), lambda bi, t: (bi, t, 0)),
                   pl.BlockSpec((1, 2, tm, 512), lambda bi, t: (bi, 0, t, 0))],
        out_shape=[jax.ShapeDtypeStruct((b, ltot, 512), BF16),
                   jax.ShapeDtypeStruct((b, ltot, 256), BF16),
                   jax.ShapeDtypeStruct((b, 2, ltot, 512), F32)],
        scratch_shapes=[pltpu.VMEM((tm + 2 * HALO, 512), F32), pltpu.VMEM((tm + 2 * HALO, 256), F32)],
        compiler_params=_cparams(("parallel", "parallel")), name="ssd_prep",
    )(proj, proj, proj, proj, proj, proj, proj, proj_c, proj_c, proj_c,
      conv_w, conv_b, dt_bias_pad, expand)


SSD_STEP_CHUNKS = 2


def _ssd_body(xsf_ref, bcf_ref, dtf_ref, xsb_ref, bcb_ref, dtb_ref, ar_ref, tri_ref, neg_ref, hm_ref,
              yf_ref, yb_ref, htf_ref, htb_ref):
    t = SSD_CHUNK

    @pl.when(pl.program_id(1) == 0)
    def _():
        htf_ref[...] = jnp.zeros(htf_ref.shape, F32)
        htb_ref[...] = jnp.zeros(htb_ref.shape, F32)

    lane = lax.broadcasted_iota(jnp.int32, (t, LANES), 1)
    low = lane < HEAD_DIM

    def within_chunk(d, xs_ref, bc_ref, dtx_ref, rows):
        xs = xs_ref[0, rows, :].astype(F32)
        dtx = dtx_ref[0, 0, rows, :]
        cum = _dot_split_lhs(tri_ref[d], dtx * ar_ref[d])
        tot = cum[t - 1:t, :] if d == 0 else cum[0:1, :]
        neg = neg_ref[d]
        xdt = xs * dtx
        xdt_b = xdt.astype(BF16)
        bc = bc_ref[0, rows, :]
        b_all, c_all = bc[:, :LANES], bc[:, LANES:]
        cb = []
        for g in range(2):
            cg = jnp.where(low if g == 0 else ~low, c_all, jnp.zeros_like(c_all))
            cb.append(lax.dot_general(cg, b_all, (((1,), (1,)), ((), ())), preferred_element_type=F32))
        ys = []
        for j in range(SSD_HEADS // 2):
            cp = cum[:, j * LANES:(j + 1) * LANES]
            cps = pltpu.roll(cp, HEAD_DIM, 1)
            cpt = cp.T
            xp = xdt_b[:, j * LANES:(j + 1) * LANES]
            acc = None
            for e in range(2):
                col = jnp.where(low, cp, cps) if e == 0 else jnp.where(low, cps, cp)
                row = jnp.broadcast_to(cpt[e * HEAD_DIM:e * HEAD_DIM + 1, :], (t, t))
                w = (cb[j // 2] * jnp.exp2(col - row + neg)).astype(BF16)
                xm = jnp.where(low if e == 0 else ~low, xp, jnp.zeros_like(xp))
                part = jnp.dot(w, xm, preferred_element_type=F32)
                acc = part if acc is None else acc + part
            ys.append(acc)
        bt = b_all.astype(F32).T.astype(BF16)
        st = jnp.dot(bt, (xdt * jnp.exp2(tot - cum)).astype(BF16), preferred_element_type=F32)
        return jnp.concatenate(ys, axis=1), c_all, jnp.exp2(cum), jnp.exp2(tot), st * hm_ref[...]

    dirs = ((xsf_ref, bcf_ref, dtf_ref, yf_ref, htf_ref), (xsb_ref, bcb_ref, dtb_ref, yb_ref, htb_ref))
    local = []
    for d, (xs_ref, bc_ref, dtx_ref, _, _) in enumerate(dirs):
        order = range(SSD_STEP_CHUNKS) if d == 0 else range(SSD_STEP_CHUNKS - 1, -1, -1)
        rows = [pl.ds(o * t, t) for o in order]
        local.append((rows, [within_chunk(d, xs_ref, bc_ref, dtx_ref, r) for r in rows]))
    for (_, _, _, y_ref, ht_ref), (rows, chunks) in zip(dirs, local):
        ht = ht_ref[...]
        for r, (y_diag, c_all, ecum, etot, st) in zip(rows, chunks):
            y_off = jnp.dot(c_all, ht.astype(BF16), preferred_element_type=F32) * ecum
            y_ref[0, r, :] = (y_diag + y_off).astype(BF16)
            ht = ht * etot + st
        ht_ref[...] = ht


def _ssd_scan(xs, bc, dtx, a_rate_x, tri, neg, hmask, l_lat):
    b, ltot, _ = xs.shape
    t = SSD_CHUNK
    ts = t * SSD_STEP_CHUNKS
    ns = ltot // ts
    nl = l_lat // ts
    nct = ns - nl

    fwd = lambda c: jnp.where(c < nct, nl + c, c - nct)
    bwd = lambda c: jnp.where(c < nct, nl + nct - 1 - c, nl - 1 - (c - nct))
    whole = lambda bi, c: (0, 0, 0)

    def specs(sid, d):
        return [pl.BlockSpec((1, ts, 512), lambda bi, c: (bi, sid(c), 0)),
                pl.BlockSpec((1, ts, 256), lambda bi, c: (bi, sid(c), 0)),
                pl.BlockSpec((1, 1, ts, 512), lambda bi, c: (bi, d, sid(c), 0))]

    return pl.pallas_call(
        _ssd_body, grid=(b, ns),
        in_specs=specs(fwd, 0) + specs(bwd, 1) + [
            pl.BlockSpec((2, 1, 512), whole), pl.BlockSpec((2, t, t), whole), pl.BlockSpec((2, t, t), whole),
            pl.BlockSpec((LANES, 512), lambda bi, c: (0, 0))],
        out_specs=[pl.BlockSpec((1, ts, 512), lambda bi, c: (bi, fwd(c), 0)),
                   pl.BlockSpec((1, ts, 512), lambda bi, c: (bi, bwd(c), 0))],
        out_shape=[jax.ShapeDtypeStruct((b, ltot, 512), BF16), jax.ShapeDtypeStruct((b, ltot, 512), BF16)],
        scratch_shapes=[pltpu.VMEM((LANES, 512), F32), pltpu.VMEM((LANES, 512), F32)],
        compiler_params=_cparams(("parallel", "arbitrary")), name="ssd_scan",
    )(xs, bc, dtx, xs, bc, dtx, a_rate_x, tri, neg, hmask)


def _merge_body(gl_ref, bg_ref, cg_ref, cgp_ref, cgn_ref, u_ref, up_ref, un_ref,
                pu_ref, pup_ref, pun_ref, z_ref, att_ref, yf_ref, yb_ref, xs_ref, x_ref, mod_ref,
                cw_ref, band_ref, pw_ref, ps_ref, dsk_ref, sn_ref, wb_ref, wo_ref, o_ref, v_ref, *, tm, l):
    i = pl.program_id(1)
    has_prev = i > 0
    has_next = i < l // tm - 1

    def ext(main_ref, prev_ref, next_ref):
        prev = jnp.where(has_prev, prev_ref[0], jnp.zeros_like(prev_ref[0]))
        nxt = jnp.where(has_next, next_ref[0], jnp.zeros_like(next_ref[0]))
        return jnp.concatenate([prev, main_ref[0], nxt], axis=0)

    v_ref[...] = ext(cg_ref, cgp_ref, cgn_ref).astype(F32) * ext(u_ref, up_ref, un_ref).astype(F32)
    cw = cw_ref[...]
    conv = (cw[0:1] * v_ref[pl.ds(HALO - 1, tm), :] + cw[1:2] * v_ref[pl.ds(HALO, tm), :]
            + cw[2:3] * v_ref[pl.ds(HALO + 1, tm), :])
    br_conv = bg_ref[0].astype(F32) * conv

    pe = ext(pu_ref, pup_ref, pun_ref)
    posm = i * tm + lax.broadcasted_iota(jnp.int32, (tm, LANES), 0)
    pooled = []
    for g, win in enumerate(POOL_WINDOWS):
        cols = slice(g * LANES, (g + 1) * LANES)
        sums = jnp.concatenate(
            [jnp.dot(band_ref[g], pe[k * POOL_BLOCK:k * POOL_BLOCK + POOL_BLOCK + 2 * HALO, cols],
                     preferred_element_type=F32) for k in range(tm // POOL_BLOCK)], axis=0)
        cnt = (jnp.minimum(posm, win // 2) + jnp.minimum(l - posm, win // 2)).astype(F32)
        pm = (sums / cnt - pu_ref[0, :, cols].astype(F32)).astype(BF16)
        pooled.append(jnp.dot(pm, pw_ref[g], preferred_element_type=F32))
    br_pool = jnp.concatenate(pooled, axis=1) * ps_ref[...]

    y = yf_ref[0].astype(F32) + yb_ref[0].astype(F32) + dsk_ref[...] * xs_ref[0].astype(F32)
    gz = y * _silu(z_ref[0].astype(F32))
    br_ssd = gz * lax.rsqrt(jnp.mean(gz * gz, axis=-1, keepdims=True) + NORM_EPS) * sn_ref[...]

    branches = (att_ref[0], br_conv.astype(BF16), br_pool.astype(BF16), br_ssd.astype(BF16))
    acc = None
    for k, br in enumerate(branches):
        gate2 = jnp.tanh(gl_ref[0, :, k * D_MODEL:(k + 1) * D_MODEL].astype(F32)) + 1.0
        term = gate2 * jnp.dot(br, wb_ref[k], preferred_element_type=F32)
        acc = term if acc is None else acc + term
    yo = jnp.dot((0.5 * acc).astype(BF16), wo_ref[...], preferred_element_type=F32)
    o_ref[0] = x_ref[0] + mod_ref[0, 2:3, :] * yo


def _merge(proj, att, yf, yb, xs_c, x, mod, conv_short, band, pool_w, pool_scale, dskip_x, ssd_norm,
           w_branch, w_out, layer, tm, ssd_row_off):
    b, l, _ = x.shape
    per_batch = mod.shape[0] > 1
    off = ssd_row_off // tm
    const2 = lambda bi, i: (0, 0)
    const3 = lambda bi, i: (0, 0, 0)
    in_specs = (
        [pl.BlockSpec((1, tm, 4096), lambda bi, i: (bi, i, 0)),
         pl.BlockSpec((1, tm, 512), lambda bi, i: (bi, i, C_BG // 512))]
        + _halo_specs(tm, l, 512, C_CG // 512) + _halo_specs(tm, l, 512, C_U // 512)
        + _halo_specs(tm, l, 512, C_PU // 512)
        + [pl.BlockSpec((1, tm, 512), lambda bi, i: (bi, i, C_Z // 512)),
           pl.BlockSpec((1, tm, 512), lambda bi, i: (bi, i, 0)),
           pl.BlockSpec((1, tm, 512), lambda bi, i: (bi, i + off, 0)),
           pl.BlockSpec((1, tm, 512), lambda bi, i: (bi, i + off, 0)),
           pl.BlockSpec((1, tm, 512), lambda bi, i: (bi, i + off, 0)),
           pl.BlockSpec((1, tm, D_MODEL), lambda bi, i: (bi, i, 0)),
           pl.BlockSpec((1, 6, D_MODEL), lambda bi, i: (bi if per_batch else 0, 0, 0)),
           pl.BlockSpec((3, 512), const2),
           pl.BlockSpec((4, POOL_BLOCK, POOL_BLOCK + 2 * HALO), const3),
           pl.BlockSpec((None, 4, LANES, LANES), lambda bi, i: (layer, 0, 0, 0)),
           pl.BlockSpec((1, 512), const2),
           pl.BlockSpec((1, 512), const2),
           pl.BlockSpec((1, 512), const2),
           pl.BlockSpec((None, N_BRANCH, BRANCH_WIDTH, D_MODEL), lambda bi, i: (layer, 0, 0, 0)),
           pl.BlockSpec((None, D_MODEL, D_MODEL), lambda bi, i: (layer, 0, 0))])
    return pl.pallas_call(
        functools.partial(_merge_body, tm=tm, l=l), grid=(b, l // tm),
        in_specs=in_specs,
        out_specs=pl.BlockSpec((1, tm, D_MODEL), lambda bi, i: (bi, i, 0)),
        out_shape=jax.ShapeDtypeStruct((b, l, D_MODEL), F32),
        scratch_shapes=[pltpu.VMEM((tm + 2 * HALO, 512), F32)],
        compiler_params=_cparams(("parallel", "parallel")), name="merge",
    )(proj, proj, proj, proj, proj, proj, proj, proj, proj, proj, proj, proj,
      att, yf, yb, xs_c, x, mod, conv_short, band, pool_w, pool_scale, dskip_x, ssd_norm, w_branch, w_out)


FFN_HALO = 8
FFN_CHUNK = 256


def _ffn_body(x_ref, xp_ref, xn_ref, mod_ref, g_ref, wu_ref, cw_ref, wd_ref, fn_ref,
              o_ref, h_ref, ug_ref, uv_ref, a_ref, *, tm, l, final):
    i = pl.program_id(1)
    n_ext = tm + 2 * FFN_HALO
    pos = i * tm - FFN_HALO + lax.broadcasted_iota(jnp.int32, (n_ext, D_MODEL), 0)
    xe = jnp.concatenate([xp_ref[0], x_ref[0], xn_ref[0]], axis=0)
    h = _rms_mod(xe, g_ref[...], mod_ref[0, 4:5, :], mod_ref[0, 3:4, :])
    h_ref[...] = jnp.where((pos >= 0) & (pos < l), h, 0.0).astype(BF16)

    def conv3(u_ref, s, w):
        return (w[0:1] * u_ref[s, pl.ds(FFN_HALO - 1, tm), :] + w[1:2] * u_ref[s, pl.ds(FFN_HALO, tm), :]
                + w[2:3] * u_ref[s, pl.ds(FFN_HALO + 1, tm), :])

    for j in range(D_FF // FFN_CHUNK):
        s = j % 2
        cg = slice(j * FFN_CHUNK, (j + 1) * FFN_CHUNK)
        cv = slice(D_FF + j * FFN_CHUNK, D_FF + (j + 1) * FFN_CHUNK)
        ug_ref[s] = jnp.dot(h_ref[...], wu_ref[:, cg], preferred_element_type=F32)
        uv_ref[s] = jnp.dot(h_ref[...], wu_ref[:, cv], preferred_element_type=F32)
        hg = conv3(ug_ref, s, cw_ref[:, cg])
        a_ref[:, cg] = (hg * (jnp.tanh(hg) + 1.0) * conv3(uv_ref, s, cw_ref[:, cv])).astype(BF16)

    out = x_ref[0] + mod_ref[0, 5:6, :] * jnp.dot(a_ref[...], wd_ref[...], preferred_element_type=F32)
    if final:
        out = out * lax.rsqrt(jnp.mean(out * out, axis=-1, keepdims=True) + NORM_EPS) * fn_ref[...]
    o_ref[0] = out


def _ffn(x, mod, g, w_up, ffn_conv, w_down, final_norm, layer, tm, final):
    b, l, _ = x.shape
    per_batch = mod.shape[0] > 1
    r = tm // FFN_HALO
    last = l // FFN_HALO - 1
    n_ext = tm + 2 * FFN_HALO
    const = lambda bi, i: (0, 0)
    resident = pl.Buffered(1)
    return pl.pallas_call(
        functools.partial(_ffn_body, tm=tm, l=l, final=final), grid=(b, l // tm),
        in_specs=[pl.BlockSpec((1, tm, D_MODEL), lambda bi, i: (bi, i, 0)),
                  pl.BlockSpec((1, FFN_HALO, D_MODEL), lambda bi, i: (bi, jnp.maximum(i * r - 1, 0), 0)),
                  pl.BlockSpec((1, FFN_HALO, D_MODEL), lambda bi, i: (bi, jnp.minimum((i + 1) * r, last), 0)),
                  pl.BlockSpec((1, 6, D_MODEL), lambda bi, i: (bi if per_batch else 0, 0, 0)),
                  pl.BlockSpec((1, D_MODEL), const),
                  pl.BlockSpec((None, D_MODEL, 2 * D_FF), lambda bi, i: (layer, 0, 0), pipeline_mode=resident),
                  pl.BlockSpec((3, 2 * D_FF), const, pipeline_mode=resident),
                  pl.BlockSpec((None, D_FF, D_MODEL), lambda bi, i: (layer, 0, 0), pipeline_mode=resident),
                  pl.BlockSpec((1, D_MODEL), const)],
        out_specs=pl.BlockSpec((1, tm, D_MODEL), lambda bi, i: (bi, i, 0)),
        out_shape=jax.ShapeDtypeStruct((b, l, D_MODEL), F32),
        scratch_shapes=[pltpu.VMEM((n_ext, D_MODEL), BF16), pltpu.VMEM((2, n_ext, FFN_CHUNK), F32),
                        pltpu.VMEM((2, n_ext, FFN_CHUNK), F32), pltpu.VMEM((tm, D_FF), BF16)],
        compiler_params=_cparams(("parallel", "parallel")), name="conv_ffn",
    )(x, x, x, mod, g.reshape(1, D_MODEL), w_up, ffn_conv, w_down, final_norm.reshape(1, D_MODEL))


def _rope_tables(n_tokens):
    rows = n_tokens // GRID_W
    row = jnp.repeat(jnp.arange(rows), GRID_W).astype(F32)
    col = jnp.tile(jnp.arange(GRID_W), rows).astype(F32)
    inv = ROPE_THETA ** (-jnp.arange(ROPE_FREQS, dtype=F32) / ROPE_FREQS)
    ang_r, ang_c = row[:, None] * inv, col[:, None] * inv
    ang = jnp.concatenate([ang_r, ang_r, ang_c, ang_c], axis=1)
    sign = jnp.tile(jnp.concatenate([-jnp.ones(ROPE_FREQS), jnp.ones(ROPE_FREQS)]), 2).astype(F32)
    cos = jnp.tile(jnp.cos(ang), (1, 2))
    sin = jnp.tile(jnp.sin(ang) * sign, (1, 2))
    return cos, sin


def _identity_tables(n):
    return jnp.ones((n, LANES), F32), jnp.zeros((n, LANES), F32)


def _cat_weights(w_in):
    o = np.cumsum([0, 512, 128, 128, 512, 512, 512, 512, 512, 768, 16, 4096])
    wt = jnp.swapaxes(w_in, 1, 2)
    q, k, v, bg, cg, u, pu, z, xbc, dt, gl = [wt[:, o[i]:o[i + 1], :] for i in range(11)]
    dt = jnp.pad(dt, ((0, 0), (0, DT_PAD - dt.shape[1]), (0, 0)))
    gl = 0.5 * gl
    return jnp.concatenate([gl, bg, cg, u, pu, z, q, xbc[:, :512], xbc[:, 512:], k, v, dt],
                           axis=1).astype(BF16)


def kernel(x, c, ctx, c_ctx, w_mod, b_mod, norm_mix, w_in, q_norm, k_norm, conv_short, pool_w, pool_scale,
           ssd_conv_w, ssd_conv_b, ssd_dt_bias, ssd_a_log, ssd_d, ssd_norm, w_branch, w_out, norm_ffn,
           w_up, ffn_conv, w_down, final_norm):
    depth = w_mod.shape[0]
    b, l, _ = x.shape
    lc = ctx.shape[1]
    ltot = l + lc
    t = SSD_CHUNK

    tm_in = min(1024, l)
    tm_mid = min(512, l)
    tq = 512
    tk_lat = 768 if ltot % 768 == 0 else lc
    tk_ctx = lc

    cos_l, sin_l = _rope_tables(l)
    cos_c, sin_c = _identity_tables(lc)
    cos_all = jnp.concatenate([cos_l, cos_c], axis=0)
    sin_all = jnp.concatenate([sin_l, sin_c], axis=0)
    lane = np.arange(LANES)
    bd = jnp.asarray((lane[:, None] // HEAD_DIM) == (lane[None, :] // HEAD_DIM), BF16)
    col512 = np.arange(512)
    expand = np.zeros((2, DT_PAD, 512), np.float32)
    for d in range(2):
        expand[d, d * SSD_HEADS + col512 // HEAD_DIM, col512] = 1.0
    expand = jnp.asarray(expand, BF16)
    pos = np.arange(t)
    tri = jnp.asarray(np.stack([pos[:, None] >= pos[None, :], pos[:, None] <= pos[None, :]]), BF16)
    neg = jnp.asarray(np.where(np.stack([pos[:, None] >= pos[None, :], pos[:, None] <= pos[None, :]]),
                               0.0, -np.inf), F32)
    brow, bcol = np.arange(POOL_BLOCK)[:, None], np.arange(POOL_BLOCK + 2 * HALO)[None, :]
    band = jnp.asarray(np.stack([(bcol >= HALO + brow - w // 2) & (bcol < HALO + brow + w // 2)
                                 for w in POOL_WINDOWS]), BF16)
    hmask = jnp.asarray((lane[:, None] // SSD_STATE) == (col512[None, :] // 256), F32)

    cond8 = jnp.zeros((8, D_MODEL), F32).at[:b].set(c).at[b].set(c_ctx)
    mod_all = _modulation(cond8, w_mod, b_mod).reshape(depth, 8, 6, D_MODEL)
    w_cat = _cat_weights(w_in)
    wb, wo, pw = w_branch.astype(BF16), w_out.astype(BF16), pool_w.astype(BF16)
    gate_half = jnp.concatenate([jnp.full((D_FF,), 0.5, F32), jnp.ones((D_FF,), F32)])
    wu, wd = (w_up * gate_half).astype(BF16), w_down.astype(BF16)
    xc = ctx
    for i in range(depth):
        last = i == depth - 1
        mod_l, mod_c = mod_all[i, :b], mod_all[i, b:b + 1]
        gq = (jnp.tile(q_norm[i], 2) * QK_SCALE).reshape(1, LANES)
        gk = jnp.tile(k_norm[i], 2).reshape(1, LANES)
        neg_bound = jnp.full((1, LANES), -HEAD_DIM * QK_SCALE, F32) * (jnp.max(jnp.abs(q_norm[i]))
                                                                  * jnp.max(jnp.abs(k_norm[i])))
        a_rate_x = jnp.repeat(-jnp.exp(ssd_a_log[i].astype(F32)) * math.log2(math.e),
                              HEAD_DIM, axis=1).reshape(2, 1, 512)
        dskip_x = jnp.repeat(ssd_d[i], HEAD_DIM).reshape(1, 512)
        dt_bias_pad = jnp.pad(ssd_dt_bias[i].reshape(1, 2 * SSD_HEADS), ((0, 0), (0, DT_PAD - 2 * SSD_HEADS)))
        ps = pool_scale[i].reshape(1, 512)
        sn = ssd_norm[i].reshape(1, 512)
        scb = ssd_conv_b[i].reshape(1, 768)

        proj = _inproj(x, mod_l, norm_mix[i], w_cat, i, tm_in)
        proj_c = _inproj(xc, mod_c, norm_mix[i], w_cat, i, lc)

        qh = _qprep(proj, cos_l, sin_l, gq, neg_bound, bd, tm_mid)
        kt, v1 = _kvprep(proj, proj_c, cos_all, sin_all, gk, bd)
        att = _flash(qh, kt, v1, ltot, 0, tq, tk_lat)

        xs_s, bc_s, dtx = _ssdprep(proj, proj_c, ssd_conv_w[i], scb, dt_bias_pad, expand)
        yf, yb = _ssd_scan(xs_s, bc_s, dtx, a_rate_x, tri, neg, hmask, l)

        x_mid = _merge(proj, att, yf, yb, xs_s, x, mod_l, conv_short[i], band, pw, ps, dskip_x, sn, wb, wo, i, tm_mid, 0)
        if not last:
            qh_c = _qprep(proj_c, cos_c, sin_c, gq, neg_bound, bd, lc)
            att_c = _flash(qh_c, kt, v1, lc, l, min(tq, lc), tk_ctx)
            xc_mid = _merge(proj_c, att_c, yf, yb, xs_s, xc, mod_c, conv_short[i], band, pw, ps, dskip_x, sn, wb, wo, i, lc, l)
            xc = _ffn(xc_mid, mod_c, norm_ffn[i], wu, ffn_conv[i], wd, final_norm, i, lc, False)
        x = _ffn(x_mid, mod_l, norm_ffn[i], wu, ffn_conv[i], wd, final_norm, i, tm_mid, last)
    return x
```

```python
import functools
import math

import jax
import jax.numpy as jnp
import numpy as np
from jax import lax
from jax.experimental import pallas as pl
from jax.experimental.pallas import tpu as pltpu

F32 = jnp.float32
BF16 = jnp.bfloat16

D_MODEL = 1024
GRID_W = 64
NORM_EPS = 1e-6
ATT_HEADS = 8
ATT_KV_HEADS = 2
HEAD_DIM = 64
ROPE_THETA = 10000.0
ROPE_FREQS = HEAD_DIM // 4
POOL_WINDOWS = (2, 4, 8, 16)
SSD_HEADS = 8
SSD_STATE = 64
SSD_CHUNK = 128
N_BRANCH = 4
BRANCH_WIDTH = 512
D_FF = 2816

C_GL, C_BG, C_CG, C_U, C_PU, C_Z, C_Q, C_XS, C_BC, C_K, C_V, C_DT = (
    0, 4096, 4608, 5120, 5632, 6144, 6656, 7168, 7680, 7936, 8064, 8192)
N_CAT = 8448
DT_PAD = 256

HALO = 16
POOL_BLOCK = 128
LANES = 128
QK_SCALE = HEAD_DIM ** -0.5 * math.log2(math.e)
ROW_SUM_MIN = 2.0 ** -47
VMEM_LIMIT = 56 * 1024 * 1024


def _cparams(sem):
    return pltpu.CompilerParams(dimension_semantics=sem, vmem_limit_bytes=VMEM_LIMIT)


def _silu(x):
    hx = 0.5 * x
    return hx * (jnp.tanh(hx) + 1.0)


def _split_dot(a, b_bf16):
    hi = a.astype(BF16)
    lo = (a - hi.astype(F32)).astype(BF16)
    return (jnp.dot(hi, b_bf16, preferred_element_type=F32)
            + jnp.dot(lo, b_bf16, preferred_element_type=F32))


def _dot_split_lhs(a_bf16, b):
    hi = b.astype(BF16)
    lo = (b - hi.astype(F32)).astype(BF16)
    return (jnp.dot(a_bf16, hi, preferred_element_type=F32)
            + jnp.dot(a_bf16, lo, preferred_element_type=F32))


def _rms_mod(x, g, scale, shift):
    ms = jnp.mean(x * x, axis=-1, keepdims=True)
    y = x * lax.rsqrt(ms + NORM_EPS) * g
    return y * (1.0 + scale) + shift


def _mod_body(c_ref, w_ref, b_ref, o_ref):
    s = _silu(c_ref[...])
    o_ref[...] = jnp.dot(s, w_ref[...], precision=lax.Precision.HIGHEST,
                         preferred_element_type=F32) + b_ref[...]


def _modulation(cond8, w_mod, b_mod):
    depth, _, n = w_mod.shape
    tn = 1536
    return pl.pallas_call(
        _mod_body, grid=(depth, n // tn),
        in_specs=[pl.BlockSpec((8, D_MODEL), lambda d, j: (0, 0)),
                  pl.BlockSpec((None, D_MODEL, tn), lambda d, j: (d, 0, j)),
                  pl.BlockSpec((None, 1, tn), lambda d, j: (d, 0, j))],
        out_specs=pl.BlockSpec((None, 8, tn), lambda d, j: (d, 0, j)),
        out_shape=jax.ShapeDtypeStruct((depth, 8, n), F32),
        compiler_params=_cparams(("arbitrary", "arbitrary")), name="modulation",
    )(cond8, w_mod, b_mod.reshape(depth, 1, n))


def _inproj_body(x_ref, mod_ref, g_ref, w_ref, o_ref, h_ref):
    @pl.when(pl.program_id(2) == 0)
    def _():
        h = _rms_mod(x_ref[0], g_ref[...], mod_ref[0, 1:2, :], mod_ref[0, 0:1, :])
        h_ref[...] = h.astype(BF16)

    o_ref[0] = lax.dot_general(h_ref[...], w_ref[...], (((1,), (1,)), ((), ())),
                               preferred_element_type=F32).astype(BF16)


def _inproj(x, mod, g, w_cat, layer, tm):
    b, l, _ = x.shape
    n = w_cat.shape[1]
    tn = 2816
    per_batch = mod.shape[0] > 1
    return pl.pallas_call(
        _inproj_body, grid=(b, l // tm, n // tn),
        in_specs=[pl.BlockSpec((1, tm, D_MODEL), lambda bi, i, j: (bi, i, 0)),
                  pl.BlockSpec((1, 6, D_MODEL), lambda bi, i, j: (bi if per_batch else 0, 0, 0)),
                  pl.BlockSpec((1, D_MODEL), lambda bi, i, j: (0, 0)),
                  pl.BlockSpec((None, tn, D_MODEL), lambda bi, i, j: (layer, j, 0))],
        out_specs=pl.BlockSpec((1, tm, tn), lambda bi, i, j: (bi, i, j)),
        out_shape=jax.ShapeDtypeStruct((b, l, n), BF16),
        scratch_shapes=[pltpu.VMEM((tm, D_MODEL), BF16)],
        compiler_params=_cparams(("parallel", "parallel", "arbitrary")), name="inproj",
    )(x, mod, g.reshape(1, D_MODEL), w_cat)


def _head_norm_rope(t, cos, sin, gain, bd):
    ss = _split_dot(t * t, bd)
    y = t * lax.rsqrt(ss * (1.0 / HEAD_DIM) + NORM_EPS) * gain
    lane = lax.broadcasted_iota(jnp.int32, y.shape, 1)
    partner = jnp.where((lane % 32) < 16, pltpu.roll(y, LANES - 16, 1), pltpu.roll(y, 16, 1))
    return y * cos + partner * sin


def _qprep_body(ql_ref, qc_ref, cos_ref, sin_ref, g_ref, negb_ref, bd_ref, o_ref, *, nt_lat):
    q = jnp.where(pl.program_id(1) == nt_lat, qc_ref[0], ql_ref[0])
    cos, sin, gain, bd = cos_ref[...], sin_ref[...], g_ref[...], bd_ref[...]
    lane = lax.broadcasted_iota(jnp.int32, cos.shape, 1)
    for c in range(ATT_HEADS // 2):
        t = q[:, c * LANES:(c + 1) * LANES].astype(F32)
        r = _head_norm_rope(t, cos, sin, gain, bd)
        tail = jnp.where(lane == HEAD_DIM, negb_ref[...], 0.0)
        o_ref[0, 2 * c] = jnp.where(lane < HEAD_DIM, r, tail).astype(BF16)
        o_ref[0, 2 * c + 1] = jnp.where(lane < HEAD_DIM, pltpu.roll(r, HEAD_DIM, 1), tail).astype(BF16)


def _kvprep_body(kl_ref, vl_ref, kc_ref, vc_ref, cos_ref, sin_ref, g_ref, bd_ref, kt_ref, v1_ref, *, nt_lat):
    is_ctx = pl.program_id(1) == nt_lat
    k = jnp.where(is_ctx, kc_ref[0], kl_ref[0]).astype(F32)
    v = jnp.where(is_ctx, vc_ref[0], vl_ref[0]).astype(F32)
    r = _head_norm_rope(k, cos_ref[...], sin_ref[...], g_ref[...], bd_ref[...])
    lane = lax.broadcasted_iota(jnp.int32, r.shape, 1)
    ones_col = jnp.where(lane == HEAD_DIM, 1.0, 0.0)
    for g in range(ATT_KV_HEADS):
        rg = r if g == 0 else pltpu.roll(r, HEAD_DIM, 1)
        vg = v if g == 0 else pltpu.roll(v, HEAD_DIM, 1)
        kt_ref[0, g] = jnp.where(lane < HEAD_DIM, rg, ones_col).T.astype(BF16)
        v1_ref[0, g] = jnp.where(lane < HEAD_DIM, vg, ones_col).astype(BF16)


def _flash_body(q_ref, kt_ref, v_ref, o_ref, s_ref, acc_ref, m_ref, *, tq, tk, nk):
    rep = ATT_HEADS // ATT_KV_HEADS
    rows = rep * tq
    q = q_ref[0].reshape(rows, LANES)

    def keys(c):
        return kt_ref[0, 0, :, pl.ds(pl.multiple_of(c * tk, tk), tk)]

    def vals(c):
        return v_ref[0, 0, pl.ds(pl.multiple_of(c * tk, tk), tk), :]

    s_ref[0] = jnp.dot(q, keys(0), preferred_element_type=F32)
    acc = jnp.zeros((rows, LANES), F32)
    for c in range(nk):
        if c + 1 < nk:
            s_ref[(c + 1) % 2] = jnp.dot(q, keys(c + 1), preferred_element_type=F32)
        p = jnp.exp2(s_ref[c % 2]).astype(BF16)
        acc = acc + jnp.dot(p, vals(c), preferred_element_type=F32)
    acc_ref[...] = acc

    @pl.when(jnp.min(acc[:, HEAD_DIM:HEAD_DIM + 1]) < ROW_SUM_MIN)
    def _():
        m_ref[...] = jnp.full(m_ref.shape, -jnp.inf, F32)
        acc_ref[...] = jnp.zeros(acc_ref.shape, F32)

        def step(c, carry):
            s = jnp.dot(q, keys(c), preferred_element_type=F32)
            m_prev = m_ref[...]
            m_new = jnp.maximum(m_prev, jnp.max(s, axis=1, keepdims=True))
            alpha = jnp.exp2(m_prev - m_new)
            p = jnp.exp2(s - jnp.tile(m_new, (1, tk // LANES))).astype(BF16)
            acc_ref[...] = acc_ref[...] * alpha + jnp.dot(p, vals(c), preferred_element_type=F32)
            m_ref[...] = m_new
            return carry

        lax.fori_loop(0, nk, step, 0)

    acc = acc_ref[...]
    o = acc / acc[:, HEAD_DIM:HEAD_DIM + 1]
    lane = lax.broadcasted_iota(jnp.int32, (tq, LANES), 1)
    for j in range(rep // 2):
        a = o[(2 * j) * tq:(2 * j + 1) * tq]
        bb = o[(2 * j + 1) * tq:(2 * j + 2) * tq]
        o_ref[0, :, j * LANES:(j + 1) * LANES] = jnp.where(
            lane < HEAD_DIM, a, pltpu.roll(bb, HEAD_DIM, 1)).astype(BF16)


def _flash(qh, kt, v1, lq, q_off, lk, k_off, tq, tk):
    b, l = qh.shape[0], lq
    rep = ATT_HEADS // ATT_KV_HEADS
    kb = k_off // lk
    qb = q_off // tq
    return pl.pallas_call(
        functools.partial(_flash_body, tq=tq, tk=tk, nk=lk // tk),
        grid=(b, ATT_KV_HEADS, l // tq),
        in_specs=[pl.BlockSpec((1, rep, tq, LANES), lambda bi, g, i: (bi, g, i + qb, 0)),
                  pl.BlockSpec((1, 1, LANES, lk), lambda bi, g, i: (bi, g, 0, kb)),
                  pl.BlockSpec((1, 1, lk, LANES), lambda bi, g, i: (bi, g, kb, 0))],
        out_specs=pl.BlockSpec((1, tq, rep * HEAD_DIM), lambda bi, g, i: (bi, i, g)),
        out_shape=jax.ShapeDtypeStruct((b, l, ATT_HEADS * HEAD_DIM), BF16),
        scratch_shapes=[pltpu.VMEM((2, rep * tq, tk), F32),
                        pltpu.VMEM((rep * tq, LANES), F32), pltpu.VMEM((rep * tq, LANES), F32)],
        compiler_params=_cparams(("parallel", "parallel", "arbitrary")), name="flash_attention",
    )(qh, kt, v1)


def _halo_specs(tm, l, width, col_block):
    r = tm // HALO
    last = l // HALO - 1
    main = pl.BlockSpec((1, tm, width), lambda bi, i: (bi, i, col_block))
    prev = pl.BlockSpec((1, HALO, width), lambda bi, i: (bi, jnp.maximum(i * r - 1, 0), col_block))
    nxt = pl.BlockSpec((1, HALO, width), lambda bi, i: (bi, jnp.minimum((i + 1) * r, last), col_block))
    return [main, prev, nxt]


def _ssdprep_body(xs_ref, xsp_ref, xsn_ref, bc_ref, bcp_ref, bcn_ref, dt_ref,
                  xsc_ref, bcc_ref, dtc_ref, w_ref, b_ref, dtb_ref, ex_ref,
                  xo_ref, bco_ref, dtx_ref, ex_scr, eb_scr, *, nt_lat, tm):
    t = pl.program_id(1)
    is_ctx = t == nt_lat
    has_prev = (t > 0) & (t < nt_lat)
    has_next = t < nt_lat - 1

    def conv_silu(main_ref, prev_ref, next_ref, ctx_ref, scr, lo, hi):
        main = jnp.where(is_ctx, ctx_ref[0], main_ref[0])
        prev = jnp.where(has_prev, prev_ref[0], jnp.zeros_like(prev_ref[0]))
        nxt = jnp.where(has_next, next_ref[0], jnp.zeros_like(next_ref[0]))
        scr[...] = jnp.concatenate([prev, main, nxt], axis=0).astype(F32)
        w = w_ref[:, lo:hi]
        y = (w[0:1] * scr[pl.ds(HALO - 1, tm), :] + w[1:2] * scr[pl.ds(HALO, tm), :]
             + w[2:3] * scr[pl.ds(HALO + 1, tm), :]) + b_ref[:, lo:hi]
        return _silu(y)

    xo_ref[0] = conv_silu(xs_ref, xsp_ref, xsn_ref, xsc_ref, ex_scr, 0, 512).astype(BF16)
    bco_ref[0] = conv_silu(bc_ref, bcp_ref, bcn_ref, bcc_ref, eb_scr, 512, 768).astype(BF16)
    raw = jnp.where(is_ctx, dtc_ref[0], dt_ref[0]).astype(F32) + dtb_ref[...]
    dt = jnp.maximum(raw, 0.0) + jnp.log1p(jnp.exp(-jnp.abs(raw)))
    for d in range(2):
        dtx_ref[0, d] = _split_dot(dt, ex_ref[d])


def _seqprep_body(*refs, nt_lat, tm):
    n_kv_in, n_ssd_in = 8, 14
    ql_ref, qc_ref, gq_ref, negb_ref = refs[:4]
    kv_in, ssd_in = refs[4:4 + n_kv_in], refs[4 + n_kv_in:4 + n_kv_in + n_ssd_in]
    qo_ref, kt_ref, v1_ref, xo_ref, bco_ref, dtx_ref, ex_scr, eb_scr = refs[4 + n_kv_in + n_ssd_in:]
    cos_ref, sin_ref, bd_ref = kv_in[4], kv_in[5], kv_in[7]
    _qprep_body(ql_ref, qc_ref, cos_ref, sin_ref, gq_ref, negb_ref, bd_ref, qo_ref, nt_lat=nt_lat)
    _kvprep_body(*kv_in, kt_ref, v1_ref, nt_lat=nt_lat)
    _ssdprep_body(*ssd_in, xo_ref, bco_ref, dtx_ref, ex_scr, eb_scr, nt_lat=nt_lat, tm=tm)


def _seqprep(proj, proj_c, cos, sin, gain_q, neg_bound, gain, bd, conv_w, conv_b, dt_bias_pad, expand):
    b, l, _ = proj.shape
    lc = proj_c.shape[1]
    tm = lc
    nt_lat = l // tm
    ltot = l + lc
    r = tm // HALO
    last = l // HALO - 1
    cl = lambda t: jnp.minimum(t, nt_lat - 1)
    const2 = lambda bi, t: (0, 0)

    def lat_specs(width, col):
        return [pl.BlockSpec((1, tm, width), lambda bi, t: (bi, cl(t), col)),
                pl.BlockSpec((1, HALO, width), lambda bi, t: (bi, jnp.maximum(cl(t) * r - 1, 0), col)),
                pl.BlockSpec((1, HALO, width), lambda bi, t: (bi, jnp.minimum((cl(t) + 1) * r, last), col))]

    q_specs = [pl.BlockSpec((1, tm, 512), lambda bi, t: (bi, cl(t), C_Q // 512)),
               pl.BlockSpec((1, tm, 512), lambda bi, t: (bi, 0, C_Q // 512)),
               pl.BlockSpec((1, LANES), const2),
               pl.BlockSpec((1, LANES), const2)]
    kv_specs = [pl.BlockSpec((1, tm, LANES), lambda bi, t: (bi, cl(t), C_K // LANES)),
                pl.BlockSpec((1, tm, LANES), lambda bi, t: (bi, cl(t), C_V // LANES)),
                pl.BlockSpec((1, tm, LANES), lambda bi, t: (bi, 0, C_K // LANES)),
                pl.BlockSpec((1, tm, LANES), lambda bi, t: (bi, 0, C_V // LANES)),
                pl.BlockSpec((tm, LANES), lambda bi, t: (t, 0)),
                pl.BlockSpec((tm, LANES), lambda bi, t: (t, 0)),
                pl.BlockSpec((1, LANES), const2),
                pl.BlockSpec((LANES, LANES), const2)]
    ssd_specs = (lat_specs(512, C_XS // 512) + lat_specs(256, C_BC // 256)
                 + [pl.BlockSpec((1, tm, DT_PAD), lambda bi, t: (bi, cl(t), C_DT // DT_PAD)),
                    pl.BlockSpec((1, tm, 512), lambda bi, t: (bi, 0, C_XS // 512)),
                    pl.BlockSpec((1, tm, 256), lambda bi, t: (bi, 0, C_BC // 256)),
                    pl.BlockSpec((1, tm, DT_PAD), lambda bi, t: (bi, 0, C_DT // DT_PAD)),
                    pl.BlockSpec((3, 768), const2),
                    pl.BlockSpec((1, 768), const2),
                    pl.BlockSpec((1, DT_PAD), const2),
                    pl.BlockSpec((2, DT_PAD, 512), lambda bi, t: (0, 0, 0))])
    return pl.pallas_call(
        functools.partial(_seqprep_body, nt_lat=nt_lat, tm=tm), grid=(b, nt_lat + 1),
        in_specs=q_specs + kv_specs + ssd_specs,
        out_specs=[pl.BlockSpec((1, ATT_HEADS, tm, LANES), lambda bi, t: (bi, 0, t, 0)),
                   pl.BlockSpec((1, ATT_KV_HEADS, LANES, tm), lambda bi, t: (bi, 0, 0, t)),
                   pl.BlockSpec((1, ATT_KV_HEADS, tm, LANES), lambda bi, t: (bi, 0, t, 0)),
                   pl.BlockSpec((1, tm, 512), lambda bi, t: (bi, t, 0)),
                   pl.BlockSpec((1, tm, 256), lambda bi, t: (bi, t, 0)),
                   pl.BlockSpec((1, 2, tm, 512), lambda bi, t: (bi, 0, t, 0))],
        out_shape=[jax.ShapeDtypeStruct((b, ATT_HEADS, ltot, LANES), BF16),
                   jax.ShapeDtypeStruct((b, ATT_KV_HEADS, LANES, ltot), BF16),
                   jax.ShapeDtypeStruct((b, ATT_KV_HEADS, ltot, LANES), BF16),
                   jax.ShapeDtypeStruct((b, ltot, 512), BF16),
                   jax.ShapeDtypeStruct((b, ltot, 256), BF16),
                   jax.ShapeDtypeStruct((b, 2, ltot, 512), F32)],
        scratch_shapes=[pltpu.VMEM((tm + 2 * HALO, 512), F32), pltpu.VMEM((tm + 2 * HALO, 256), F32)],
        compiler_params=_cparams(("parallel", "parallel")), name="seq_prep",
    )(proj, proj_c, gain_q, neg_bound, proj, proj, proj_c, proj_c, cos, sin, gain, bd,
      proj, proj, proj, proj, proj, proj, proj, proj_c, proj_c, proj_c, conv_w, conv_b, dt_bias_pad, expand)


SSD_STEP_CHUNKS = 2


def _ssd_body(xsf_ref, bcf_ref, dtf_ref, xsb_ref, bcb_ref, dtb_ref, ar_ref, tri_ref, neg_ref, hm_ref,
              yf_ref, yb_ref, htf_ref, htb_ref):
    t = SSD_CHUNK

    @pl.when(pl.program_id(1) == 0)
    def _():
        htf_ref[...] = jnp.zeros(htf_ref.shape, F32)
        htb_ref[...] = jnp.zeros(htb_ref.shape, F32)

    lane = lax.broadcasted_iota(jnp.int32, (t, LANES), 1)
    low = lane < HEAD_DIM

    def within_chunk(d, xs_ref, bc_ref, dtx_ref, rows):
        xs = xs_ref[0, rows, :].astype(F32)
        dtx = dtx_ref[0, 0, rows, :]
        cum = _dot_split_lhs(tri_ref[d], dtx * ar_ref[d])
        tot = cum[t - 1:t, :] if d == 0 else cum[0:1, :]
        neg = neg_ref[d]
        xdt = xs * dtx
        xdt_b = xdt.astype(BF16)
        bc = bc_ref[0, rows, :]
        b_all, c_all = bc[:, :LANES], bc[:, LANES:]
        cb = []
        for g in range(2):
            cg = jnp.where(low if g == 0 else ~low, c_all, jnp.zeros_like(c_all))
            cb.append(lax.dot_general(cg, b_all, (((1,), (1,)), ((), ())), preferred_element_type=F32))
        ys = []
        for j in range(SSD_HEADS // 2):
            cp = cum[:, j * LANES:(j + 1) * LANES]
            cps = pltpu.roll(cp, HEAD_DIM, 1)
            cpt = cp.T
            xp = xdt_b[:, j * LANES:(j + 1) * LANES]
            acc = None
            for e in range(2):
                col = jnp.where(low, cp, cps) if e == 0 else jnp.where(low, cps, cp)
                row = jnp.broadcast_to(cpt[e * HEAD_DIM:e * HEAD_DIM + 1, :], (t, t))
                w = (cb[j // 2] * jnp.exp2(col - row + neg)).astype(BF16)
                xm = jnp.where(low if e == 0 else ~low, xp, jnp.zeros_like(xp))
                part = jnp.dot(w, xm, preferred_element_type=F32)
                acc = part if acc is None else acc + part
            ys.append(acc)
        bt = b_all.astype(F32).T.astype(BF16)
        st = jnp.dot(bt, (xdt * jnp.exp2(tot - cum)).astype(BF16), preferred_element_type=F32)
        return jnp.concatenate(ys, axis=1), c_all, jnp.exp2(cum), jnp.exp2(tot), st * hm_ref[...]

    dirs = ((xsf_ref, bcf_ref, dtf_ref, yf_ref, htf_ref), (xsb_ref, bcb_ref, dtb_ref, yb_ref, htb_ref))
    local = []
    for d, (xs_ref, bc_ref, dtx_ref, _, _) in enumerate(dirs):
        order = range(SSD_STEP_CHUNKS) if d == 0 else range(SSD_STEP_CHUNKS - 1, -1, -1)
        rows = [pl.ds(o * t, t) for o in order]
        local.append((rows, [within_chunk(d, xs_ref, bc_ref, dtx_ref, r) for r in rows]))
    for (_, _, _, y_ref, ht_ref), (rows, chunks) in zip(dirs, local):
        ht = ht_ref[...]
        for r, (y_diag, c_all, ecum, etot, st) in zip(rows, chunks):
            y_off = jnp.dot(c_all, ht.astype(BF16), preferred_element_type=F32) * ecum
            y_ref[0, r, :] = (y_diag + y_off).astype(BF16)
            ht = ht * etot + st
        ht_ref[...] = ht


def _ssd_scan(xs, bc, dtx, a_rate_x, tri, neg, hmask, l_lat):
    b, ltot, _ = xs.shape
    t = SSD_CHUNK
    ts = t * SSD_STEP_CHUNKS
    ns = ltot // ts
    nl = l_lat // ts
    nct = ns - nl

    fwd = lambda c: jnp.where(c < nct, nl + c, c - nct)
    bwd = lambda c: jnp.where(c < nct, nl + nct - 1 - c, nl - 1 - (c - nct))
    whole = lambda bi, c: (0, 0, 0)

    def specs(sid, d):
        return [pl.BlockSpec((1, ts, 512), lambda bi, c: (bi, sid(c), 0)),
                pl.BlockSpec((1, ts, 256), lambda bi, c: (bi, sid(c), 0)),
                pl.BlockSpec((1, 1, ts, 512), lambda bi, c: (bi, d, sid(c), 0))]

    return pl.pallas_call(
        _ssd_body, grid=(b, ns),
        in_specs=specs(fwd, 0) + specs(bwd, 1) + [
            pl.BlockSpec((2, 1, 512), whole), pl.BlockSpec((2, t, t), whole), pl.BlockSpec((2, t, t), whole),
            pl.BlockSpec((LANES, 512), lambda bi, c: (0, 0))],
        out_specs=[pl.BlockSpec((1, ts, 512), lambda bi, c: (bi, fwd(c), 0)),
                   pl.BlockSpec((1, ts, 512), lambda bi, c: (bi, bwd(c), 0))],
        out_shape=[jax.ShapeDtypeStruct((b, ltot, 512), BF16), jax.ShapeDtypeStruct((b, ltot, 512), BF16)],
        scratch_shapes=[pltpu.VMEM((LANES, 512), F32), pltpu.VMEM((LANES, 512), F32)],
        compiler_params=_cparams(("parallel", "arbitrary")), name="ssd_scan",
    )(xs, bc, dtx, xs, bc, dtx, a_rate_x, tri, neg, hmask)


def _merge_body(gl_ref, bg_ref, cg_ref, cgp_ref, cgn_ref, u_ref, up_ref, un_ref,
                pu_ref, pup_ref, pun_ref, z_ref, att_ref, yf_ref, yb_ref, xs_ref, x_ref, mod_ref,
                cw_ref, band_ref, pw_ref, ps_ref, dsk_ref, sn_ref, wb_ref, wo_ref, o_ref, v_ref, *, tm, l):
    i = pl.program_id(1)
    has_prev = i > 0
    has_next = i < l // tm - 1

    def ext(main_ref, prev_ref, next_ref):
        prev = jnp.where(has_prev, prev_ref[0], jnp.zeros_like(prev_ref[0]))
        nxt = jnp.where(has_next, next_ref[0], jnp.zeros_like(next_ref[0]))
        return jnp.concatenate([prev, main_ref[0], nxt], axis=0)

    v_ref[...] = ext(cg_ref, cgp_ref, cgn_ref).astype(F32) * ext(u_ref, up_ref, un_ref).astype(F32)
    cw = cw_ref[...]
    conv = (cw[0:1] * v_ref[pl.ds(HALO - 1, tm), :] + cw[1:2] * v_ref[pl.ds(HALO, tm), :]
            + cw[2:3] * v_ref[pl.ds(HALO + 1, tm), :])
    br_conv = bg_ref[0].astype(F32) * conv

    pe = ext(pu_ref, pup_ref, pun_ref)
    posm = i * tm + lax.broadcasted_iota(jnp.int32, (tm, LANES), 0)
    pooled = []
    for g, win in enumerate(POOL_WINDOWS):
        cols = slice(g * LANES, (g + 1) * LANES)
        sums = jnp.concatenate(
            [jnp.dot(band_ref[g], pe[k * POOL_BLOCK:k * POOL_BLOCK + POOL_BLOCK + 2 * HALO, cols],
                     preferred_element_type=F32) for k in range(tm // POOL_BLOCK)], axis=0)
        cnt = (jnp.minimum(posm, win // 2) + jnp.minimum(l - posm, win // 2)).astype(F32)
        pm = (sums / cnt - pu_ref[0, :, cols].astype(F32)).astype(BF16)
        pooled.append(jnp.dot(pm, pw_ref[g], preferred_element_type=F32))
    br_pool = jnp.concatenate(pooled, axis=1) * ps_ref[...]

    y = yf_ref[0].astype(F32) + yb_ref[0].astype(F32) + dsk_ref[...] * xs_ref[0].astype(F32)
    gz = y * _silu(z_ref[0].astype(F32))
    br_ssd = gz * lax.rsqrt(jnp.mean(gz * gz, axis=-1, keepdims=True) + NORM_EPS) * sn_ref[...]

    branches = (att_ref[0], br_conv.astype(BF16), br_pool.astype(BF16), br_ssd.astype(BF16))
    acc = None
    for k, br in enumerate(branches):
        gate2 = jnp.tanh(gl_ref[0, :, k * D_MODEL:(k + 1) * D_MODEL].astype(F32)) + 1.0
        term = gate2 * jnp.dot(br, wb_ref[k], preferred_element_type=F32)
        acc = term if acc is None else acc + term
    yo = jnp.dot((0.5 * acc).astype(BF16), wo_ref[...], preferred_element_type=F32)
    o_ref[0] = x_ref[0] + mod_ref[0, 2:3, :] * yo


def _merge(proj, att, yf, yb, xs_c, x, mod, conv_short, band, pool_w, pool_scale, dskip_x, ssd_norm,
           w_branch, w_out, layer, tm, ssd_row_off):
    b, l, _ = x.shape
    per_batch = mod.shape[0] > 1
    off = ssd_row_off // tm
    const2 = lambda bi, i: (0, 0)
    const3 = lambda bi, i: (0, 0, 0)
    in_specs = (
        [pl.BlockSpec((1, tm, 4096), lambda bi, i: (bi, i, 0)),
         pl.BlockSpec((1, tm, 512), lambda bi, i: (bi, i, C_BG // 512))]
        + _halo_specs(tm, l, 512, C_CG // 512) + _halo_specs(tm, l, 512, C_U // 512)
        + _halo_specs(tm, l, 512, C_PU // 512)
        + [pl.BlockSpec((1, tm, 512), lambda bi, i: (bi, i, C_Z // 512)),
           pl.BlockSpec((1, tm, 512), lambda bi, i: (bi, i, 0)),
           pl.BlockSpec((1, tm, 512), lambda bi, i: (bi, i + off, 0)),
           pl.BlockSpec((1, tm, 512), lambda bi, i: (bi, i + off, 0)),
           pl.BlockSpec((1, tm, 512), lambda bi, i: (bi, i + off, 0)),
           pl.BlockSpec((1, tm, D_MODEL), lambda bi, i: (bi, i, 0)),
           pl.BlockSpec((1, 6, D_MODEL), lambda bi, i: (bi if per_batch else 0, 0, 0)),
           pl.BlockSpec((3, 512), const2),
           pl.BlockSpec((4, POOL_BLOCK, POOL_BLOCK + 2 * HALO), const3),
           pl.BlockSpec((None, 4, LANES, LANES), lambda bi, i: (layer, 0, 0, 0)),
           pl.BlockSpec((1, 512), const2),
           pl.BlockSpec((1, 512), const2),
           pl.BlockSpec((1, 512), const2),
           pl.BlockSpec((None, N_BRANCH, BRANCH_WIDTH, D_MODEL), lambda bi, i: (layer, 0, 0, 0)),
           pl.BlockSpec((None, D_MODEL, D_MODEL), lambda bi, i: (layer, 0, 0))])
    return pl.pallas_call(
        functools.partial(_merge_body, tm=tm, l=l), grid=(b, l // tm),
        in_specs=in_specs,
        out_specs=pl.BlockSpec((1, tm, D_MODEL), lambda bi, i: (bi, i, 0)),
        out_shape=jax.ShapeDtypeStruct((b, l, D_MODEL), F32),
        scratch_shapes=[pltpu.VMEM((tm + 2 * HALO, 512), F32)],
        compiler_params=_cparams(("parallel", "parallel")), name="merge",
    )(proj, proj, proj, proj, proj, proj, proj, proj, proj, proj, proj, proj,
      att, yf, yb, xs_c, x, mod, conv_short, band, pool_w, pool_scale, dskip_x, ssd_norm, w_branch, w_out)


FFN_HALO = 8
FFN_CHUNK = 256


def _ffn_body(x_ref, xp_ref, xn_ref, mod_ref, g_ref, wu_ref, cw_ref, wd_ref, fn_ref,
              o_ref, h_ref, ug_ref, uv_ref, a_ref, *, tm, l, final):
    i = pl.program_id(1)
    n_ext = tm + 2 * FFN_HALO
    pos = i * tm - FFN_HALO + lax.broadcasted_iota(jnp.int32, (n_ext, D_MODEL), 0)
    xe = jnp.concatenate([xp_ref[0], x_ref[0], xn_ref[0]], axis=0)
    h = _rms_mod(xe, g_ref[...], mod_ref[0, 4:5, :], mod_ref[0, 3:4, :])
    h_ref[...] = jnp.where((pos >= 0) & (pos < l), h, 0.0).astype(BF16)

    def conv3(u_ref, s, w):
        return (w[0:1] * u_ref[s, pl.ds(FFN_HALO - 1, tm), :] + w[1:2] * u_ref[s, pl.ds(FFN_HALO, tm), :]
                + w[2:3] * u_ref[s, pl.ds(FFN_HALO + 1, tm), :])

    for j in range(D_FF // FFN_CHUNK):
        s = j % 2
        cg = slice(j * FFN_CHUNK, (j + 1) * FFN_CHUNK)
        cv = slice(D_FF + j * FFN_CHUNK, D_FF + (j + 1) * FFN_CHUNK)
        ug_ref[s] = jnp.dot(h_ref[...], wu_ref[:, cg], preferred_element_type=F32)
        uv_ref[s] = jnp.dot(h_ref[...], wu_ref[:, cv], preferred_element_type=F32)
        hg = conv3(ug_ref, s, cw_ref[:, cg])
        a_ref[:, cg] = (hg * (jnp.tanh(hg) + 1.0) * conv3(uv_ref, s, cw_ref[:, cv])).astype(BF16)

    out = x_ref[0] + mod_ref[0, 5:6, :] * jnp.dot(a_ref[...], wd_ref[...], preferred_element_type=F32)
    if final:
        out = out * lax.rsqrt(jnp.mean(out * out, axis=-1, keepdims=True) + NORM_EPS) * fn_ref[...]
    o_ref[0] = out


def _ffn(x, mod, g, w_up, ffn_conv, w_down, final_norm, layer, tm, final):
    b, l, _ = x.shape
    per_batch = mod.shape[0] > 1
    r = tm // FFN_HALO
    last = l // FFN_HALO - 1
    n_ext = tm + 2 * FFN_HALO
    const = lambda bi, i: (0, 0)
    resident = pl.Buffered(1)
    return pl.pallas_call(
        functools.partial(_ffn_body, tm=tm, l=l, final=final), grid=(b, l // tm),
        in_specs=[pl.BlockSpec((1, tm, D_MODEL), lambda bi, i: (bi, i, 0)),
                  pl.BlockSpec((1, FFN_HALO, D_MODEL), lambda bi, i: (bi, jnp.maximum(i * r - 1, 0), 0)),
                  pl.BlockSpec((1, FFN_HALO, D_MODEL), lambda bi, i: (bi, jnp.minimum((i + 1) * r, last), 0)),
                  pl.BlockSpec((1, 6, D_MODEL), lambda bi, i: (bi if per_batch else 0, 0, 0)),
                  pl.BlockSpec((1, D_MODEL), const),
                  pl.BlockSpec((None, D_MODEL, 2 * D_FF), lambda bi, i: (layer, 0, 0), pipeline_mode=resident),
                  pl.BlockSpec((3, 2 * D_FF), const, pipeline_mode=resident),
                  pl.BlockSpec((None, D_FF, D_MODEL), lambda bi, i: (layer, 0, 0), pipeline_mode=resident),
                  pl.BlockSpec((1, D_MODEL), const)],
        out_specs=pl.BlockSpec((1, tm, D_MODEL), lambda bi, i: (bi, i, 0)),
        out_shape=jax.ShapeDtypeStruct((b, l, D_MODEL), F32),
        scratch_shapes=[pltpu.VMEM((n_ext, D_MODEL), BF16), pltpu.VMEM((2, n_ext, FFN_CHUNK), F32),
                        pltpu.VMEM((2, n_ext, FFN_CHUNK), F32), pltpu.VMEM((tm, D_FF), BF16)],
        compiler_params=_cparams(("parallel", "parallel")), name="conv_ffn",
    )(x, x, x, mod, g.reshape(1, D_MODEL), w_up, ffn_conv, w_down, final_norm.reshape(1, D_MODEL))


def _rope_tables(n_tokens):
    rows = n_tokens // GRID_W
    row = jnp.repeat(jnp.arange(rows), GRID_W).astype(F32)
    col = jnp.tile(jnp.arange(GRID_W), rows).astype(F32)
    inv = ROPE_THETA ** (-jnp.arange(ROPE_FREQS, dtype=F32) / ROPE_FREQS)
    ang_r, ang_c = row[:, None] * inv, col[:, None] * inv
    ang = jnp.concatenate([ang_r, ang_r, ang_c, ang_c], axis=1)
    sign = jnp.tile(jnp.concatenate([-jnp.ones(ROPE_FREQS), jnp.ones(ROPE_FREQS)]), 2).astype(F32)
    cos = jnp.tile(jnp.cos(ang), (1, 2))
    sin = jnp.tile(jnp.sin(ang) * sign, (1, 2))
    return cos, sin


def _identity_tables(n):
    return jnp.ones((n, LANES), F32), jnp.zeros((n, LANES), F32)


def _cat_weights(w_in):
    o = np.cumsum([0, 512, 128, 128, 512, 512, 512, 512, 512, 768, 16, 4096])
    wt = jnp.swapaxes(w_in, 1, 2)
    q, k, v, bg, cg, u, pu, z, xbc, dt, gl = [wt[:, o[i]:o[i + 1], :] for i in range(11)]
    dt = jnp.pad(dt, ((0, 0), (0, DT_PAD - dt.shape[1]), (0, 0)))
    gl = 0.5 * gl
    return jnp.concatenate([gl, bg, cg, u, pu, z, q, xbc[:, :512], xbc[:, 512:], k, v, dt],
                           axis=1).astype(BF16)


def kernel(x, c, ctx, c_ctx, w_mod, b_mod, norm_mix, w_in, q_norm, k_norm, conv_short, pool_w, pool_scale,
           ssd_conv_w, ssd_conv_b, ssd_dt_bias, ssd_a_log, ssd_d, ssd_norm, w_branch, w_out, norm_ffn,
           w_up, ffn_conv, w_down, final_norm):
    depth = w_mod.shape[0]
    b, l, _ = x.shape
    lc = ctx.shape[1]
    ltot = l + lc
    t = SSD_CHUNK

    tm_in = min(1024, l)
    tm_mid = min(512, l)
    tq = 512
    tk_lat = 768 if ltot % 768 == 0 else lc
    tk_ctx = lc

    cos_l, sin_l = _rope_tables(l)
    cos_c, sin_c = _identity_tables(lc)
    cos_all = jnp.concatenate([cos_l, cos_c], axis=0)
    sin_all = jnp.concatenate([sin_l, sin_c], axis=0)
    lane = np.arange(LANES)
    bd = jnp.asarray((lane[:, None] // HEAD_DIM) == (lane[None, :] // HEAD_DIM), BF16)
    col512 = np.arange(512)
    expand = np.zeros((2, DT_PAD, 512), np.float32)
    for d in range(2):
        expand[d, d * SSD_HEADS + col512 // HEAD_DIM, col512] = 1.0
    expand = jnp.asarray(expand, BF16)
    pos = np.arange(t)
    tri = jnp.asarray(np.stack([pos[:, None] >= pos[None, :], pos[:, None] <= pos[None, :]]), BF16)
    neg = jnp.asarray(np.where(np.stack([pos[:, None] >= pos[None, :], pos[:, None] <= pos[None, :]]),
                               0.0, -np.inf), F32)
    brow, bcol = np.arange(POOL_BLOCK)[:, None], np.arange(POOL_BLOCK + 2 * HALO)[None, :]
    band = jnp.asarray(np.stack([(bcol >= HALO + brow - w // 2) & (bcol < HALO + brow + w // 2)
                                 for w in POOL_WINDOWS]), BF16)
    hmask = jnp.asarray((lane[:, None] // SSD_STATE) == (col512[None, :] // 256), F32)

    cond8 = jnp.zeros((8, D_MODEL), F32).at[:b].set(c).at[b].set(c_ctx)
    mod_all = _modulation(cond8, w_mod, b_mod).reshape(depth, 8, 6, D_MODEL)
    w_cat = _cat_weights(w_in)
    wb, wo, pw = w_branch.astype(BF16), w_out.astype(BF16), pool_w.astype(BF16)
    gate_half = jnp.concatenate([jnp.full((D_FF,), 0.5, F32), jnp.ones((D_FF,), F32)])
    wu, wd = (w_up * gate_half).astype(BF16), w_down.astype(BF16)
    xc = ctx
    for i in range(depth):
        last = i == depth - 1
        mod_l, mod_c = mod_all[i, :b], mod_all[i, b:b + 1]
        gq = (jnp.tile(q_norm[i], 2) * QK_SCALE).reshape(1, LANES)
        gk = jnp.tile(k_norm[i], 2).reshape(1, LANES)
        neg_bound = jnp.full((1, LANES), -HEAD_DIM * QK_SCALE, F32) * (jnp.max(jnp.abs(q_norm[i]))
                                                                  * jnp.max(jnp.abs(k_norm[i])))
        a_rate_x = jnp.repeat(-jnp.exp(ssd_a_log[i].astype(F32)) * math.log2(math.e),
                              HEAD_DIM, axis=1).reshape(2, 1, 512)
        dskip_x = jnp.repeat(ssd_d[i], HEAD_DIM).reshape(1, 512)
        dt_bias_pad = jnp.pad(ssd_dt_bias[i].reshape(1, 2 * SSD_HEADS), ((0, 0), (0, DT_PAD - 2 * SSD_HEADS)))
        ps = pool_scale[i].reshape(1, 512)
        sn = ssd_norm[i].reshape(1, 512)
        scb = ssd_conv_b[i].reshape(1, 768)

        proj = _inproj(x, mod_l, norm_mix[i], w_cat, i, tm_in)
        proj_c = _inproj(xc, mod_c, norm_mix[i], w_cat, i, lc)

        qh, kt, v1, xs_s, bc_s, dtx = _seqprep(proj, proj_c, cos_all, sin_all, gq, neg_bound, gk, bd,
                                               ssd_conv_w[i], scb, dt_bias_pad, expand)
        att = _flash(qh, kt, v1, l, 0, ltot, 0, tq, tk_lat)

        yf, yb = _ssd_scan(xs_s, bc_s, dtx, a_rate_x, tri, neg, hmask, l)

        x_mid = _merge(proj, att, yf, yb, xs_s, x, mod_l, conv_short[i], band, pw, ps, dskip_x, sn, wb, wo, i, tm_mid, 0)
        if not last:
            att_c = _flash(qh, kt, v1, lc, l, lc, l, min(tq, lc), tk_ctx)
            xc_mid = _merge(proj_c, att_c, yf, yb, xs_s, xc, mod_c, conv_short[i], band, pw, ps, dskip_x, sn, wb, wo, i, lc, l)
            xc = _ffn(xc_mid, mod_c, norm_ffn[i], wu, ffn_conv[i], wd, final_norm, i, lc, False)
        x = _ffn(x_mid, mod_l, norm_ffn[i], wu, ffn_conv[i], wd, final_norm, i, tm_mid, last)
    return x
```

```python
import functools
import math

import jax
import jax.numpy as jnp
import numpy as np
from jax import lax
from jax.experimental import pallas as pl
from jax.experimental.pallas import tpu as pltpu

F32 = jnp.float32
BF16 = jnp.bfloat16

D_MODEL = 1024
GRID_W = 64
NORM_EPS = 1e-6
ATT_HEADS = 8
ATT_KV_HEADS = 2
HEAD_DIM = 64
ROPE_THETA = 10000.0
ROPE_FREQS = HEAD_DIM // 4
POOL_WINDOWS = (2, 4, 8, 16)
SSD_HEADS = 8
SSD_STATE = 64
SSD_CHUNK = 128
N_BRANCH = 4
BRANCH_WIDTH = 512
D_FF = 2816

C_GL, C_BG, C_CG, C_U, C_PU, C_Z, C_Q, C_XS, C_BC, C_K, C_V, C_DT = (
    0, 4096, 4608, 5120, 5632, 6144, 6656, 7168, 7680, 7936, 8064, 8192)
N_CAT = 8448
DT_PAD = 256

HALO = 16
POOL_BLOCK = 128
LANES = 128
QK_SCALE = HEAD_DIM ** -0.5 * math.log2(math.e)
ROW_SUM_MIN = 2.0 ** -47
VMEM_LIMIT = 56 * 1024 * 1024


def _cparams(sem):
    return pltpu.CompilerParams(dimension_semantics=sem, vmem_limit_bytes=VMEM_LIMIT)


def _silu(x):
    hx = 0.5 * x
    return hx * (jnp.tanh(hx) + 1.0)


def _split_dot(a, b_bf16):
    hi = a.astype(BF16)
    lo = (a - hi.astype(F32)).astype(BF16)
    return (jnp.dot(hi, b_bf16, preferred_element_type=F32)
            + jnp.dot(lo, b_bf16, preferred_element_type=F32))


def _dot_split_lhs(a_bf16, b):
    hi = b.astype(BF16)
    lo = (b - hi.astype(F32)).astype(BF16)
    return (jnp.dot(a_bf16, hi, preferred_element_type=F32)
            + jnp.dot(a_bf16, lo, preferred_element_type=F32))


def _rms_mod(x, g, scale, shift):
    ms = jnp.mean(x * x, axis=-1, keepdims=True)
    return x * lax.rsqrt(ms + NORM_EPS) * (g * (1.0 + scale)) + shift


def _mod_body(c_ref, w_ref, b_ref, o_ref):
    s = _silu(c_ref[...])
    o_ref[...] = jnp.dot(s, w_ref[...], precision=lax.Precision.HIGHEST,
                         preferred_element_type=F32) + b_ref[...]


def _modulation(cond8, w_mod, b_mod):
    depth, _, n = w_mod.shape
    tn = 1536
    return pl.pallas_call(
        _mod_body, grid=(depth, n // tn),
        in_specs=[pl.BlockSpec((8, D_MODEL), lambda d, j: (0, 0)),
                  pl.BlockSpec((None, D_MODEL, tn), lambda d, j: (d, 0, j)),
                  pl.BlockSpec((None, 1, tn), lambda d, j: (d, 0, j))],
        out_specs=pl.BlockSpec((None, 8, tn), lambda d, j: (d, 0, j)),
        out_shape=jax.ShapeDtypeStruct((depth, 8, n), F32),
        compiler_params=_cparams(("arbitrary", "arbitrary")), name="modulation",
    )(cond8, w_mod, b_mod.reshape(depth, 1, n))


def _inproj_body(x_ref, mod_ref, g_ref, w_ref, o_ref, h_ref):
    @pl.when(pl.program_id(2) == 0)
    def _():
        h = _rms_mod(x_ref[0], g_ref[...], mod_ref[0, 1:2, :], mod_ref[0, 0:1, :])
        h_ref[...] = h.astype(BF16)

    o_ref[0] = lax.dot_general(h_ref[...], w_ref[...], (((1,), (1,)), ((), ())),
                               preferred_element_type=F32).astype(BF16)


def _inproj(x, mod, g, w_cat, layer, tm):
    b, l, _ = x.shape
    n = w_cat.shape[1]
    tn = 2816
    per_batch = mod.shape[0] > 1
    return pl.pallas_call(
        _inproj_body, grid=(b, l // tm, n // tn),
        in_specs=[pl.BlockSpec((1, tm, D_MODEL), lambda bi, i, j: (bi, i, 0)),
                  pl.BlockSpec((1, 6, D_MODEL), lambda bi, i, j: (bi if per_batch else 0, 0, 0)),
                  pl.BlockSpec((1, D_MODEL), lambda bi, i, j: (0, 0)),
                  pl.BlockSpec((None, tn, D_MODEL), lambda bi, i, j: (layer, j, 0))],
        out_specs=pl.BlockSpec((1, tm, tn), lambda bi, i, j: (bi, i, j)),
        out_shape=jax.ShapeDtypeStruct((b, l, n), BF16),
        scratch_shapes=[pltpu.VMEM((tm, D_MODEL), BF16)],
        compiler_params=_cparams(("parallel", "parallel", "arbitrary")), name="inproj",
    )(x, mod, g.reshape(1, D_MODEL), w_cat)


def _head_norm_rope(t, cos, sin, gain, bd):
    ss = _split_dot(t * t, bd)
    y = t * lax.rsqrt(ss * (1.0 / HEAD_DIM) + NORM_EPS) * gain
    lane = lax.broadcasted_iota(jnp.int32, y.shape, 1)
    partner = jnp.where((lane % 32) < 16, pltpu.roll(y, LANES - 16, 1), pltpu.roll(y, 16, 1))
    return y * cos + partner * sin


def _qprep_body(ql_ref, qc_ref, cos_ref, sin_ref, g_ref, negb_ref, bd_ref, o_ref, *, nt_lat):
    q = jnp.where(pl.program_id(1) == nt_lat, qc_ref[0], ql_ref[0])
    cos, sin, gain, bd = cos_ref[...], sin_ref[...], g_ref[...], bd_ref[...]
    lane = lax.broadcasted_iota(jnp.int32, cos.shape, 1)
    for c in range(ATT_HEADS // 2):
        t = q[:, c * LANES:(c + 1) * LANES].astype(F32)
        r = _head_norm_rope(t, cos, sin, gain, bd)
        tail = jnp.where(lane == HEAD_DIM, negb_ref[...], 0.0)
        o_ref[0, 2 * c] = jnp.where(lane < HEAD_DIM, r, tail).astype(BF16)
        o_ref[0, 2 * c + 1] = jnp.where(lane < HEAD_DIM, pltpu.roll(r, HEAD_DIM, 1), tail).astype(BF16)


def _kvprep_body(kl_ref, vl_ref, kc_ref, vc_ref, cos_ref, sin_ref, g_ref, bd_ref, kt_ref, v1_ref, *, nt_lat):
    is_ctx = pl.program_id(1) == nt_lat
    k = jnp.where(is_ctx, kc_ref[0], kl_ref[0]).astype(F32)
    v = jnp.where(is_ctx, vc_ref[0], vl_ref[0]).astype(F32)
    r = _head_norm_rope(k, cos_ref[...], sin_ref[...], g_ref[...], bd_ref[...])
    lane = lax.broadcasted_iota(jnp.int32, r.shape, 1)
    ones_col = jnp.where(lane == HEAD_DIM, 1.0, 0.0)
    for g in range(ATT_KV_HEADS):
        rg = r if g == 0 else pltpu.roll(r, HEAD_DIM, 1)
        vg = v if g == 0 else pltpu.roll(v, HEAD_DIM, 1)
        kt_ref[0, g] = jnp.where(lane < HEAD_DIM, rg, ones_col).T.astype(BF16)
        v1_ref[0, g] = jnp.where(lane < HEAD_DIM, vg, ones_col).astype(BF16)


def _flash_body(q_ref, kt_ref, v_ref, o_ref, s_ref, acc_ref, m_ref, *, tq, tk, nk):
    rep = ATT_HEADS // ATT_KV_HEADS
    rows = rep * tq
    q = q_ref[0].reshape(rows, LANES)

    def keys(c):
        return kt_ref[0, 0, :, pl.ds(pl.multiple_of(c * tk, tk), tk)]

    def vals(c):
        return v_ref[0, 0, pl.ds(pl.multiple_of(c * tk, tk), tk), :]

    s_ref[0] = jnp.dot(q, keys(0), preferred_element_type=F32)
    acc = jnp.zeros((rows, LANES), F32)
    for c in range(nk):
        if c + 1 < nk:
            s_ref[(c + 1) % 2] = jnp.dot(q, keys(c + 1), preferred_element_type=F32)
        p = jnp.exp2(s_ref[c % 2]).astype(BF16)
        acc = acc + jnp.dot(p, vals(c), preferred_element_type=F32)
    acc_ref[...] = acc

    @pl.when(jnp.min(acc[:, HEAD_DIM:HEAD_DIM + 1]) < ROW_SUM_MIN)
    def _():
        m_ref[...] = jnp.full(m_ref.shape, -jnp.inf, F32)
        acc_ref[...] = jnp.zeros(acc_ref.shape, F32)

        def step(c, carry):
            s = jnp.dot(q, keys(c), preferred_element_type=F32)
            m_prev = m_ref[...]
            m_new = jnp.maximum(m_prev, jnp.max(s, axis=1, keepdims=True))
            alpha = jnp.exp2(m_prev - m_new)
            p = jnp.exp2(s - jnp.tile(m_new, (1, tk // LANES))).astype(BF16)
            acc_ref[...] = acc_ref[...] * alpha + jnp.dot(p, vals(c), preferred_element_type=F32)
            m_ref[...] = m_new
            return carry

        lax.fori_loop(0, nk, step, 0)

    acc = acc_ref[...]
    o = acc / acc[:, HEAD_DIM:HEAD_DIM + 1]
    lane = lax.broadcasted_iota(jnp.int32, (tq, LANES), 1)
    for j in range(rep // 2):
        a = o[(2 * j) * tq:(2 * j + 1) * tq]
        bb = o[(2 * j + 1) * tq:(2 * j + 2) * tq]
        o_ref[0, :, j * LANES:(j + 1) * LANES] = jnp.where(
            lane < HEAD_DIM, a, pltpu.roll(bb, HEAD_DIM, 1)).astype(BF16)


def _flash(qh, kt, v1, lq, q_off, lk, k_off, tq, tk):
    b, l = qh.shape[0], lq
    rep = ATT_HEADS // ATT_KV_HEADS
    kb = k_off // lk
    qb = q_off // tq
    return pl.pallas_call(
        functools.partial(_flash_body, tq=tq, tk=tk, nk=lk // tk),
        grid=(b, ATT_KV_HEADS, l // tq),
        in_specs=[pl.BlockSpec((1, rep, tq, LANES), lambda bi, g, i: (bi, g, i + qb, 0)),
                  pl.BlockSpec((1, 1, LANES, lk), lambda bi, g, i: (bi, g, 0, kb)),
                  pl.BlockSpec((1, 1, lk, LANES), lambda bi, g, i: (bi, g, kb, 0))],
        out_specs=pl.BlockSpec((1, tq, rep * HEAD_DIM), lambda bi, g, i: (bi, i, g)),
        out_shape=jax.ShapeDtypeStruct((b, l, ATT_HEADS * HEAD_DIM), BF16),
        scratch_shapes=[pltpu.VMEM((2, rep * tq, tk), F32),
                        pltpu.VMEM((rep * tq, LANES), F32), pltpu.VMEM((rep * tq, LANES), F32)],
        compiler_params=_cparams(("parallel", "parallel", "arbitrary")), name="flash_attention",
    )(qh, kt, v1)


def _halo_specs(tm, l, width, col_block):
    r = tm // HALO
    last = l // HALO - 1
    main = pl.BlockSpec((1, tm, width), lambda bi, i: (bi, i, col_block))
    prev = pl.BlockSpec((1, HALO, width), lambda bi, i: (bi, jnp.maximum(i * r - 1, 0), col_block))
    nxt = pl.BlockSpec((1, HALO, width), lambda bi, i: (bi, jnp.minimum((i + 1) * r, last), col_block))
    return [main, prev, nxt]


def _ssdprep_body(xs_ref, xsp_ref, xsn_ref, bc_ref, bcp_ref, bcn_ref, dt_ref,
                  xsc_ref, bcc_ref, dtc_ref, w_ref, b_ref, dtb_ref, ex_ref,
                  xo_ref, bco_ref, dtx_ref, ex_scr, eb_scr, *, nt_lat, tm):
    t = pl.program_id(1)
    is_ctx = t == nt_lat
    has_prev = (t > 0) & (t < nt_lat)
    has_next = t < nt_lat - 1

    def conv_silu(main_ref, prev_ref, next_ref, ctx_ref, scr, lo, hi):
        main = jnp.where(is_ctx, ctx_ref[0], main_ref[0])
        prev = jnp.where(has_prev, prev_ref[0], jnp.zeros_like(prev_ref[0]))
        nxt = jnp.where(has_next, next_ref[0], jnp.zeros_like(next_ref[0]))
        scr[...] = jnp.concatenate([prev, main, nxt], axis=0).astype(F32)
        w = w_ref[:, lo:hi]
        y = (w[0:1] * scr[pl.ds(HALO - 1, tm), :] + w[1:2] * scr[pl.ds(HALO, tm), :]
             + w[2:3] * scr[pl.ds(HALO + 1, tm), :]) + b_ref[:, lo:hi]
        return _silu(y)

    xo_ref[0] = conv_silu(xs_ref, xsp_ref, xsn_ref, xsc_ref, ex_scr, 0, 512).astype(BF16)
    bco_ref[0] = conv_silu(bc_ref, bcp_ref, bcn_ref, bcc_ref, eb_scr, 512, 768).astype(BF16)
    raw = jnp.where(is_ctx, dtc_ref[0], dt_ref[0]).astype(F32) + dtb_ref[...]
    dt = jnp.maximum(raw, 0.0) + jnp.log1p(jnp.exp(-jnp.abs(raw)))
    for d in range(2):
        dtx_ref[0, d] = _split_dot(dt, ex_ref[d])


def _seqprep_body(*refs, nt_lat, tm):
    n_kv_in, n_ssd_in = 8, 14
    ql_ref, qc_ref, gq_ref, negb_ref = refs[:4]
    kv_in, ssd_in = refs[4:4 + n_kv_in], refs[4 + n_kv_in:4 + n_kv_in + n_ssd_in]
    qo_ref, kt_ref, v1_ref, xo_ref, bco_ref, dtx_ref, ex_scr, eb_scr = refs[4 + n_kv_in + n_ssd_in:]
    cos_ref, sin_ref, bd_ref = kv_in[4], kv_in[5], kv_in[7]
    _qprep_body(ql_ref, qc_ref, cos_ref, sin_ref, gq_ref, negb_ref, bd_ref, qo_ref, nt_lat=nt_lat)
    _kvprep_body(*kv_in, kt_ref, v1_ref, nt_lat=nt_lat)
    _ssdprep_body(*ssd_in, xo_ref, bco_ref, dtx_ref, ex_scr, eb_scr, nt_lat=nt_lat, tm=tm)


def _seqprep(proj, proj_c, cos, sin, gain_q, neg_bound, gain, bd, conv_w, conv_b, dt_bias_pad, expand):
    b, l, _ = proj.shape
    lc = proj_c.shape[1]
    tm = lc
    nt_lat = l // tm
    ltot = l + lc
    r = tm // HALO
    last = l // HALO - 1
    cl = lambda t: jnp.minimum(t, nt_lat - 1)
    const2 = lambda bi, t: (0, 0)

    def lat_specs(width, col):
        return [pl.BlockSpec((1, tm, width), lambda bi, t: (bi, cl(t), col)),
                pl.BlockSpec((1, HALO, width), lambda bi, t: (bi, jnp.maximum(cl(t) * r - 1, 0), col)),
                pl.BlockSpec((1, HALO, width), lambda bi, t: (bi, jnp.minimum((cl(t) + 1) * r, last), col))]

    q_specs = [pl.BlockSpec((1, tm, 512), lambda bi, t: (bi, cl(t), C_Q // 512)),
               pl.BlockSpec((1, tm, 512), lambda bi, t: (bi, 0, C_Q // 512)),
               pl.BlockSpec((1, LANES), const2),
               pl.BlockSpec((1, LANES), const2)]
    kv_specs = [pl.BlockSpec((1, tm, LANES), lambda bi, t: (bi, cl(t), C_K // LANES)),
                pl.BlockSpec((1, tm, LANES), lambda bi, t: (bi, cl(t), C_V // LANES)),
                pl.BlockSpec((1, tm, LANES), lambda bi, t: (bi, 0, C_K // LANES)),
                pl.BlockSpec((1, tm, LANES), lambda bi, t: (bi, 0, C_V // LANES)),
                pl.BlockSpec((tm, LANES), lambda bi, t: (t, 0)),
                pl.BlockSpec((tm, LANES), lambda bi, t: (t, 0)),
                pl.BlockSpec((1, LANES), const2),
                pl.BlockSpec((LANES, LANES), const2)]
    ssd_specs = (lat_specs(512, C_XS // 512) + lat_specs(256, C_BC // 256)
                 + [pl.BlockSpec((1, tm, DT_PAD), lambda bi, t: (bi, cl(t), C_DT // DT_PAD)),
                    pl.BlockSpec((1, tm, 512), lambda bi, t: (bi, 0, C_XS // 512)),
                    pl.BlockSpec((1, tm, 256), lambda bi, t: (bi, 0, C_BC // 256)),
                    pl.BlockSpec((1, tm, DT_PAD), lambda bi, t: (bi, 0, C_DT // DT_PAD)),
                    pl.BlockSpec((3, 768), const2),
                    pl.BlockSpec((1, 768), const2),
                    pl.BlockSpec((1, DT_PAD), const2),
                    pl.BlockSpec((2, DT_PAD, 512), lambda bi, t: (0, 0, 0))])
    return pl.pallas_call(
        functools.partial(_seqprep_body, nt_lat=nt_lat, tm=tm), grid=(b, nt_lat + 1),
        in_specs=q_specs + kv_specs + ssd_specs,
        out_specs=[pl.BlockSpec((1, ATT_HEADS, tm, LANES), lambda bi, t: (bi, 0, t, 0)),
                   pl.BlockSpec((1, ATT_KV_HEADS, LANES, tm), lambda bi, t: (bi, 0, 0, t)),
                   pl.BlockSpec((1, ATT_KV_HEADS, tm, LANES), lambda bi, t: (bi, 0, t, 0)),
                   pl.BlockSpec((1, tm, 512), lambda bi, t: (bi, t, 0)),
                   pl.BlockSpec((1, tm, 256), lambda bi, t: (bi, t, 0)),
                   pl.BlockSpec((1, 2, tm, 512), lambda bi, t: (bi, 0, t, 0))],
        out_shape=[jax.ShapeDtypeStruct((b, ATT_HEADS, ltot, LANES), BF16),
                   jax.ShapeDtypeStruct((b, ATT_KV_HEADS, LANES, ltot), BF16),
                   jax.ShapeDtypeStruct((b, ATT_KV_HEADS, ltot, LANES), BF16),
                   jax.ShapeDtypeStruct((b, ltot, 512), BF16),
                   jax.ShapeDtypeStruct((b, ltot, 256), BF16),
                   jax.ShapeDtypeStruct((b, 2, ltot, 512), F32)],
        scratch_shapes=[pltpu.VMEM((tm + 2 * HALO, 512), F32), pltpu.VMEM((tm + 2 * HALO, 256), F32)],
        compiler_params=_cparams(("parallel", "parallel")), name="seq_prep",
    )(proj, proj_c, gain_q, neg_bound, proj, proj, proj_c, proj_c, cos, sin, gain, bd,
      proj, proj, proj, proj, proj, proj, proj, proj_c, proj_c, proj_c, conv_w, conv_b, dt_bias_pad, expand)


SSD_STEP_CHUNKS = 2


def _ssd_body(xsf_ref, bcf_ref, dtf_ref, xsb_ref, bcb_ref, dtb_ref, ar_ref, tri_ref, neg_ref, hm_ref,
              yf_ref, yb_ref, htf_ref, htb_ref):
    t = SSD_CHUNK
    nb = xsf_ref.shape[0]

    @pl.when(pl.program_id(0) == 0)
    def _():
        htf_ref[...] = jnp.zeros(htf_ref.shape, F32)
        htb_ref[...] = jnp.zeros(htb_ref.shape, F32)

    lane = lax.broadcasted_iota(jnp.int32, (t, LANES), 1)
    low = lane < HEAD_DIM

    def within_chunk(bi, d, xs_ref, bc_ref, dtx_ref, rows):
        xs = xs_ref[bi, rows, :].astype(F32)
        dtx = dtx_ref[bi, 0, rows, :]
        cum = _dot_split_lhs(tri_ref[d], dtx * ar_ref[d])
        tot = cum[t - 1:t, :] if d == 0 else cum[0:1, :]
        neg = neg_ref[d]
        xdt = xs * dtx
        xdt_b = xdt.astype(BF16)
        bc = bc_ref[bi, rows, :]
        b_all, c_all = bc[:, :LANES], bc[:, LANES:]
        cb = []
        for g in range(2):
            cg = jnp.where(low if g == 0 else ~low, c_all, jnp.zeros_like(c_all))
            cb.append(lax.dot_general(cg, b_all, (((1,), (1,)), ((), ())), preferred_element_type=F32))
        ys = []
        for j in range(SSD_HEADS // 2):
            cp = cum[:, j * LANES:(j + 1) * LANES]
            cps = pltpu.roll(cp, HEAD_DIM, 1)
            cpt = cp.T
            xp = xdt_b[:, j * LANES:(j + 1) * LANES]
            acc = None
            for e in range(2):
                col = jnp.where(low, cp, cps) if e == 0 else jnp.where(low, cps, cp)
                row = jnp.broadcast_to(cpt[e * HEAD_DIM:e * HEAD_DIM + 1, :], (t, t))
                w = (cb[j // 2] * jnp.exp2(col - row + neg)).astype(BF16)
                xm = jnp.where(low if e == 0 else ~low, xp, jnp.zeros_like(xp))
                part = jnp.dot(w, xm, preferred_element_type=F32)
                acc = part if acc is None else acc + part
            ys.append(acc)
        bt = b_all.astype(F32).T.astype(BF16)
        st = jnp.dot(bt, (xdt * jnp.exp2(tot - cum)).astype(BF16), preferred_element_type=F32)
        return jnp.concatenate(ys, axis=1), c_all, jnp.exp2(cum), jnp.exp2(tot), st * hm_ref[...]

    dirs = ((xsf_ref, bcf_ref, dtf_ref, yf_ref, htf_ref), (xsb_ref, bcb_ref, dtb_ref, yb_ref, htb_ref))
    local = []
    for bi in range(nb):
        for d, (xs_ref, bc_ref, dtx_ref, y_ref, ht_ref) in enumerate(dirs):
            order = range(SSD_STEP_CHUNKS) if d == 0 else range(SSD_STEP_CHUNKS - 1, -1, -1)
            rows = [pl.ds(o * t, t) for o in order]
            local.append((bi, y_ref, ht_ref, rows, [within_chunk(bi, d, xs_ref, bc_ref, dtx_ref, r) for r in rows]))
    for bi, y_ref, ht_ref, rows, chunks in local:
        ht = ht_ref[bi]
        for r, (y_diag, c_all, ecum, etot, st) in zip(rows, chunks):
            y_off = jnp.dot(c_all, ht.astype(BF16), preferred_element_type=F32) * ecum
            y_ref[bi, r, :] = (y_diag + y_off).astype(BF16)
            ht = ht * etot + st
        ht_ref[bi] = ht


def _ssd_scan(xs, bc, dtx, a_rate_x, tri, neg, hmask, l_lat):
    b, ltot, _ = xs.shape
    t = SSD_CHUNK
    ts = t * SSD_STEP_CHUNKS
    ns = ltot // ts
    nl = l_lat // ts
    nct = ns - nl

    fwd = lambda c: jnp.where(c < nct, nl + c, c - nct)
    bwd = lambda c: jnp.where(c < nct, nl + nct - 1 - c, nl - 1 - (c - nct))
    whole = lambda c: (0, 0, 0)

    def specs(sid, d):
        return [pl.BlockSpec((b, ts, 512), lambda c: (0, sid(c), 0)),
                pl.BlockSpec((b, ts, 256), lambda c: (0, sid(c), 0)),
                pl.BlockSpec((b, 1, ts, 512), lambda c: (0, d, sid(c), 0))]

    return pl.pallas_call(
        _ssd_body, grid=(ns,),
        in_specs=specs(fwd, 0) + specs(bwd, 1) + [
            pl.BlockSpec((2, 1, 512), whole), pl.BlockSpec((2, t, t), whole), pl.BlockSpec((2, t, t), whole),
            pl.BlockSpec((LANES, 512), lambda c: (0, 0))],
        out_specs=[pl.BlockSpec((b, ts, 512), lambda c: (0, fwd(c), 0)),
                   pl.BlockSpec((b, ts, 512), lambda c: (0, bwd(c), 0))],
        out_shape=[jax.ShapeDtypeStruct((b, ltot, 512), BF16), jax.ShapeDtypeStruct((b, ltot, 512), BF16)],
        scratch_shapes=[pltpu.VMEM((b, LANES, 512), F32), pltpu.VMEM((b, LANES, 512), F32)],
        compiler_params=_cparams(("arbitrary",)), name="ssd_scan",
    )(xs, bc, dtx, xs, bc, dtx, a_rate_x, tri, neg, hmask)


def _merge_body(gl_ref, bg_ref, cg_ref, cgp_ref, cgn_ref, u_ref, up_ref, un_ref,
                pu_ref, pup_ref, pun_ref, z_ref, att_ref, yf_ref, yb_ref, xs_ref, x_ref, mod_ref,
                cw_ref, band_ref, pw_ref, ps_ref, dsk_ref, sn_ref, wb_ref, wo_ref, o_ref, v_ref, *, tm, l):
    i = pl.program_id(1)
    has_prev = i > 0
    has_next = i < l // tm - 1

    def ext(main_ref, prev_ref, next_ref):
        prev = jnp.where(has_prev, prev_ref[0], jnp.zeros_like(prev_ref[0]))
        nxt = jnp.where(has_next, next_ref[0], jnp.zeros_like(next_ref[0]))
        return jnp.concatenate([prev, main_ref[0], nxt], axis=0)

    v_ref[...] = ext(cg_ref, cgp_ref, cgn_ref).astype(F32) * ext(u_ref, up_ref, un_ref).astype(F32)
    cw = cw_ref[...]
    conv = (cw[0:1] * v_ref[pl.ds(HALO - 1, tm), :] + cw[1:2] * v_ref[pl.ds(HALO, tm), :]
            + cw[2:3] * v_ref[pl.ds(HALO + 1, tm), :])
    br_conv = bg_ref[0].astype(F32) * conv

    pe = ext(pu_ref, pup_ref, pun_ref)
    posm = i * tm + lax.broadcasted_iota(jnp.int32, (tm, LANES), 0)
    pooled = []
    for g, win in enumerate(POOL_WINDOWS):
        cols = slice(g * LANES, (g + 1) * LANES)
        sums = jnp.concatenate(
            [jnp.dot(band_ref[g], pe[k * POOL_BLOCK:k * POOL_BLOCK + POOL_BLOCK + 2 * HALO, cols],
                     preferred_element_type=F32) for k in range(tm // POOL_BLOCK)], axis=0)
        cnt = (jnp.minimum(posm, win // 2) + jnp.minimum(l - posm, win // 2)).astype(F32)
        pm = (sums / cnt - pu_ref[0, :, cols].astype(F32)).astype(BF16)
        pooled.append(jnp.dot(pm, pw_ref[g], preferred_element_type=F32))
    br_pool = jnp.concatenate(pooled, axis=1) * ps_ref[...]

    y = yf_ref[0].astype(F32) + yb_ref[0].astype(F32) + dsk_ref[...] * xs_ref[0].astype(F32)
    gz = y * _silu(z_ref[0].astype(F32))
    br_ssd = gz * lax.rsqrt(jnp.mean(gz * gz, axis=-1, keepdims=True) + NORM_EPS) * sn_ref[...]

    branches = (att_ref[0], br_conv.astype(BF16), br_pool.astype(BF16), br_ssd.astype(BF16))
    acc = None
    for k, br in enumerate(branches):
        gate2 = jnp.tanh(gl_ref[0, :, k * D_MODEL:(k + 1) * D_MODEL].astype(F32)) + 1.0
        term = gate2 * jnp.dot(br, wb_ref[k], preferred_element_type=F32)
        acc = term if acc is None else acc + term
    yo = jnp.dot((0.5 * acc).astype(BF16), wo_ref[...], preferred_element_type=F32)
    o_ref[0] = x_ref[0] + mod_ref[0, 2:3, :] * yo


def _merge(proj, att, yf, yb, xs_c, x, mod, conv_short, band, pool_w, pool_scale, dskip_x, ssd_norm,
           w_branch, w_out, layer, tm, ssd_row_off):
    b, l, _ = x.shape
    per_batch = mod.shape[0] > 1
    off = ssd_row_off // tm
    const2 = lambda bi, i: (0, 0)
    const3 = lambda bi, i: (0, 0, 0)
    in_specs = (
        [pl.BlockSpec((1, tm, 4096), lambda bi, i: (bi, i, 0)),
         pl.BlockSpec((1, tm, 512), lambda bi, i: (bi, i, C_BG // 512))]
        + _halo_specs(tm, l, 512, C_CG // 512) + _halo_specs(tm, l, 512, C_U // 512)
        + _halo_specs(tm, l, 512, C_PU // 512)
        + [pl.BlockSpec((1, tm, 512), lambda bi, i: (bi, i, C_Z // 512)),
           pl.BlockSpec((1, tm, 512), lambda bi, i: (bi, i, 0)),
           pl.BlockSpec((1, tm, 512), lambda bi, i: (bi, i + off, 0)),
           pl.BlockSpec((1, tm, 512), lambda bi, i: (bi, i + off, 0)),
           pl.BlockSpec((1, tm, 512), lambda bi, i: (bi, i + off, 0)),
           pl.BlockSpec((1, tm, D_MODEL), lambda bi, i: (bi, i, 0)),
           pl.BlockSpec((1, 6, D_MODEL), lambda bi, i: (bi if per_batch else 0, 0, 0)),
           pl.BlockSpec((3, 512), const2),
           pl.BlockSpec((4, POOL_BLOCK, POOL_BLOCK + 2 * HALO), const3),
           pl.BlockSpec((None, 4, LANES, LANES), lambda bi, i: (layer, 0, 0, 0)),
           pl.BlockSpec((1, 512), const2),
           pl.BlockSpec((1, 512), const2),
           pl.BlockSpec((1, 512), const2),
           pl.BlockSpec((None, N_BRANCH, BRANCH_WIDTH, D_MODEL), lambda bi, i: (layer, 0, 0, 0)),
           pl.BlockSpec((None, D_MODEL, D_MODEL), lambda bi, i: (layer, 0, 0))])
    return pl.pallas_call(
        functools.partial(_merge_body, tm=tm, l=l), grid=(b, l // tm),
        in_specs=in_specs,
        out_specs=pl.BlockSpec((1, tm, D_MODEL), lambda bi, i: (bi, i, 0)),
        out_shape=jax.ShapeDtypeStruct((b, l, D_MODEL), F32),
        scratch_shapes=[pltpu.VMEM((tm + 2 * HALO, 512), F32)],
        compiler_params=_cparams(("parallel", "parallel")), name="merge",
    )(proj, proj, proj, proj, proj, proj, proj, proj, proj, proj, proj, proj,
      att, yf, yb, xs_c, x, mod, conv_short, band, pool_w, pool_scale, dskip_x, ssd_norm, w_branch, w_out)


FFN_HALO = 8
FFN_CHUNK = 256


def _ffn_body(x_ref, xp_ref, xn_ref, mod_ref, g_ref, wu_ref, cw_ref, wd_ref, fn_ref,
              o_ref, h_ref, ug_ref, uv_ref, a_ref, *, tm, l, final):
    i = pl.program_id(1)
    n_ext = tm + 2 * FFN_HALO
    pos = i * tm - FFN_HALO + lax.broadcasted_iota(jnp.int32, (n_ext, D_MODEL), 0)
    xe = jnp.concatenate([xp_ref[0], x_ref[0], xn_ref[0]], axis=0)
    h = _rms_mod(xe, g_ref[...], mod_ref[0, 4:5, :], mod_ref[0, 3:4, :])
    h_ref[...] = jnp.where((pos >= 0) & (pos < l), h, 0.0).astype(BF16)

    def conv3(u_ref, s, w):
        return (w[0:1] * u_ref[s, pl.ds(FFN_HALO - 1, tm), :] + w[1:2] * u_ref[s, pl.ds(FFN_HALO, tm), :]
                + w[2:3] * u_ref[s, pl.ds(FFN_HALO + 1, tm), :])

    for j in range(D_FF // FFN_CHUNK):
        s = j % 2
        cg = slice(j * FFN_CHUNK, (j + 1) * FFN_CHUNK)
        cv = slice(D_FF + j * FFN_CHUNK, D_FF + (j + 1) * FFN_CHUNK)
        ug_ref[s] = jnp.dot(h_ref[...], wu_ref[:, cg], preferred_element_type=F32)
        uv_ref[s] = jnp.dot(h_ref[...], wu_ref[:, cv], preferred_element_type=F32)
        hg = conv3(ug_ref, s, cw_ref[:, cg])
        a_ref[:, cg] = (hg * (jnp.tanh(hg) + 1.0) * conv3(uv_ref, s, cw_ref[:, cv])).astype(BF16)

    out = x_ref[0] + mod_ref[0, 5:6, :] * jnp.dot(a_ref[...], wd_ref[...], preferred_element_type=F32)
    if final:
        out = out * lax.rsqrt(jnp.mean(out * out, axis=-1, keepdims=True) + NORM_EPS) * fn_ref[...]
    o_ref[0] = out


def _ffn(x, mod, g, w_up, ffn_conv, w_down, final_norm, layer, tm, final):
    b, l, _ = x.shape
    per_batch = mod.shape[0] > 1
    r = tm // FFN_HALO
    last = l // FFN_HALO - 1
    n_ext = tm + 2 * FFN_HALO
    const = lambda bi, i: (0, 0)
    resident = pl.Buffered(1)
    return pl.pallas_call(
        functools.partial(_ffn_body, tm=tm, l=l, final=final), grid=(b, l // tm),
        in_specs=[pl.BlockSpec((1, tm, D_MODEL), lambda bi, i: (bi, i, 0)),
                  pl.BlockSpec((1, FFN_HALO, D_MODEL), lambda bi, i: (bi, jnp.maximum(i * r - 1, 0), 0)),
                  pl.BlockSpec((1, FFN_HALO, D_MODEL), lambda bi, i: (bi, jnp.minimum((i + 1) * r, last), 0)),
                  pl.BlockSpec((1, 6, D_MODEL), lambda bi, i: (bi if per_batch else 0, 0, 0)),
                  pl.BlockSpec((1, D_MODEL), const),
                  pl.BlockSpec((None, D_MODEL, 2 * D_FF), lambda bi, i: (layer, 0, 0), pipeline_mode=resident),
                  pl.BlockSpec((3, 2 * D_FF), const, pipeline_mode=resident),
                  pl.BlockSpec((None, D_FF, D_MODEL), lambda bi, i: (layer, 0, 0), pipeline_mode=resident),
                  pl.BlockSpec((1, D_MODEL), const)],
        out_specs=pl.BlockSpec((1, tm, D_MODEL), lambda bi, i: (bi, i, 0)),
        out_shape=jax.ShapeDtypeStruct((b, l, D_MODEL), F32),
        scratch_shapes=[pltpu.VMEM((n_ext, D_MODEL), BF16), pltpu.VMEM((2, n_ext, FFN_CHUNK), F32),
                        pltpu.VMEM((2, n_ext, FFN_CHUNK), F32), pltpu.VMEM((tm, D_FF), BF16)],
        compiler_params=_cparams(("parallel", "parallel")), name="conv_ffn",
    )(x, x, x, mod, g.reshape(1, D_MODEL), w_up, ffn_conv, w_down, final_norm.reshape(1, D_MODEL))


def _rope_tables(n_tokens):
    rows = n_tokens // GRID_W
    row = jnp.repeat(jnp.arange(rows), GRID_W).astype(F32)
    col = jnp.tile(jnp.arange(GRID_W), rows).astype(F32)
    inv = ROPE_THETA ** (-jnp.arange(ROPE_FREQS, dtype=F32) / ROPE_FREQS)
    ang_r, ang_c = row[:, None] * inv, col[:, None] * inv
    ang = jnp.concatenate([ang_r, ang_r, ang_c, ang_c], axis=1)
    sign = jnp.tile(jnp.concatenate([-jnp.ones(ROPE_FREQS), jnp.ones(ROPE_FREQS)]), 2).astype(F32)
    cos = jnp.tile(jnp.cos(ang), (1, 2))
    sin = jnp.tile(jnp.sin(ang) * sign, (1, 2))
    return cos, sin


def _identity_tables(n):
    return jnp.ones((n, LANES), F32), jnp.zeros((n, LANES), F32)


def _cat_weights(w_in):
    o = np.cumsum([0, 512, 128, 128, 512, 512, 512, 512, 512, 768, 16, 4096])
    wt = jnp.swapaxes(w_in, 1, 2)
    q, k, v, bg, cg, u, pu, z, xbc, dt, gl = [wt[:, o[i]:o[i + 1], :] for i in range(11)]
    dt = jnp.pad(dt, ((0, 0), (0, DT_PAD - dt.shape[1]), (0, 0)))
    gl = 0.5 * gl
    return jnp.concatenate([gl, bg, cg, u, pu, z, q, xbc[:, :512], xbc[:, 512:], k, v, dt],
                           axis=1).astype(BF16)


def kernel(x, c, ctx, c_ctx, w_mod, b_mod, norm_mix, w_in, q_norm, k_norm, conv_short, pool_w, pool_scale,
           ssd_conv_w, ssd_conv_b, ssd_dt_bias, ssd_a_log, ssd_d, ssd_norm, w_branch, w_out, norm_ffn,
           w_up, ffn_conv, w_down, final_norm):
    depth = w_mod.shape[0]
    b, l, _ = x.shape
    lc = ctx.shape[1]
    ltot = l + lc
    t = SSD_CHUNK

    tm_in = min(1024, l)
    tm_mid = min(512, l)
    tq = 512
    tk_lat = 768 if ltot % 768 == 0 else lc
    tk_ctx = lc

    cos_l, sin_l = _rope_tables(l)
    cos_c, sin_c = _identity_tables(lc)
    cos_all = jnp.concatenate([cos_l, cos_c], axis=0)
    sin_all = jnp.concatenate([sin_l, sin_c], axis=0)
    lane = np.arange(LANES)
    bd = jnp.asarray((lane[:, None] // HEAD_DIM) == (lane[None, :] // HEAD_DIM), BF16)
    col512 = np.arange(512)
    expand = np.zeros((2, DT_PAD, 512), np.float32)
    for d in range(2):
        expand[d, d * SSD_HEADS + col512 // HEAD_DIM, col512] = 1.0
    expand = jnp.asarray(expand, BF16)
    pos = np.arange(t)
    tri = jnp.asarray(np.stack([pos[:, None] >= pos[None, :], pos[:, None] <= pos[None, :]]), BF16)
    neg = jnp.asarray(np.where(np.stack([pos[:, None] >= pos[None, :], pos[:, None] <= pos[None, :]]),
                               0.0, -np.inf), F32)
    brow, bcol = np.arange(POOL_BLOCK)[:, None], np.arange(POOL_BLOCK + 2 * HALO)[None, :]
    band = jnp.asarray(np.stack([(bcol >= HALO + brow - w // 2) & (bcol < HALO + brow + w // 2)
                                 for w in POOL_WINDOWS]), BF16)
    hmask = jnp.asarray((lane[:, None] // SSD_STATE) == (col512[None, :] // 256), F32)

    cond8 = jnp.zeros((8, D_MODEL), F32).at[:b].set(c).at[b].set(c_ctx)
    mod_all = _modulation(cond8, w_mod, b_mod).reshape(depth, 8, 6, D_MODEL)
    w_cat = _cat_weights(w_in)
    wb, wo, pw = w_branch.astype(BF16), w_out.astype(BF16), pool_w.astype(BF16)
    gate_half = jnp.concatenate([jnp.full((D_FF,), 0.5, F32), jnp.ones((D_FF,), F32)])
    wu, wd = (w_up * gate_half).astype(BF16), w_down.astype(BF16)
    xc = ctx
    for i in range(depth):
        last = i == depth - 1
        mod_l, mod_c = mod_all[i, :b], mod_all[i, b:b + 1]
        gq = (jnp.tile(q_norm[i], 2) * QK_SCALE).reshape(1, LANES)
        gk = jnp.tile(k_norm[i], 2).reshape(1, LANES)
        neg_bound = jnp.full((1, LANES), -HEAD_DIM * QK_SCALE, F32) * (jnp.max(jnp.abs(q_norm[i]))
                                                                  * jnp.max(jnp.abs(k_norm[i])))
        a_rate_x = jnp.repeat(-jnp.exp(ssd_a_log[i].astype(F32)) * math.log2(math.e),
                              HEAD_DIM, axis=1).reshape(2, 1, 512)
        dskip_x = jnp.repeat(ssd_d[i], HEAD_DIM).reshape(1, 512)
        dt_bias_pad = jnp.pad(ssd_dt_bias[i].reshape(1, 2 * SSD_HEADS), ((0, 0), (0, DT_PAD - 2 * SSD_HEADS)))
        ps = pool_scale[i].reshape(1, 512)
        sn = ssd_norm[i].reshape(1, 512)
        scb = ssd_conv_b[i].reshape(1, 768)

        proj = _inproj(x, mod_l, norm_mix[i], w_cat, i, tm_in)
        proj_c = _inproj(xc, mod_c, norm_mix[i], w_cat, i, lc)

        qh, kt, v1, xs_s, bc_s, dtx = _seqprep(proj, proj_c, cos_all, sin_all, gq, neg_bound, gk, bd,
                                               ssd_conv_w[i], scb, dt_bias_pad, expand)
        att = _flash(qh, kt, v1, l, 0, ltot, 0, tq, tk_lat)

        yf, yb = _ssd_scan(xs_s, bc_s, dtx, a_rate_x, tri, neg, hmask, l)

        x_mid = _merge(proj, att, yf, yb, xs_s, x, mod_l, conv_short[i], band, pw, ps, dskip_x, sn, wb, wo, i, tm_mid, 0)
        if not last:
            att_c = _flash(qh, kt, v1, lc, l, lc, l, min(tq, lc), tk_ctx)
            xc_mid = _merge(proj_c, att_c, yf, yb, xs_s, xc, mod_c, conv_short[i], band, pw, ps, dskip_x, sn, wb, wo, i, lc, l)
            xc = _ffn(xc_mid, mod_c, norm_ffn[i], wu, ffn_conv[i], wd, final_norm, i, lc, False)
        x = _ffn(x_mid, mod_l, norm_ffn[i], wu, ffn_conv[i], wd, final_norm, i, tm_mid, last)
    return x
```

```python
import functools
import math

import jax
import jax.numpy as jnp
import numpy as np
from jax import lax
from jax.experimental import pallas as pl
from jax.experimental.pallas import tpu as pltpu

F32 = jnp.float32
BF16 = jnp.bfloat16

D_MODEL = 1024
GRID_W = 64
NORM_EPS = 1e-6
ATT_HEADS = 8
ATT_KV_HEADS = 2
HEAD_DIM = 64
ROPE_THETA = 10000.0
ROPE_FREQS = HEAD_DIM // 4
POOL_WINDOWS = (2, 4, 8, 16)
SSD_HEADS = 8
SSD_STATE = 64
SSD_CHUNK = 128
N_BRANCH = 4
BRANCH_WIDTH = 512
D_FF = 2816

C_GL, C_BG, C_CG, C_U, C_PU, C_Z, C_Q, C_XS, C_BC, C_K, C_V, C_DT = (
    0, 4096, 4608, 5120, 5632, 6144, 6656, 7168, 7680, 7936, 8064, 8192)
N_CAT = 8448
DT_PAD = 256

HALO = 16
POOL_BLOCK = 128
LANES = 128
QK_SCALE = HEAD_DIM ** -0.5 * math.log2(math.e)
ROW_SUM_MIN = 2.0 ** -47
VMEM_LIMIT = 56 * 1024 * 1024


def _cparams(sem):
    return pltpu.CompilerParams(dimension_semantics=sem, vmem_limit_bytes=VMEM_LIMIT)


def _silu(x):
    hx = 0.5 * x
    return hx * (jnp.tanh(hx) + 1.0)


def _split_dot(a, b_bf16):
    hi = a.astype(BF16)
    lo = (a - hi.astype(F32)).astype(BF16)
    return (jnp.dot(hi, b_bf16, preferred_element_type=F32)
            + jnp.dot(lo, b_bf16, preferred_element_type=F32))


def _dot_split_lhs(a_bf16, b):
    hi = b.astype(BF16)
    lo = (b - hi.astype(F32)).astype(BF16)
    return (jnp.dot(a_bf16, hi, preferred_element_type=F32)
            + jnp.dot(a_bf16, lo, preferred_element_type=F32))


def _rms_mod(x, g, scale, shift):
    ms = jnp.mean(x * x, axis=-1, keepdims=True)
    return x * lax.rsqrt(ms + NORM_EPS) * (g * (1.0 + scale)) + shift


def _mod_body(c_ref, w_ref, b_ref, o_ref):
    s = _silu(c_ref[...])
    o_ref[...] = jnp.dot(s, w_ref[...], precision=lax.Precision.HIGHEST,
                         preferred_element_type=F32) + b_ref[...]


def _modulation(cond8, w_mod, b_mod):
    depth, _, n = w_mod.shape
    tn = 1536
    return pl.pallas_call(
        _mod_body, grid=(depth, n // tn),
        in_specs=[pl.BlockSpec((8, D_MODEL), lambda d, j: (0, 0)),
                  pl.BlockSpec((None, D_MODEL, tn), lambda d, j: (d, 0, j)),
                  pl.BlockSpec((None, 1, tn), lambda d, j: (d, 0, j))],
        out_specs=pl.BlockSpec((None, 8, tn), lambda d, j: (d, 0, j)),
        out_shape=jax.ShapeDtypeStruct((depth, 8, n), F32),
        compiler_params=_cparams(("arbitrary", "arbitrary")), name="modulation",
    )(cond8, w_mod, b_mod.reshape(depth, 1, n))


def _inproj_body(x_ref, mod_ref, g_ref, w_ref, o_ref, h_ref):
    @pl.when(pl.program_id(2) == 0)
    def _():
        h = _rms_mod(x_ref[0], g_ref[...], mod_ref[0, 1:2, :], mod_ref[0, 0:1, :])
        h_ref[...] = h.astype(BF16)

    o_ref[0] = lax.dot_general(h_ref[...], w_ref[...], (((1,), (1,)), ((), ())),
                               preferred_element_type=F32).astype(BF16)


def _inproj(x, mod, g, w_cat, layer, tm):
    b, l, _ = x.shape
    n = w_cat.shape[1]
    tn = 2816
    per_batch = mod.shape[0] > 1
    return pl.pallas_call(
        _inproj_body, grid=(b, l // tm, n // tn),
        in_specs=[pl.BlockSpec((1, tm, D_MODEL), lambda bi, i, j: (bi, i, 0)),
                  pl.BlockSpec((1, 6, D_MODEL), lambda bi, i, j: (bi if per_batch else 0, 0, 0)),
                  pl.BlockSpec((1, D_MODEL), lambda bi, i, j: (0, 0)),
                  pl.BlockSpec((None, tn, D_MODEL), lambda bi, i, j: (layer, j, 0))],
        out_specs=pl.BlockSpec((1, tm, tn), lambda bi, i, j: (bi, i, j)),
        out_shape=jax.ShapeDtypeStruct((b, l, n), BF16),
        scratch_shapes=[pltpu.VMEM((tm, D_MODEL), BF16)],
        compiler_params=_cparams(("parallel", "parallel", "arbitrary")), name="inproj",
    )(x, mod, g.reshape(1, D_MODEL), w_cat)


def _head_norm_rope(t, cos, sin, gain, bd):
    ss = _split_dot(t * t, bd)
    y = t * lax.rsqrt(ss * (1.0 / HEAD_DIM) + NORM_EPS) * gain
    lane = lax.broadcasted_iota(jnp.int32, y.shape, 1)
    partner = jnp.where((lane % 32) < 16, pltpu.roll(y, LANES - 16, 1), pltpu.roll(y, 16, 1))
    return y * cos + partner * sin


def _qprep_body(ql_ref, qc_ref, cos_ref, sin_ref, g_ref, negb_ref, bd_ref, o_ref, *, nt_lat):
    q = jnp.where(pl.program_id(1) == nt_lat, qc_ref[0], ql_ref[0])
    cos, sin, gain, bd = cos_ref[...], sin_ref[...], g_ref[...], bd_ref[...]
    lane = lax.broadcasted_iota(jnp.int32, cos.shape, 1)
    for c in range(ATT_HEADS // 2):
        t = q[:, c * LANES:(c + 1) * LANES].astype(F32)
        r = _head_norm_rope(t, cos, sin, gain, bd)
        tail = jnp.where(lane == HEAD_DIM, negb_ref[...], 0.0)
        o_ref[0, 2 * c] = jnp.where(lane < HEAD_DIM, r, tail).astype(BF16)
        o_ref[0, 2 * c + 1] = jnp.where(lane < HEAD_DIM, pltpu.roll(r, HEAD_DIM, 1), tail).astype(BF16)


def _kvprep_body(kl_ref, vl_ref, kc_ref, vc_ref, cos_ref, sin_ref, g_ref, bd_ref, kt_ref, v1_ref, *, nt_lat):
    is_ctx = pl.program_id(1) == nt_lat
    k = jnp.where(is_ctx, kc_ref[0], kl_ref[0]).astype(F32)
    v = jnp.where(is_ctx, vc_ref[0], vl_ref[0]).astype(F32)
    r = _head_norm_rope(k, cos_ref[...], sin_ref[...], g_ref[...], bd_ref[...])
    lane = lax.broadcasted_iota(jnp.int32, r.shape, 1)
    ones_col = jnp.where(lane == HEAD_DIM, 1.0, 0.0)
    for g in range(ATT_KV_HEADS):
        rg = r if g == 0 else pltpu.roll(r, HEAD_DIM, 1)
        vg = v if g == 0 else pltpu.roll(v, HEAD_DIM, 1)
        kt_ref[0, g] = jnp.where(lane < HEAD_DIM, rg, ones_col).T.astype(BF16)
        v1_ref[0, g] = jnp.where(lane < HEAD_DIM, vg, ones_col).astype(BF16)


def _flash_body(q_ref, kt_ref, v_ref, o_ref, s_ref, acc_ref, m_ref, *, tq, tk, nk):
    rep = ATT_HEADS // ATT_KV_HEADS
    rows = rep * tq
    q = q_ref[0].reshape(rows, LANES)

    def keys(c):
        return kt_ref[0, 0, :, pl.ds(pl.multiple_of(c * tk, tk), tk)]

    def vals(c):
        return v_ref[0, 0, pl.ds(pl.multiple_of(c * tk, tk), tk), :]

    s_ref[0] = jnp.dot(q, keys(0), preferred_element_type=F32)
    acc = jnp.zeros((rows, LANES), F32)
    for c in range(nk):
        if c + 1 < nk:
            s_ref[(c + 1) % 2] = jnp.dot(q, keys(c + 1), preferred_element_type=F32)
        p = jnp.exp2(s_ref[c % 2]).astype(BF16)
        acc = acc + jnp.dot(p, vals(c), preferred_element_type=F32)

    def finish(acc):
        o = acc / acc[:, HEAD_DIM:HEAD_DIM + 1]
        lane = lax.broadcasted_iota(jnp.int32, (tq, LANES), 1)
        for j in range(rep // 2):
            a = o[(2 * j) * tq:(2 * j + 1) * tq]
            bb = o[(2 * j + 1) * tq:(2 * j + 2) * tq]
            o_ref[0, :, j * LANES:(j + 1) * LANES] = jnp.where(
                lane < HEAD_DIM, a, pltpu.roll(bb, HEAD_DIM, 1)).astype(BF16)

    finish(acc)

    @pl.when(jnp.min(acc[:, HEAD_DIM:HEAD_DIM + 1]) < ROW_SUM_MIN)
    def _():
        m_ref[...] = jnp.full(m_ref.shape, -jnp.inf, F32)
        acc_ref[...] = jnp.zeros(acc_ref.shape, F32)

        def step(c, carry):
            s = jnp.dot(q, keys(c), preferred_element_type=F32)
            m_prev = m_ref[...]
            m_new = jnp.maximum(m_prev, jnp.max(s, axis=1, keepdims=True))
            alpha = jnp.exp2(m_prev - m_new)
            p = jnp.exp2(s - jnp.tile(m_new, (1, tk // LANES))).astype(BF16)
            acc_ref[...] = acc_ref[...] * alpha + jnp.dot(p, vals(c), preferred_element_type=F32)
            m_ref[...] = m_new
            return carry

        lax.fori_loop(0, nk, step, 0)
        finish(acc_ref[...])


def _flash(qh, kt, v1, lq, q_off, lk, k_off, tq, tk):
    b, l = qh.shape[0], lq
    rep = ATT_HEADS // ATT_KV_HEADS
    kb = k_off // lk
    qb = q_off // tq
    return pl.pallas_call(
        functools.partial(_flash_body, tq=tq, tk=tk, nk=lk // tk),
        grid=(b, ATT_KV_HEADS, l // tq),
        in_specs=[pl.BlockSpec((1, rep, tq, LANES), lambda bi, g, i: (bi, g, i + qb, 0)),
                  pl.BlockSpec((1, 1, LANES, lk), lambda bi, g, i: (bi, g, 0, kb)),
                  pl.BlockSpec((1, 1, lk, LANES), lambda bi, g, i: (bi, g, kb, 0))],
        out_specs=pl.BlockSpec((1, tq, rep * HEAD_DIM), lambda bi, g, i: (bi, i, g)),
        out_shape=jax.ShapeDtypeStruct((b, l, ATT_HEADS * HEAD_DIM), BF16),
        scratch_shapes=[pltpu.VMEM((2, rep * tq, tk), F32),
                        pltpu.VMEM((rep * tq, LANES), F32), pltpu.VMEM((rep * tq, LANES), F32)],
        compiler_params=_cparams(("parallel", "parallel", "arbitrary")), name="flash_attention",
    )(qh, kt, v1)


def _halo_specs(tm, l, width, col_block):
    r = tm // HALO
    last = l // HALO - 1
    main = pl.BlockSpec((1, tm, width), lambda bi, i: (bi, i, col_block))
    prev = pl.BlockSpec((1, HALO, width), lambda bi, i: (bi, jnp.maximum(i * r - 1, 0), col_block))
    nxt = pl.BlockSpec((1, HALO, width), lambda bi, i: (bi, jnp.minimum((i + 1) * r, last), col_block))
    return [main, prev, nxt]


def _ssdprep_body(xs_ref, xsp_ref, xsn_ref, bc_ref, bcp_ref, bcn_ref, dt_ref,
                  xsc_ref, bcc_ref, dtc_ref, w_ref, b_ref, dtb_ref, ex_ref,
                  xo_ref, bco_ref, dtx_ref, ex_scr, eb_scr, *, nt_lat, tm):
    t = pl.program_id(1)
    is_ctx = t == nt_lat
    has_prev = (t > 0) & (t < nt_lat)
    has_next = t < nt_lat - 1

    def conv_silu(main_ref, prev_ref, next_ref, ctx_ref, scr, lo, hi):
        main = jnp.where(is_ctx, ctx_ref[0], main_ref[0])
        prev = jnp.where(has_prev, prev_ref[0], jnp.zeros_like(prev_ref[0]))
        nxt = jnp.where(has_next, next_ref[0], jnp.zeros_like(next_ref[0]))
        scr[...] = jnp.concatenate([prev, main, nxt], axis=0).astype(F32)
        w = w_ref[:, lo:hi]
        y = (w[0:1] * scr[pl.ds(HALO - 1, tm), :] + w[1:2] * scr[pl.ds(HALO, tm), :]
             + w[2:3] * scr[pl.ds(HALO + 1, tm), :]) + b_ref[:, lo:hi]
        return _silu(y)

    xo_ref[0] = conv_silu(xs_ref, xsp_ref, xsn_ref, xsc_ref, ex_scr, 0, 512).astype(BF16)
    bco_ref[0] = conv_silu(bc_ref, bcp_ref, bcn_ref, bcc_ref, eb_scr, 512, 768).astype(BF16)
    raw = jnp.where(is_ctx, dtc_ref[0], dt_ref[0]).astype(F32) + dtb_ref[...]
    dt = jnp.maximum(raw, 0.0) + jnp.log1p(jnp.exp(-jnp.abs(raw)))
    for d in range(2):
        dtx_ref[0, d] = _split_dot(dt, ex_ref[d])


def _seqprep_body(*refs, nt_lat, tm):
    n_kv_in, n_ssd_in = 8, 14
    ql_ref, qc_ref, gq_ref, negb_ref = refs[:4]
    kv_in, ssd_in = refs[4:4 + n_kv_in], refs[4 + n_kv_in:4 + n_kv_in + n_ssd_in]
    qo_ref, kt_ref, v1_ref, xo_ref, bco_ref, dtx_ref, ex_scr, eb_scr = refs[4 + n_kv_in + n_ssd_in:]
    cos_ref, sin_ref, bd_ref = kv_in[4], kv_in[5], kv_in[7]
    _qprep_body(ql_ref, qc_ref, cos_ref, sin_ref, gq_ref, negb_ref, bd_ref, qo_ref, nt_lat=nt_lat)
    _kvprep_body(*kv_in, kt_ref, v1_ref, nt_lat=nt_lat)
    _ssdprep_body(*ssd_in, xo_ref, bco_ref, dtx_ref, ex_scr, eb_scr, nt_lat=nt_lat, tm=tm)


def _seqprep(proj, proj_c, cos, sin, gain_q, neg_bound, gain, bd, conv_w, conv_b, dt_bias_pad, expand):
    b, l, _ = proj.shape
    lc = proj_c.shape[1]
    tm = lc
    nt_lat = l // tm
    ltot = l + lc
    r = tm // HALO
    last = l // HALO - 1
    cl = lambda t: jnp.minimum(t, nt_lat - 1)
    const2 = lambda bi, t: (0, 0)

    def lat_specs(width, col):
        return [pl.BlockSpec((1, tm, width), lambda bi, t: (bi, cl(t), col)),
                pl.BlockSpec((1, HALO, width), lambda bi, t: (bi, jnp.maximum(cl(t) * r - 1, 0), col)),
                pl.BlockSpec((1, HALO, width), lambda bi, t: (bi, jnp.minimum((cl(t) + 1) * r, last), col))]

    q_specs = [pl.BlockSpec((1, tm, 512), lambda bi, t: (bi, cl(t), C_Q // 512)),
               pl.BlockSpec((1, tm, 512), lambda bi, t: (bi, 0, C_Q // 512)),
               pl.BlockSpec((1, LANES), const2),
               pl.BlockSpec((1, LANES), const2)]
    kv_specs = [pl.BlockSpec((1, tm, LANES), lambda bi, t: (bi, cl(t), C_K // LANES)),
                pl.BlockSpec((1, tm, LANES), lambda bi, t: (bi, cl(t), C_V // LANES)),
                pl.BlockSpec((1, tm, LANES), lambda bi, t: (bi, 0, C_K // LANES)),
                pl.BlockSpec((1, tm, LANES), lambda bi, t: (bi, 0, C_V // LANES)),
                pl.BlockSpec((tm, LANES), lambda bi, t: (t, 0)),
                pl.BlockSpec((tm, LANES), lambda bi, t: (t, 0)),
                pl.BlockSpec((1, LANES), const2),
                pl.BlockSpec((LANES, LANES), const2)]
    ssd_specs = (lat_specs(512, C_XS // 512) + lat_specs(256, C_BC // 256)
                 + [pl.BlockSpec((1, tm, DT_PAD), lambda bi, t: (bi, cl(t), C_DT // DT_PAD)),
                    pl.BlockSpec((1, tm, 512), lambda bi, t: (bi, 0, C_XS // 512)),
                    pl.BlockSpec((1, tm, 256), lambda bi, t: (bi, 0, C_BC // 256)),
                    pl.BlockSpec((1, tm, DT_PAD), lambda bi, t: (bi, 0, C_DT // DT_PAD)),
                    pl.BlockSpec((3, 768), const2),
                    pl.BlockSpec((1, 768), const2),
                    pl.BlockSpec((1, DT_PAD), const2),
                    pl.BlockSpec((2, DT_PAD, 512), lambda bi, t: (0, 0, 0))])
    return pl.pallas_call(
        functools.partial(_seqprep_body, nt_lat=nt_lat, tm=tm), grid=(b, nt_lat + 1),
        in_specs=q_specs + kv_specs + ssd_specs,
        out_specs=[pl.BlockSpec((1, ATT_HEADS, tm, LANES), lambda bi, t: (bi, 0, t, 0)),
                   pl.BlockSpec((1, ATT_KV_HEADS, LANES, tm), lambda bi, t: (bi, 0, 0, t)),
                   pl.BlockSpec((1, ATT_KV_HEADS, tm, LANES), lambda bi, t: (bi, 0, t, 0)),
                   pl.BlockSpec((1, tm, 512), lambda bi, t: (bi, t, 0)),
                   pl.BlockSpec((1, tm, 256), lambda bi, t: (bi, t, 0)),
                   pl.BlockSpec((1, 2, tm, 512), lambda bi, t: (bi, 0, t, 0))],
        out_shape=[jax.ShapeDtypeStruct((b, ATT_HEADS, ltot, LANES), BF16),
                   jax.ShapeDtypeStruct((b, ATT_KV_HEADS, LANES, ltot), BF16),
                   jax.ShapeDtypeStruct((b, ATT_KV_HEADS, ltot, LANES), BF16),
                   jax.ShapeDtypeStruct((b, ltot, 512), BF16),
                   jax.ShapeDtypeStruct((b, ltot, 256), BF16),
                   jax.ShapeDtypeStruct((b, 2, ltot, 512), F32)],
        scratch_shapes=[pltpu.VMEM((tm + 2 * HALO, 512), F32), pltpu.VMEM((tm + 2 * HALO, 256), F32)],
        compiler_params=_cparams(("parallel", "parallel")), name="seq_prep",
    )(proj, proj_c, gain_q, neg_bound, proj, proj, proj_c, proj_c, cos, sin, gain, bd,
      proj, proj, proj, proj, proj, proj, proj, proj_c, proj_c, proj_c, conv_w, conv_b, dt_bias_pad, expand)


SSD_STEP_CHUNKS = 2


def _ssd_body(xsf_ref, bcf_ref, dtf_ref, xsb_ref, bcb_ref, dtb_ref, ar_ref, tri_ref, neg_ref, hm_ref,
              yf_ref, yb_ref, htf_ref, htb_ref):
    t = SSD_CHUNK
    nb = xsf_ref.shape[0]

    @pl.when(pl.program_id(0) == 0)
    def _():
        htf_ref[...] = jnp.zeros(htf_ref.shape, F32)
        htb_ref[...] = jnp.zeros(htb_ref.shape, F32)

    lane = lax.broadcasted_iota(jnp.int32, (t, LANES), 1)
    low = lane < HEAD_DIM

    def within_chunk(bi, d, xs_ref, bc_ref, dtx_ref, rows):
        xs = xs_ref[bi, rows, :].astype(F32)
        dtx = dtx_ref[bi, 0, rows, :]
        cum = _dot_split_lhs(tri_ref[d], dtx * ar_ref[d])
        tot = cum[t - 1:t, :] if d == 0 else cum[0:1, :]
        neg = neg_ref[d]
        xdt = xs * dtx
        xdt_b = xdt.astype(BF16)
        bc = bc_ref[bi, rows, :]
        b_all, c_all = bc[:, :LANES], bc[:, LANES:]
        cb = []
        for g in range(2):
            cg = jnp.where(low if g == 0 else ~low, c_all, jnp.zeros_like(c_all))
            cb.append(lax.dot_general(cg, b_all, (((1,), (1,)), ((), ())), preferred_element_type=F32))
        ys = []
        for j in range(SSD_HEADS // 2):
            cp = cum[:, j * LANES:(j + 1) * LANES]
            cps = pltpu.roll(cp, HEAD_DIM, 1)
            cpt = cp.T
            xp = xdt_b[:, j * LANES:(j + 1) * LANES]
            acc = None
            for e in range(2):
                col = jnp.where(low, cp, cps) if e == 0 else jnp.where(low, cps, cp)
                row = jnp.broadcast_to(cpt[e * HEAD_DIM:e * HEAD_DIM + 1, :], (t, t))
                w = (cb[j // 2] * jnp.exp2(col - row + neg)).astype(BF16)
                xm = jnp.where(low if e == 0 else ~low, xp, jnp.zeros_like(xp))
                part = jnp.dot(w, xm, preferred_element_type=F32)
                acc = part if acc is None else acc + part
            ys.append(acc)
        bt = b_all.astype(F32).T.astype(BF16)
        st = jnp.dot(bt, (xdt * jnp.exp2(tot - cum)).astype(BF16), preferred_element_type=F32)
        return jnp.concatenate(ys, axis=1), c_all, jnp.exp2(cum), jnp.exp2(tot), st * hm_ref[...]

    dirs = ((xsf_ref, bcf_ref, dtf_ref, yf_ref, htf_ref), (xsb_ref, bcb_ref, dtb_ref, yb_ref, htb_ref))
    local = []
    for bi in range(nb):
        for d, (xs_ref, bc_ref, dtx_ref, y_ref, ht_ref) in enumerate(dirs):
            order = range(SSD_STEP_CHUNKS) if d == 0 else range(SSD_STEP_CHUNKS - 1, -1, -1)
            rows = [pl.ds(o * t, t) for o in order]
            local.append((bi, y_ref, ht_ref, rows, [within_chunk(bi, d, xs_ref, bc_ref, dtx_ref, r) for r in rows]))
    for bi, y_ref, ht_ref, rows, chunks in local:
        ht = ht_ref[bi]
        for r, (y_diag, c_all, ecum, etot, st) in zip(rows, chunks):
            y_off = jnp.dot(c_all, ht.astype(BF16), preferred_element_type=F32) * ecum
            y_ref[bi, r, :] = (y_diag + y_off).astype(BF16)
            ht = ht * etot + st
        ht_ref[bi] = ht


def _ssd_scan(xs, bc, dtx, a_rate_x, tri, neg, hmask, l_lat):
    b, ltot, _ = xs.shape
    t = SSD_CHUNK
    ts = t * SSD_STEP_CHUNKS
    ns = ltot // ts
    nl = l_lat // ts
    nct = ns - nl

    fwd = lambda c: jnp.where(c < nct, nl + c, c - nct)
    bwd = lambda c: jnp.where(c < nct, nl + nct - 1 - c, nl - 1 - (c - nct))
    whole = lambda c: (0, 0, 0)

    def specs(sid, d):
        return [pl.BlockSpec((b, ts, 512), lambda c: (0, sid(c), 0)),
                pl.BlockSpec((b, ts, 256), lambda c: (0, sid(c), 0)),
                pl.BlockSpec((b, 1, ts, 512), lambda c: (0, d, sid(c), 0))]

    return pl.pallas_call(
        _ssd_body, grid=(ns,),
        in_specs=specs(fwd, 0) + specs(bwd, 1) + [
            pl.BlockSpec((2, 1, 512), whole), pl.BlockSpec((2, t, t), whole), pl.BlockSpec((2, t, t), whole),
            pl.BlockSpec((LANES, 512), lambda c: (0, 0))],
        out_specs=[pl.BlockSpec((b, ts, 512), lambda c: (0, fwd(c), 0)),
                   pl.BlockSpec((b, ts, 512), lambda c: (0, bwd(c), 0))],
        out_shape=[jax.ShapeDtypeStruct((b, ltot, 512), BF16), jax.ShapeDtypeStruct((b, ltot, 512), BF16)],
        scratch_shapes=[pltpu.VMEM((b, LANES, 512), F32), pltpu.VMEM((b, LANES, 512), F32)],
        compiler_params=_cparams(("arbitrary",)), name="ssd_scan",
    )(xs, bc, dtx, xs, bc, dtx, a_rate_x, tri, neg, hmask)


def _merge_body(gl_ref, bg_ref, cg_ref, cgp_ref, cgn_ref, u_ref, up_ref, un_ref,
                pu_ref, pup_ref, pun_ref, z_ref, att_ref, yf_ref, yb_ref, xs_ref, x_ref, mod_ref,
                cw_ref, band_ref, pw_ref, ps_ref, dsk_ref, sn_ref, wb_ref, wo_ref, o_ref, v_ref, *, tm, l):
    i = pl.program_id(1)
    has_prev = i > 0
    has_next = i < l // tm - 1

    def ext(main_ref, prev_ref, next_ref):
        prev = jnp.where(has_prev, prev_ref[0], jnp.zeros_like(prev_ref[0]))
        nxt = jnp.where(has_next, next_ref[0], jnp.zeros_like(next_ref[0]))
        return jnp.concatenate([prev, main_ref[0], nxt], axis=0)

    v_ref[...] = ext(cg_ref, cgp_ref, cgn_ref).astype(F32) * ext(u_ref, up_ref, un_ref).astype(F32)
    cw = cw_ref[...]
    conv = (cw[0:1] * v_ref[pl.ds(HALO - 1, tm), :] + cw[1:2] * v_ref[pl.ds(HALO, tm), :]
            + cw[2:3] * v_ref[pl.ds(HALO + 1, tm), :])
    br_conv = bg_ref[0].astype(F32) * conv

    pe = ext(pu_ref, pup_ref, pun_ref)
    posm = i * tm + lax.broadcasted_iota(jnp.int32, (tm, LANES), 0)
    pooled = []
    for g, win in enumerate(POOL_WINDOWS):
        cols = slice(g * LANES, (g + 1) * LANES)
        sums = jnp.concatenate(
            [jnp.dot(band_ref[g], pe[k * POOL_BLOCK:k * POOL_BLOCK + POOL_BLOCK + 2 * HALO, cols],
                     preferred_element_type=F32) for k in range(tm // POOL_BLOCK)], axis=0)
        cnt = (jnp.minimum(posm, win // 2) + jnp.minimum(l - posm, win // 2)).astype(F32)
        pm = (sums / cnt - pu_ref[0, :, cols].astype(F32)).astype(BF16)
        pooled.append(jnp.dot(pm, pw_ref[g], preferred_element_type=F32))
    br_pool = jnp.concatenate(pooled, axis=1) * ps_ref[...]

    y = yf_ref[0].astype(F32) + yb_ref[0].astype(F32) + dsk_ref[...] * xs_ref[0].astype(F32)
    gz = y * _silu(z_ref[0].astype(F32))
    br_ssd = gz * lax.rsqrt(jnp.mean(gz * gz, axis=-1, keepdims=True) + NORM_EPS) * sn_ref[...]

    branches = (att_ref[0], br_conv.astype(BF16), br_pool.astype(BF16), br_ssd.astype(BF16))
    acc = None
    for k, br in enumerate(branches):
        gate2 = jnp.tanh(gl_ref[0, :, k * D_MODEL:(k + 1) * D_MODEL].astype(F32)) + 1.0
        term = gate2 * jnp.dot(br, wb_ref[k], preferred_element_type=F32)
        acc = term if acc is None else acc + term
    yo = jnp.dot((0.5 * acc).astype(BF16), wo_ref[...], preferred_element_type=F32)
    o_ref[0] = x_ref[0] + mod_ref[0, 2:3, :] * yo


def _merge(proj, att, yf, yb, xs_c, x, mod, conv_short, band, pool_w, pool_scale, dskip_x, ssd_norm,
           w_branch, w_out, layer, tm, ssd_row_off):
    b, l, _ = x.shape
    per_batch = mod.shape[0] > 1
    off = ssd_row_off // tm
    const2 = lambda bi, i: (0, 0)
    const3 = lambda bi, i: (0, 0, 0)
    in_specs = (
        [pl.BlockSpec((1, tm, 4096), lambda bi, i: (bi, i, 0)),
         pl.BlockSpec((1, tm, 512), lambda bi, i: (bi, i, C_BG // 512))]
        + _halo_specs(tm, l, 512, C_CG // 512) + _halo_specs(tm, l, 512, C_U // 512)
        + _halo_specs(tm, l, 512, C_PU // 512)
        + [pl.BlockSpec((1, tm, 512), lambda bi, i: (bi, i, C_Z // 512)),
           pl.BlockSpec((1, tm, 512), lambda bi, i: (bi, i, 0)),
           pl.BlockSpec((1, tm, 512), lambda bi, i: (bi, i + off, 0)),
           pl.BlockSpec((1, tm, 512), lambda bi, i: (bi, i + off, 0)),
           pl.BlockSpec((1, tm, 512), lambda bi, i: (bi, i + off, 0)),
           pl.BlockSpec((1, tm, D_MODEL), lambda bi, i: (bi, i, 0)),
           pl.BlockSpec((1, 6, D_MODEL), lambda bi, i: (bi if per_batch else 0, 0, 0)),
           pl.BlockSpec((3, 512), const2),
           pl.BlockSpec((4, POOL_BLOCK, POOL_BLOCK + 2 * HALO), const3),
           pl.BlockSpec((None, 4, LANES, LANES), lambda bi, i: (layer, 0, 0, 0)),
           pl.BlockSpec((1, 512), const2),
           pl.BlockSpec((1, 512), const2),
           pl.BlockSpec((1, 512), const2),
           pl.BlockSpec((None, N_BRANCH, BRANCH_WIDTH, D_MODEL), lambda bi, i: (layer, 0, 0, 0)),
           pl.BlockSpec((None, D_MODEL, D_MODEL), lambda bi, i: (layer, 0, 0))])
    return pl.pallas_call(
        functools.partial(_merge_body, tm=tm, l=l), grid=(b, l // tm),
        in_specs=in_specs,
        out_specs=pl.BlockSpec((1, tm, D_MODEL), lambda bi, i: (bi, i, 0)),
        out_shape=jax.ShapeDtypeStruct((b, l, D_MODEL), F32),
        scratch_shapes=[pltpu.VMEM((tm + 2 * HALO, 512), F32)],
        compiler_params=_cparams(("parallel", "parallel")), name="merge",
    )(proj, proj, proj, proj, proj, proj, proj, proj, proj, proj, proj, proj,
      att, yf, yb, xs_c, x, mod, conv_short, band, pool_w, pool_scale, dskip_x, ssd_norm, w_branch, w_out)


FFN_HALO = 8
FFN_CHUNK = 256


def _ffn_body(x_ref, xp_ref, xn_ref, mod_ref, g_ref, wu_ref, cw_ref, wd_ref, fn_ref,
              o_ref, h_ref, ug_ref, uv_ref, a_ref, *, tm, l, final):
    i = pl.program_id(1)
    n_ext = tm + 2 * FFN_HALO
    pos = i * tm - FFN_HALO + lax.broadcasted_iota(jnp.int32, (n_ext, D_MODEL), 0)
    xe = jnp.concatenate([xp_ref[0], x_ref[0], xn_ref[0]], axis=0)
    h = _rms_mod(xe, g_ref[...], mod_ref[0, 4:5, :], mod_ref[0, 3:4, :])
    h_ref[...] = jnp.where((pos >= 0) & (pos < l), h, 0.0).astype(BF16)

    def conv3(u_ref, s, w):
        return (w[0:1] * u_ref[s, pl.ds(FFN_HALO - 1, tm), :] + w[1:2] * u_ref[s, pl.ds(FFN_HALO, tm), :]
                + w[2:3] * u_ref[s, pl.ds(FFN_HALO + 1, tm), :])

    for j in range(D_FF // FFN_CHUNK):
        s = j % 2
        cg = slice(j * FFN_CHUNK, (j + 1) * FFN_CHUNK)
        cv = slice(D_FF + j * FFN_CHUNK, D_FF + (j + 1) * FFN_CHUNK)
        ug_ref[s] = jnp.dot(h_ref[...], wu_ref[:, cg], preferred_element_type=F32)
        uv_ref[s] = jnp.dot(h_ref[...], wu_ref[:, cv], preferred_element_type=F32)
        hg = conv3(ug_ref, s, cw_ref[:, cg])
        a_ref[:, cg] = (hg * (jnp.tanh(hg) + 1.0) * conv3(uv_ref, s, cw_ref[:, cv])).astype(BF16)

    out = x_ref[0] + mod_ref[0, 5:6, :] * jnp.dot(a_ref[...], wd_ref[...], preferred_element_type=F32)
    if final:
        out = out * lax.rsqrt(jnp.mean(out * out, axis=-1, keepdims=True) + NORM_EPS) * fn_ref[...]
    o_ref[0] = out


def _ffn(x, mod, g, w_up, ffn_conv, w_down, final_norm, layer, tm, final):
    b, l, _ = x.shape
    per_batch = mod.shape[0] > 1
    r = tm // FFN_HALO
    last = l // FFN_HALO - 1
    n_ext = tm + 2 * FFN_HALO
    const = lambda bi, i: (0, 0)
    resident = pl.Buffered(1)
    return pl.pallas_call(
        functools.partial(_ffn_body, tm=tm, l=l, final=final), grid=(b, l // tm),
        in_specs=[pl.BlockSpec((1, tm, D_MODEL), lambda bi, i: (bi, i, 0)),
                  pl.BlockSpec((1, FFN_HALO, D_MODEL), lambda bi, i: (bi, jnp.maximum(i * r - 1, 0), 0)),
                  pl.BlockSpec((1, FFN_HALO, D_MODEL), lambda bi, i: (bi, jnp.minimum((i + 1) * r, last), 0)),
                  pl.BlockSpec((1, 6, D_MODEL), lambda bi, i: (bi if per_batch else 0, 0, 0)),
                  pl.BlockSpec((1, D_MODEL), const),
                  pl.BlockSpec((None, D_MODEL, 2 * D_FF), lambda bi, i: (layer, 0, 0), pipeline_mode=resident),
                  pl.BlockSpec((3, 2 * D_FF), const, pipeline_mode=resident),
                  pl.BlockSpec((None, D_FF, D_MODEL), lambda bi, i: (layer, 0, 0), pipeline_mode=resident),
                  pl.BlockSpec((1, D_MODEL), const)],
        out_specs=pl.BlockSpec((1, tm, D_MODEL), lambda bi, i: (bi, i, 0)),
        out_shape=jax.ShapeDtypeStruct((b, l, D_MODEL), F32),
        scratch_shapes=[pltpu.VMEM((n_ext, D_MODEL), BF16), pltpu.VMEM((2, n_ext, FFN_CHUNK), F32),
                        pltpu.VMEM((2, n_ext, FFN_CHUNK), F32), pltpu.VMEM((tm, D_FF), BF16)],
        compiler_params=_cparams(("parallel", "parallel")), name="conv_ffn",
    )(x, x, x, mod, g.reshape(1, D_MODEL), w_up, ffn_conv, w_down, final_norm.reshape(1, D_MODEL))


def _rope_tables(n_tokens):
    rows = n_tokens // GRID_W
    row = jnp.repeat(jnp.arange(rows), GRID_W).astype(F32)
    col = jnp.tile(jnp.arange(GRID_W), rows).astype(F32)
    inv = ROPE_THETA ** (-jnp.arange(ROPE_FREQS, dtype=F32) / ROPE_FREQS)
    ang_r, ang_c = row[:, None] * inv, col[:, None] * inv
    ang = jnp.concatenate([ang_r, ang_r, ang_c, ang_c], axis=1)
    sign = jnp.tile(jnp.concatenate([-jnp.ones(ROPE_FREQS), jnp.ones(ROPE_FREQS)]), 2).astype(F32)
    cos = jnp.tile(jnp.cos(ang), (1, 2))
    sin = jnp.tile(jnp.sin(ang) * sign, (1, 2))
    return cos, sin


def _identity_tables(n):
    return jnp.ones((n, LANES), F32), jnp.zeros((n, LANES), F32)


def _cat_weights(w_in):
    o = np.cumsum([0, 512, 128, 128, 512, 512, 512, 512, 512, 768, 16, 4096])
    wt = jnp.swapaxes(w_in, 1, 2)
    q, k, v, bg, cg, u, pu, z, xbc, dt, gl = [wt[:, o[i]:o[i + 1], :] for i in range(11)]
    dt = jnp.pad(dt, ((0, 0), (0, DT_PAD - dt.shape[1]), (0, 0)))
    gl = 0.5 * gl
    return jnp.concatenate([gl, bg, cg, u, pu, z, q, xbc[:, :512], xbc[:, 512:], k, v, dt],
                           axis=1).astype(BF16)


def kernel(x, c, ctx, c_ctx, w_mod, b_mod, norm_mix, w_in, q_norm, k_norm, conv_short, pool_w, pool_scale,
           ssd_conv_w, ssd_conv_b, ssd_dt_bias, ssd_a_log, ssd_d, ssd_norm, w_branch, w_out, norm_ffn,
           w_up, ffn_conv, w_down, final_norm):
    depth = w_mod.shape[0]
    b, l, _ = x.shape
    lc = ctx.shape[1]
    ltot = l + lc
    t = SSD_CHUNK

    tm_in = min(1024, l)
    tm_mid = min(512, l)
    tq = 512
    tk_lat = 768 if ltot % 768 == 0 else lc
    tk_ctx = lc

    cos_l, sin_l = _rope_tables(l)
    cos_c, sin_c = _identity_tables(lc)
    cos_all = jnp.concatenate([cos_l, cos_c], axis=0)
    sin_all = jnp.concatenate([sin_l, sin_c], axis=0)
    lane = np.arange(LANES)
    bd = jnp.asarray((lane[:, None] // HEAD_DIM) == (lane[None, :] // HEAD_DIM), BF16)
    col512 = np.arange(512)
    expand = np.zeros((2, DT_PAD, 512), np.float32)
    for d in range(2):
        expand[d, d * SSD_HEADS + col512 // HEAD_DIM, col512] = 1.0
    expand = jnp.asarray(expand, BF16)
    pos = np.arange(t)
    tri = jnp.asarray(np.stack([pos[:, None] >= pos[None, :], pos[:, None] <= pos[None, :]]), BF16)
    neg = jnp.asarray(np.where(np.stack([pos[:, None] >= pos[None, :], pos[:, None] <= pos[None, :]]),
                               0.0, -np.inf), F32)
    brow, bcol = np.arange(POOL_BLOCK)[:, None], np.arange(POOL_BLOCK + 2 * HALO)[None, :]
    band = jnp.asarray(np.stack([(bcol >= HALO + brow - w // 2) & (bcol < HALO + brow + w // 2)
                                 for w in POOL_WINDOWS]), BF16)
    hmask = jnp.asarray((lane[:, None] // SSD_STATE) == (col512[None, :] // 256), F32)

    cond8 = jnp.zeros((8, D_MODEL), F32).at[:b].set(c).at[b].set(c_ctx)
    mod_all = _modulation(cond8, w_mod, b_mod).reshape(depth, 8, 6, D_MODEL)
    w_cat = _cat_weights(w_in)
    wb, wo, pw = w_branch.astype(BF16), w_out.astype(BF16), pool_w.astype(BF16)
    gate_half = jnp.concatenate([jnp.full((D_FF,), 0.5, F32), jnp.ones((D_FF,), F32)])
    wu, wd = (w_up * gate_half).astype(BF16), w_down.astype(BF16)
    xc = ctx
    for i in range(depth):
        last = i == depth - 1
        mod_l, mod_c = mod_all[i, :b], mod_all[i, b:b + 1]
        gq = (jnp.tile(q_norm[i], 2) * QK_SCALE).reshape(1, LANES)
        gk = jnp.tile(k_norm[i], 2).reshape(1, LANES)
        neg_bound = jnp.full((1, LANES), -HEAD_DIM * QK_SCALE, F32) * (jnp.max(jnp.abs(q_norm[i]))
                                                                  * jnp.max(jnp.abs(k_norm[i])))
        a_rate_x = jnp.repeat(-jnp.exp(ssd_a_log[i].astype(F32)) * math.log2(math.e),
                              HEAD_DIM, axis=1).reshape(2, 1, 512)
        dskip_x = jnp.repeat(ssd_d[i], HEAD_DIM).reshape(1, 512)
        dt_bias_pad = jnp.pad(ssd_dt_bias[i].reshape(1, 2 * SSD_HEADS), ((0, 0), (0, DT_PAD - 2 * SSD_HEADS)))
        ps = pool_scale[i].reshape(1, 512)
        sn = ssd_norm[i].reshape(1, 512)
        scb = ssd_conv_b[i].reshape(1, 768)

        proj = _inproj(x, mod_l, norm_mix[i], w_cat, i, tm_in)
        proj_c = _inproj(xc, mod_c, norm_mix[i], w_cat, i, lc)

        qh, kt, v1, xs_s, bc_s, dtx = _seqprep(proj, proj_c, cos_all, sin_all, gq, neg_bound, gk, bd,
                                               ssd_conv_w[i], scb, dt_bias_pad, expand)
        att = _flash(qh, kt, v1, l, 0, ltot, 0, tq, tk_lat)

        yf, yb = _ssd_scan(xs_s, bc_s, dtx, a_rate_x, tri, neg, hmask, l)

        x_mid = _merge(proj, att, yf, yb, xs_s, x, mod_l, conv_short[i], band, pw, ps, dskip_x, sn, wb, wo, i, tm_mid, 0)
        if not last:
            att_c = _flash(qh, kt, v1, lc, l, lc, l, min(tq, lc), tk_ctx)
            xc_mid = _merge(proj_c, att_c, yf, yb, xs_s, xc, mod_c, conv_short[i], band, pw, ps, dskip_x, sn, wb, wo, i, lc, l)
            xc = _ffn(xc_mid, mod_c, norm_ffn[i], wu, ffn_conv[i], wd, final_norm, i, lc, False)
        x = _ffn(x_mid, mod_l, norm_ffn[i], wu, ffn_conv[i], wd, final_norm, i, tm_mid, last)
    return x
```

```python
import functools
import math

import jax
import jax.numpy as jnp
import numpy as np
from jax import lax
from jax.experimental import pallas as pl
from jax.experimental.pallas import tpu as pltpu

F32 = jnp.float32
BF16 = jnp.bfloat16

D_MODEL = 1024
GRID_W = 64
NORM_EPS = 1e-6
ATT_HEADS = 8
ATT_KV_HEADS = 2
HEAD_DIM = 64
ROPE_THETA = 10000.0
ROPE_FREQS = HEAD_DIM // 4
POOL_WINDOWS = (2, 4, 8, 16)
SSD_HEADS = 8
SSD_STATE = 64
SSD_CHUNK = 128
N_BRANCH = 4
BRANCH_WIDTH = 512
D_FF = 2816

C_GL, C_BG, C_CG, C_U, C_PU, C_Z, C_Q, C_XS, C_BC, C_K, C_V, C_DT = (
    0, 4096, 4608, 5120, 5632, 6144, 6656, 7168, 7680, 7936, 8064, 8192)
N_CAT = 8448
DT_PAD = 256

HALO = 16
POOL_BLOCK = 128
LANES = 128
QK_SCALE = HEAD_DIM ** -0.5 * math.log2(math.e)
ROW_SUM_MIN = 2.0 ** -47
VMEM_LIMIT = 56 * 1024 * 1024


def _cparams(sem):
    return pltpu.CompilerParams(dimension_semantics=sem, vmem_limit_bytes=VMEM_LIMIT)


def _silu(x):
    hx = 0.5 * x
    return hx * (jnp.tanh(hx) + 1.0)


def _split_dot(a, b_bf16):
    hi = a.astype(BF16)
    lo = (a - hi.astype(F32)).astype(BF16)
    return (jnp.dot(hi, b_bf16, preferred_element_type=F32)
            + jnp.dot(lo, b_bf16, preferred_element_type=F32))


def _dot_split_lhs(a_bf16, b):
    hi = b.astype(BF16)
    lo = (b - hi.astype(F32)).astype(BF16)
    return (jnp.dot(a_bf16, hi, preferred_element_type=F32)
            + jnp.dot(a_bf16, lo, preferred_element_type=F32))


def _rms_mod(x, g, scale, shift):
    ms = jnp.mean(x * x, axis=-1, keepdims=True)
    return x * lax.rsqrt(ms + NORM_EPS) * (g * (1.0 + scale)) + shift


def _mod_body(c_ref, w_ref, b_ref, o_ref):
    s = _silu(c_ref[...])
    w = w_ref[...]
    s_hi, w_hi = s.astype(BF16), w.astype(BF16)
    s_lo = (s - s_hi.astype(F32)).astype(BF16)
    w_lo = (w - w_hi.astype(F32)).astype(BF16)
    o_ref[...] = (jnp.dot(s_hi, w_hi, preferred_element_type=F32)
                  + jnp.dot(s_lo, w_hi, preferred_element_type=F32)
                  + jnp.dot(s_hi, w_lo, preferred_element_type=F32)) + b_ref[...]


def _modulation(cond8, w_mod, b_mod):
    depth, _, n = w_mod.shape
    tn = 1536
    return pl.pallas_call(
        _mod_body, grid=(depth, n // tn),
        in_specs=[pl.BlockSpec((8, D_MODEL), lambda d, j: (0, 0)),
                  pl.BlockSpec((None, D_MODEL, tn), lambda d, j: (d, 0, j)),
                  pl.BlockSpec((None, 1, tn), lambda d, j: (d, 0, j))],
        out_specs=pl.BlockSpec((None, 8, tn), lambda d, j: (d, 0, j)),
        out_shape=jax.ShapeDtypeStruct((depth, 8, n), F32),
        compiler_params=_cparams(("arbitrary", "arbitrary")), name="modulation",
    )(cond8, w_mod, b_mod.reshape(depth, 1, n))


def _inproj_body(x_ref, mod_ref, g_ref, w_ref, o_ref, h_ref):
    @pl.when(pl.program_id(2) == 0)
    def _():
        h = _rms_mod(x_ref[0], g_ref[...], mod_ref[0, 1:2, :], mod_ref[0, 0:1, :])
        h_ref[...] = h.astype(BF16)

    o_ref[0] = lax.dot_general(h_ref[...], w_ref[...], (((1,), (1,)), ((), ())),
                               preferred_element_type=F32).astype(BF16)


def _inproj(x, mod, g, w_cat, layer, tm):
    b, l, _ = x.shape
    n = w_cat.shape[1]
    tn = 2816
    per_batch = mod.shape[0] > 1
    return pl.pallas_call(
        _inproj_body, grid=(b, l // tm, n // tn),
        in_specs=[pl.BlockSpec((1, tm, D_MODEL), lambda bi, i, j: (bi, i, 0)),
                  pl.BlockSpec((1, 6, D_MODEL), lambda bi, i, j: (bi if per_batch else 0, 0, 0)),
                  pl.BlockSpec((1, D_MODEL), lambda bi, i, j: (0, 0)),
                  pl.BlockSpec((None, tn, D_MODEL), lambda bi, i, j: (layer, j, 0))],
        out_specs=pl.BlockSpec((1, tm, tn), lambda bi, i, j: (bi, i, j)),
        out_shape=jax.ShapeDtypeStruct((b, l, n), BF16),
        scratch_shapes=[pltpu.VMEM((tm, D_MODEL), BF16)],
        compiler_params=_cparams(("parallel", "parallel", "arbitrary")), name="inproj",
    )(x, mod, g.reshape(1, D_MODEL), w_cat)


def _head_norm_rope(t, cos, sin, gain, bd):
    ss = _split_dot(t * t, bd)
    y = t * lax.rsqrt(ss * (1.0 / HEAD_DIM) + NORM_EPS) * gain
    lane = lax.broadcasted_iota(jnp.int32, y.shape, 1)
    partner = jnp.where((lane % 32) < 16, pltpu.roll(y, LANES - 16, 1), pltpu.roll(y, 16, 1))
    return y * cos + partner * sin


def _qprep_body(ql_ref, qc_ref, cos_ref, sin_ref, g_ref, negb_ref, bd_ref, o_ref, *, nt_lat):
    q = jnp.where(pl.program_id(1) == nt_lat, qc_ref[0], ql_ref[0])
    cos, sin, gain, bd = cos_ref[...], sin_ref[...], g_ref[...], bd_ref[...]
    lane = lax.broadcasted_iota(jnp.int32, cos.shape, 1)
    for c in range(ATT_HEADS // 2):
        t = q[:, c * LANES:(c + 1) * LANES].astype(F32)
        r = _head_norm_rope(t, cos, sin, gain, bd)
        tail = jnp.where(lane == HEAD_DIM, negb_ref[...], 0.0)
        o_ref[0, 2 * c] = jnp.where(lane < HEAD_DIM, r, tail).astype(BF16)
        o_ref[0, 2 * c + 1] = jnp.where(lane < HEAD_DIM, pltpu.roll(r, HEAD_DIM, 1), tail).astype(BF16)


def _kvprep_body(kl_ref, vl_ref, kc_ref, vc_ref, cos_ref, sin_ref, g_ref, bd_ref, kt_ref, v1_ref, *, nt_lat):
    is_ctx = pl.program_id(1) == nt_lat
    k = jnp.where(is_ctx, kc_ref[0], kl_ref[0]).astype(F32)
    v = jnp.where(is_ctx, vc_ref[0], vl_ref[0]).astype(F32)
    r = _head_norm_rope(k, cos_ref[...], sin_ref[...], g_ref[...], bd_ref[...])
    lane = lax.broadcasted_iota(jnp.int32, r.shape, 1)
    ones_col = jnp.where(lane == HEAD_DIM, 1.0, 0.0)
    for g in range(ATT_KV_HEADS):
        rg = r if g == 0 else pltpu.roll(r, HEAD_DIM, 1)
        vg = v if g == 0 else pltpu.roll(v, HEAD_DIM, 1)
        kt_ref[0, g] = jnp.where(lane < HEAD_DIM, rg, ones_col).T.astype(BF16)
        v1_ref[0, g] = jnp.where(lane < HEAD_DIM, vg, ones_col).astype(BF16)


def _flash_body(q_ref, kt_ref, v_ref, o_ref, s_ref, acc_ref, m_ref, *, tq, tk, nk):
    rep = ATT_HEADS // ATT_KV_HEADS
    rows = rep * tq
    q = q_ref[0].reshape(rows, LANES)

    def keys(c):
        return kt_ref[0, 0, :, pl.ds(pl.multiple_of(c * tk, tk), tk)]

    def vals(c):
        return v_ref[0, 0, pl.ds(pl.multiple_of(c * tk, tk), tk), :]

    s_ref[0] = jnp.dot(q, keys(0), preferred_element_type=F32)
    acc = jnp.zeros((rows, LANES), F32)
    for c in range(nk):
        if c + 1 < nk:
            s_ref[(c + 1) % 2] = jnp.dot(q, keys(c + 1), preferred_element_type=F32)
        p = jnp.exp2(s_ref[c % 2]).astype(BF16)
        acc = acc + jnp.dot(p, vals(c), preferred_element_type=F32)

    def finish(acc):
        o = acc / acc[:, HEAD_DIM:HEAD_DIM + 1]
        lane = lax.broadcasted_iota(jnp.int32, (tq, LANES), 1)
        for j in range(rep // 2):
            a = o[(2 * j) * tq:(2 * j + 1) * tq]
            bb = o[(2 * j + 1) * tq:(2 * j + 2) * tq]
            o_ref[0, :, j * LANES:(j + 1) * LANES] = jnp.where(
                lane < HEAD_DIM, a, pltpu.roll(bb, HEAD_DIM, 1)).astype(BF16)

    finish(acc)

    @pl.when(jnp.min(acc[:, HEAD_DIM:HEAD_DIM + 1]) < ROW_SUM_MIN)
    def _():
        m_ref[...] = jnp.full(m_ref.shape, -jnp.inf, F32)
        acc_ref[...] = jnp.zeros(acc_ref.shape, F32)

        def step(c, carry):
            s = jnp.dot(q, keys(c), preferred_element_type=F32)
            m_prev = m_ref[...]
            m_new = jnp.maximum(m_prev, jnp.max(s, axis=1, keepdims=True))
            alpha = jnp.exp2(m_prev - m_new)
            p = jnp.exp2(s - jnp.tile(m_new, (1, tk // LANES))).astype(BF16)
            acc_ref[...] = acc_ref[...] * alpha + jnp.dot(p, vals(c), preferred_element_type=F32)
            m_ref[...] = m_new
            return carry

        lax.fori_loop(0, nk, step, 0)
        finish(acc_ref[...])


def _flash(qh, kt, v1, lq, q_off, lk, k_off, tq, tk):
    b, l = qh.shape[0], lq
    rep = ATT_HEADS // ATT_KV_HEADS
    kb = k_off // lk
    qb = q_off // tq
    return pl.pallas_call(
        functools.partial(_flash_body, tq=tq, tk=tk, nk=lk // tk),
        grid=(b, ATT_KV_HEADS, l // tq),
        in_specs=[pl.BlockSpec((1, rep, tq, LANES), lambda bi, g, i: (bi, g, i + qb, 0)),
                  pl.BlockSpec((1, 1, LANES, lk), lambda bi, g, i: (bi, g, 0, kb)),
                  pl.BlockSpec((1, 1, lk, LANES), lambda bi, g, i: (bi, g, kb, 0))],
        out_specs=pl.BlockSpec((1, tq, rep * HEAD_DIM), lambda bi, g, i: (bi, i, g)),
        out_shape=jax.ShapeDtypeStruct((b, l, ATT_HEADS * HEAD_DIM), BF16),
        scratch_shapes=[pltpu.VMEM((2, rep * tq, tk), F32),
                        pltpu.VMEM((rep * tq, LANES), F32), pltpu.VMEM((rep * tq, LANES), F32)],
        compiler_params=_cparams(("parallel", "parallel", "arbitrary")), name="flash_attention",
    )(qh, kt, v1)


def _halo_specs(tm, l, width, col_block):
    r = tm // HALO
    last = l // HALO - 1
    main = pl.BlockSpec((1, tm, width), lambda bi, i: (bi, i, col_block))
    prev = pl.BlockSpec((1, HALO, width), lambda bi, i: (bi, jnp.maximum(i * r - 1, 0), col_block))
    nxt = pl.BlockSpec((1, HALO, width), lambda bi, i: (bi, jnp.minimum((i + 1) * r, last), col_block))
    return [main, prev, nxt]


def _ssdprep_body(xs_ref, xsp_ref, xsn_ref, bc_ref, bcp_ref, bcn_ref, dt_ref,
                  xsc_ref, bcc_ref, dtc_ref, w_ref, b_ref, dtb_ref, ex_ref,
                  xo_ref, bco_ref, dtx_ref, ex_scr, eb_scr, *, nt_lat, tm):
    t = pl.program_id(1)
    is_ctx = t == nt_lat
    has_prev = (t > 0) & (t < nt_lat)
    has_next = t < nt_lat - 1

    def conv_silu(main_ref, prev_ref, next_ref, ctx_ref, scr, lo, hi):
        main = jnp.where(is_ctx, ctx_ref[0], main_ref[0])
        prev = jnp.where(has_prev, prev_ref[0], jnp.zeros_like(prev_ref[0]))
        nxt = jnp.where(has_next, next_ref[0], jnp.zeros_like(next_ref[0]))
        scr[...] = jnp.concatenate([prev, main, nxt], axis=0).astype(F32)
        w = w_ref[:, lo:hi]
        y = (w[0:1] * scr[pl.ds(HALO - 1, tm), :] + w[1:2] * scr[pl.ds(HALO, tm), :]
             + w[2:3] * scr[pl.ds(HALO + 1, tm), :]) + b_ref[:, lo:hi]
        return _silu(y)

    xo_ref[0] = conv_silu(xs_ref, xsp_ref, xsn_ref, xsc_ref, ex_scr, 0, 512).astype(BF16)
    bco_ref[0] = conv_silu(bc_ref, bcp_ref, bcn_ref, bcc_ref, eb_scr, 512, 768).astype(BF16)
    raw = jnp.where(is_ctx, dtc_ref[0], dt_ref[0]).astype(F32) + dtb_ref[...]
    dt = jnp.maximum(raw, 0.0) + jnp.log1p(jnp.exp(-jnp.abs(raw)))
    for d in range(2):
        dtx_ref[0, d] = _split_dot(dt, ex_ref[d])


def _seqprep_body(*refs, nt_lat, tm):
    n_kv_in, n_ssd_in = 8, 14
    ql_ref, qc_ref, gq_ref, negb_ref = refs[:4]
    kv_in, ssd_in = refs[4:4 + n_kv_in], refs[4 + n_kv_in:4 + n_kv_in + n_ssd_in]
    qo_ref, kt_ref, v1_ref, xo_ref, bco_ref, dtx_ref, ex_scr, eb_scr = refs[4 + n_kv_in + n_ssd_in:]
    cos_ref, sin_ref, bd_ref = kv_in[4], kv_in[5], kv_in[7]
    _qprep_body(ql_ref, qc_ref, cos_ref, sin_ref, gq_ref, negb_ref, bd_ref, qo_ref, nt_lat=nt_lat)
    _kvprep_body(*kv_in, kt_ref, v1_ref, nt_lat=nt_lat)
    _ssdprep_body(*ssd_in, xo_ref, bco_ref, dtx_ref, ex_scr, eb_scr, nt_lat=nt_lat, tm=tm)


def _seqprep(proj, proj_c, cos, sin, gain_q, neg_bound, gain, bd, conv_w, conv_b, dt_bias_pad, expand):
    b, l, _ = proj.shape
    lc = proj_c.shape[1]
    tm = lc
    nt_lat = l // tm
    ltot = l + lc
    r = tm // HALO
    last = l // HALO - 1
    cl = lambda t: jnp.minimum(t, nt_lat - 1)
    const2 = lambda bi, t: (0, 0)

    def lat_specs(width, col):
        return [pl.BlockSpec((1, tm, width), lambda bi, t: (bi, cl(t), col)),
                pl.BlockSpec((1, HALO, width), lambda bi, t: (bi, jnp.maximum(cl(t) * r - 1, 0), col)),
                pl.BlockSpec((1, HALO, width), lambda bi, t: (bi, jnp.minimum((cl(t) + 1) * r, last), col))]

    q_specs = [pl.BlockSpec((1, tm, 512), lambda bi, t: (bi, cl(t), C_Q // 512)),
               pl.BlockSpec((1, tm, 512), lambda bi, t: (bi, 0, C_Q // 512)),
               pl.BlockSpec((1, LANES), const2),
               pl.BlockSpec((1, LANES), const2)]
    kv_specs = [pl.BlockSpec((1, tm, LANES), lambda bi, t: (bi, cl(t), C_K // LANES)),
                pl.BlockSpec((1, tm, LANES), lambda bi, t: (bi, cl(t), C_V // LANES)),
                pl.BlockSpec((1, tm, LANES), lambda bi, t: (bi, 0, C_K // LANES)),
                pl.BlockSpec((1, tm, LANES), lambda bi, t: (bi, 0, C_V // LANES)),
                pl.BlockSpec((tm, LANES), lambda bi, t: (t, 0)),
                pl.BlockSpec((tm, LANES), lambda bi, t: (t, 0)),
                pl.BlockSpec((1, LANES), const2),
                pl.BlockSpec((LANES, LANES), const2)]
    ssd_specs = (lat_specs(512, C_XS // 512) + lat_specs(256, C_BC // 256)
                 + [pl.BlockSpec((1, tm, DT_PAD), lambda bi, t: (bi, cl(t), C_DT // DT_PAD)),
                    pl.BlockSpec((1, tm, 512), lambda bi, t: (bi, 0, C_XS // 512)),
                    pl.BlockSpec((1, tm, 256), lambda bi, t: (bi, 0, C_BC // 256)),
                    pl.BlockSpec((1, tm, DT_PAD), lambda bi, t: (bi, 0, C_DT // DT_PAD)),
                    pl.BlockSpec((3, 768), const2),
                    pl.BlockSpec((1, 768), const2),
                    pl.BlockSpec((1, DT_PAD), const2),
                    pl.BlockSpec((2, DT_PAD, 512), lambda bi, t: (0, 0, 0))])
    return pl.pallas_call(
        functools.partial(_seqprep_body, nt_lat=nt_lat, tm=tm), grid=(b, nt_lat + 1),
        in_specs=q_specs + kv_specs + ssd_specs,
        out_specs=[pl.BlockSpec((1, ATT_HEADS, tm, LANES), lambda bi, t: (bi, 0, t, 0)),
                   pl.BlockSpec((1, ATT_KV_HEADS, LANES, tm), lambda bi, t: (bi, 0, 0, t)),
                   pl.BlockSpec((1, ATT_KV_HEADS, tm, LANES), lambda bi, t: (bi, 0, t, 0)),
                   pl.BlockSpec((1, tm, 512), lambda bi, t: (bi, t, 0)),
                   pl.BlockSpec((1, tm, 256), lambda bi, t: (bi, t, 0)),
                   pl.BlockSpec((1, 2, tm, 512), lambda bi, t: (bi, 0, t, 0))],
        out_shape=[jax.ShapeDtypeStruct((b, ATT_HEADS, ltot, LANES), BF16),
                   jax.ShapeDtypeStruct((b, ATT_KV_HEADS, LANES, ltot), BF16),
                   jax.ShapeDtypeStruct((b, ATT_KV_HEADS, ltot, LANES), BF16),
                   jax.ShapeDtypeStruct((b, ltot, 512), BF16),
                   jax.ShapeDtypeStruct((b, ltot, 256), BF16),
                   jax.ShapeDtypeStruct((b, 2, ltot, 512), F32)],
        scratch_shapes=[pltpu.VMEM((tm + 2 * HALO, 512), F32), pltpu.VMEM((tm + 2 * HALO, 256), F32)],
        compiler_params=_cparams(("parallel", "parallel")), name="seq_prep",
    )(proj, proj_c, gain_q, neg_bound, proj, proj, proj_c, proj_c, cos, sin, gain, bd,
      proj, proj, proj, proj, proj, proj, proj, proj_c, proj_c, proj_c, conv_w, conv_b, dt_bias_pad, expand)


SSD_STEP_CHUNKS = 2


def _ssd_body(xsf_ref, bcf_ref, dtf_ref, xsb_ref, bcb_ref, dtb_ref, ar_ref, tri_ref, neg_ref, hm_ref,
              yf_ref, yb_ref, htf_ref, htb_ref):
    t = SSD_CHUNK
    nb = xsf_ref.shape[0]

    @pl.when(pl.program_id(0) == 0)
    def _():
        htf_ref[...] = jnp.zeros(htf_ref.shape, F32)
        htb_ref[...] = jnp.zeros(htb_ref.shape, F32)

    lane = lax.broadcasted_iota(jnp.int32, (t, LANES), 1)
    low = lane < HEAD_DIM

    def within_chunk(bi, d, xs_ref, bc_ref, dtx_ref, rows):
        xs = xs_ref[bi, rows, :].astype(F32)
        dtx = dtx_ref[bi, 0, rows, :]
        cum = _dot_split_lhs(tri_ref[d], dtx * ar_ref[d])
        tot = cum[t - 1:t, :] if d == 0 else cum[0:1, :]
        neg = neg_ref[d]
        xdt = xs * dtx
        xdt_b = xdt.astype(BF16)
        bc = bc_ref[bi, rows, :]
        b_all, c_all = bc[:, :LANES], bc[:, LANES:]
        cb = []
        for g in range(2):
            cg = jnp.where(low if g == 0 else ~low, c_all, jnp.zeros_like(c_all))
            cb.append(lax.dot_general(cg, b_all, (((1,), (1,)), ((), ())), preferred_element_type=F32))
        ys = []
        for j in range(SSD_HEADS // 2):
            cp = cum[:, j * LANES:(j + 1) * LANES]
            cps = pltpu.roll(cp, HEAD_DIM, 1)
            cpt = cp.T
            xp = xdt_b[:, j * LANES:(j + 1) * LANES]
            acc = None
            for e in range(2):
                col = jnp.where(low, cp, cps) if e == 0 else jnp.where(low, cps, cp)
                row = jnp.broadcast_to(cpt[e * HEAD_DIM:e * HEAD_DIM + 1, :], (t, t))
                w = (cb[j // 2] * jnp.exp2(col - row + neg)).astype(BF16)
                xm = jnp.where(low if e == 0 else ~low, xp, jnp.zeros_like(xp))
                part = jnp.dot(w, xm, preferred_element_type=F32)
                acc = part if acc is None else acc + part
            ys.append(acc)
        bt = b_all.astype(F32).T.astype(BF16)
        st = jnp.dot(bt, (xdt * jnp.exp2(tot - cum)).astype(BF16), preferred_element_type=F32)
        return jnp.concatenate(ys, axis=1), c_all, jnp.exp2(cum), jnp.exp2(tot), st * hm_ref[...]

    dirs = ((xsf_ref, bcf_ref, dtf_ref, yf_ref, htf_ref), (xsb_ref, bcb_ref, dtb_ref, yb_ref, htb_ref))
    local = []
    for bi in range(nb):
        for d, (xs_ref, bc_ref, dtx_ref, y_ref, ht_ref) in enumerate(dirs):
            order = range(SSD_STEP_CHUNKS) if d == 0 else range(SSD_STEP_CHUNKS - 1, -1, -1)
            rows = [pl.ds(o * t, t) for o in order]
            local.append((bi, y_ref, ht_ref, rows, [within_chunk(bi, d, xs_ref, bc_ref, dtx_ref, r) for r in rows]))
    for bi, y_ref, ht_ref, rows, chunks in local:
        ht = ht_ref[bi]
        for r, (y_diag, c_all, ecum, etot, st) in zip(rows, chunks):
            y_off = jnp.dot(c_all, ht.astype(BF16), preferred_element_type=F32) * ecum
            y_ref[bi, r, :] = (y_diag + y_off).astype(BF16)
            ht = ht * etot + st
        ht_ref[bi] = ht


def _ssd_scan(xs, bc, dtx, a_rate_x, tri, neg, hmask, l_lat):
    b, ltot, _ = xs.shape
    t = SSD_CHUNK
    ts = t * SSD_STEP_CHUNKS
    ns = ltot // ts
    nl = l_lat // ts
    nct = ns - nl

    fwd = lambda c: jnp.where(c < nct, nl + c, c - nct)
    bwd = lambda c: jnp.where(c < nct, nl + nct - 1 - c, nl - 1 - (c - nct))
    whole = lambda c: (0, 0, 0)

    def specs(sid, d):
        return [pl.BlockSpec((b, ts, 512), lambda c: (0, sid(c), 0)),
                pl.BlockSpec((b, ts, 256), lambda c: (0, sid(c), 0)),
                pl.BlockSpec((b, 1, ts, 512), lambda c: (0, d, sid(c), 0))]

    return pl.pallas_call(
        _ssd_body, grid=(ns,),
        in_specs=specs(fwd, 0) + specs(bwd, 1) + [
            pl.BlockSpec((2, 1, 512), whole), pl.BlockSpec((2, t, t), whole), pl.BlockSpec((2, t, t), whole),
            pl.BlockSpec((LANES, 512), lambda c: (0, 0))],
        out_specs=[pl.BlockSpec((b, ts, 512), lambda c: (0, fwd(c), 0)),
                   pl.BlockSpec((b, ts, 512), lambda c: (0, bwd(c), 0))],
        out_shape=[jax.ShapeDtypeStruct((b, ltot, 512), BF16), jax.ShapeDtypeStruct((b, ltot, 512), BF16)],
        scratch_shapes=[pltpu.VMEM((b, LANES, 512), F32), pltpu.VMEM((b, LANES, 512), F32)],
        compiler_params=_cparams(("arbitrary",)), name="ssd_scan",
    )(xs, bc, dtx, xs, bc, dtx, a_rate_x, tri, neg, hmask)


def _merge_body(gl_ref, bg_ref, cg_ref, cgp_ref, cgn_ref, u_ref, up_ref, un_ref,
                pu_ref, pup_ref, pun_ref, z_ref, att_ref, yf_ref, yb_ref, xs_ref, x_ref, mod_ref,
                cw_ref, band_ref, pw_ref, ps_ref, dsk_ref, sn_ref, wb_ref, wo_ref, o_ref, v_ref, *, tm, l):
    i = pl.program_id(1)
    has_prev = i > 0
    has_next = i < l // tm - 1

    def ext(main_ref, prev_ref, next_ref):
        prev = jnp.where(has_prev, prev_ref[0], jnp.zeros_like(prev_ref[0]))
        nxt = jnp.where(has_next, next_ref[0], jnp.zeros_like(next_ref[0]))
        return jnp.concatenate([prev, main_ref[0], nxt], axis=0)

    v_ref[...] = ext(cg_ref, cgp_ref, cgn_ref).astype(F32) * ext(u_ref, up_ref, un_ref).astype(F32)
    cw = cw_ref[...]
    conv = (cw[0:1] * v_ref[pl.ds(HALO - 1, tm), :] + cw[1:2] * v_ref[pl.ds(HALO, tm), :]
            + cw[2:3] * v_ref[pl.ds(HALO + 1, tm), :])
    br_conv = bg_ref[0].astype(F32) * conv

    pe = ext(pu_ref, pup_ref, pun_ref)
    posm = i * tm + lax.broadcasted_iota(jnp.int32, (tm, LANES), 0)
    pooled = []
    for g, win in enumerate(POOL_WINDOWS):
        cols = slice(g * LANES, (g + 1) * LANES)
        sums = jnp.concatenate(
            [jnp.dot(band_ref[g], pe[k * POOL_BLOCK:k * POOL_BLOCK + POOL_BLOCK + 2 * HALO, cols],
                     preferred_element_type=F32) for k in range(tm // POOL_BLOCK)], axis=0)
        cnt = (jnp.minimum(posm, win // 2) + jnp.minimum(l - posm, win // 2)).astype(F32)
        pm = (sums / cnt - pu_ref[0, :, cols].astype(F32)).astype(BF16)
        pooled.append(jnp.dot(pm, pw_ref[g], preferred_element_type=F32))
    br_pool = jnp.concatenate(pooled, axis=1) * ps_ref[...]

    y = yf_ref[0].astype(F32) + yb_ref[0].astype(F32) + dsk_ref[...] * xs_ref[0].astype(F32)
    gz = y * _silu(z_ref[0].astype(F32))
    br_ssd = gz * lax.rsqrt(jnp.mean(gz * gz, axis=-1, keepdims=True) + NORM_EPS) * sn_ref[...]

    branches = (att_ref[0], br_conv.astype(BF16), br_pool.astype(BF16), br_ssd.astype(BF16))
    acc = None
    for k, br in enumerate(branches):
        gate2 = jnp.tanh(gl_ref[0, :, k * D_MODEL:(k + 1) * D_MODEL].astype(F32)) + 1.0
        term = gate2 * jnp.dot(br, wb_ref[k], preferred_element_type=F32)
        acc = term if acc is None else acc + term
    yo = jnp.dot((0.5 * acc).astype(BF16), wo_ref[...], preferred_element_type=F32)
    o_ref[0] = x_ref[0] + mod_ref[0, 2:3, :] * yo


def _merge(proj, att, yf, yb, xs_c, x, mod, conv_short, band, pool_w, pool_scale, dskip_x, ssd_norm,
           w_branch, w_out, layer, tm, ssd_row_off):
    b, l, _ = x.shape
    per_batch = mod.shape[0] > 1
    off = ssd_row_off // tm
    const2 = lambda bi, i: (0, 0)
    const3 = lambda bi, i: (0, 0, 0)
    in_specs = (
        [pl.BlockSpec((1, tm, 4096), lambda bi, i: (bi, i, 0)),
         pl.BlockSpec((1, tm, 512), lambda bi, i: (bi, i, C_BG // 512))]
        + _halo_specs(tm, l, 512, C_CG // 512) + _halo_specs(tm, l, 512, C_U // 512)
        + _halo_specs(tm, l, 512, C_PU // 512)
        + [pl.BlockSpec((1, tm, 512), lambda bi, i: (bi, i, C_Z // 512)),
           pl.BlockSpec((1, tm, 512), lambda bi, i: (bi, i, 0)),
           pl.BlockSpec((1, tm, 512), lambda bi, i: (bi, i + off, 0)),
           pl.BlockSpec((1, tm, 512), lambda bi, i: (bi, i + off, 0)),
           pl.BlockSpec((1, tm, 512), lambda bi, i: (bi, i + off, 0)),
           pl.BlockSpec((1, tm, D_MODEL), lambda bi, i: (bi, i, 0)),
           pl.BlockSpec((1, 6, D_MODEL), lambda bi, i: (bi if per_batch else 0, 0, 0)),
           pl.BlockSpec((3, 512), const2),
           pl.BlockSpec((4, POOL_BLOCK, POOL_BLOCK + 2 * HALO), const3),
           pl.BlockSpec((None, 4, LANES, LANES), lambda bi, i: (layer, 0, 0, 0)),
           pl.BlockSpec((1, 512), const2),
           pl.BlockSpec((1, 512), const2),
           pl.BlockSpec((1, 512), const2),
           pl.BlockSpec((None, N_BRANCH, BRANCH_WIDTH, D_MODEL), lambda bi, i: (layer, 0, 0, 0)),
           pl.BlockSpec((None, D_MODEL, D_MODEL), lambda bi, i: (layer, 0, 0))])
    return pl.pallas_call(
        functools.partial(_merge_body, tm=tm, l=l), grid=(b, l // tm),
        in_specs=in_specs,
        out_specs=pl.BlockSpec((1, tm, D_MODEL), lambda bi, i: (bi, i, 0)),
        out_shape=jax.ShapeDtypeStruct((b, l, D_MODEL), F32),
        scratch_shapes=[pltpu.VMEM((tm + 2 * HALO, 512), F32)],
        compiler_params=_cparams(("parallel", "parallel")), name="merge",
    )(proj, proj, proj, proj, proj, proj, proj, proj, proj, proj, proj, proj,
      att, yf, yb, xs_c, x, mod, conv_short, band, pool_w, pool_scale, dskip_x, ssd_norm, w_branch, w_out)


FFN_HALO = 8
FFN_CHUNK = 256


def _ffn_body(x_ref, xp_ref, xn_ref, mod_ref, g_ref, wu_ref, cw_ref, wd_ref, fn_ref,
              o_ref, h_ref, ug_ref, uv_ref, a_ref, *, tm, l, final):
    i = pl.program_id(1)
    n_ext = tm + 2 * FFN_HALO
    pos = i * tm - FFN_HALO + lax.broadcasted_iota(jnp.int32, (n_ext, D_MODEL), 0)
    xe = jnp.concatenate([xp_ref[0], x_ref[0], xn_ref[0]], axis=0)
    h = _rms_mod(xe, g_ref[...], mod_ref[0, 4:5, :], mod_ref[0, 3:4, :])
    h_ref[...] = jnp.where((pos >= 0) & (pos < l), h, 0.0).astype(BF16)

    def conv3(u_ref, s, w):
        return (w[0:1] * u_ref[s, pl.ds(FFN_HALO - 1, tm), :] + w[1:2] * u_ref[s, pl.ds(FFN_HALO, tm), :]
                + w[2:3] * u_ref[s, pl.ds(FFN_HALO + 1, tm), :])

    for j in range(D_FF // FFN_CHUNK):
        s = j % 2
        cg = slice(j * FFN_CHUNK, (j + 1) * FFN_CHUNK)
        cv = slice(D_FF + j * FFN_CHUNK, D_FF + (j + 1) * FFN_CHUNK)
        ug_ref[s] = jnp.dot(h_ref[...], wu_ref[:, cg], preferred_element_type=F32)
        uv_ref[s] = jnp.dot(h_ref[...], wu_ref[:, cv], preferred_element_type=F32)
        hg = conv3(ug_ref, s, cw_ref[:, cg])
        a_ref[:, cg] = (hg * (jnp.tanh(hg) + 1.0) * conv3(uv_ref, s, cw_ref[:, cv])).astype(BF16)

    out = x_ref[0] + mod_ref[0, 5:6, :] * jnp.dot(a_ref[...], wd_ref[...], preferred_element_type=F32)
    if final:
        out = out * lax.rsqrt(jnp.mean(out * out, axis=-1, keepdims=True) + NORM_EPS) * fn_ref[...]
    o_ref[0] = out


def _ffn(x, mod, g, w_up, ffn_conv, w_down, final_norm, layer, tm, final):
    b, l, _ = x.shape
    per_batch = mod.shape[0] > 1
    r = tm // FFN_HALO
    last = l // FFN_HALO - 1
    n_ext = tm + 2 * FFN_HALO
    const = lambda bi, i: (0, 0)
    resident = pl.Buffered(1)
    return pl.pallas_call(
        functools.partial(_ffn_body, tm=tm, l=l, final=final), grid=(b, l // tm),
        in_specs=[pl.BlockSpec((1, tm, D_MODEL), lambda bi, i: (bi, i, 0)),
                  pl.BlockSpec((1, FFN_HALO, D_MODEL), lambda bi, i: (bi, jnp.maximum(i * r - 1, 0), 0)),
                  pl.BlockSpec((1, FFN_HALO, D_MODEL), lambda bi, i: (bi, jnp.minimum((i + 1) * r, last), 0)),
                  pl.BlockSpec((1, 6, D_MODEL), lambda bi, i: (bi if per_batch else 0, 0, 0)),
                  pl.BlockSpec((1, D_MODEL), const),
                  pl.BlockSpec((None, D_MODEL, 2 * D_FF), lambda bi, i: (layer, 0, 0), pipeline_mode=resident),
                  pl.BlockSpec((3, 2 * D_FF), const, pipeline_mode=resident),
                  pl.BlockSpec((None, D_FF, D_MODEL), lambda bi, i: (layer, 0, 0), pipeline_mode=resident),
                  pl.BlockSpec((1, D_MODEL), const)],
        out_specs=pl.BlockSpec((1, tm, D_MODEL), lambda bi, i: (bi, i, 0)),
        out_shape=jax.ShapeDtypeStruct((b, l, D_MODEL), F32),
        scratch_shapes=[pltpu.VMEM((n_ext, D_MODEL), BF16), pltpu.VMEM((2, n_ext, FFN_CHUNK), F32),
                        pltpu.VMEM((2, n_ext, FFN_CHUNK), F32), pltpu.VMEM((tm, D_FF), BF16)],
        compiler_params=_cparams(("parallel", "parallel")), name="conv_ffn",
    )(x, x, x, mod, g.reshape(1, D_MODEL), w_up, ffn_conv, w_down, final_norm.reshape(1, D_MODEL))


def _rope_tables(n_tokens):
    rows = n_tokens // GRID_W
    row = jnp.repeat(jnp.arange(rows), GRID_W).astype(F32)
    col = jnp.tile(jnp.arange(GRID_W), rows).astype(F32)
    inv = ROPE_THETA ** (-jnp.arange(ROPE_FREQS, dtype=F32) / ROPE_FREQS)
    ang_r, ang_c = row[:, None] * inv, col[:, None] * inv
    ang = jnp.concatenate([ang_r, ang_r, ang_c, ang_c], axis=1)
    sign = jnp.tile(jnp.concatenate([-jnp.ones(ROPE_FREQS), jnp.ones(ROPE_FREQS)]), 2).astype(F32)
    cos = jnp.tile(jnp.cos(ang), (1, 2))
    sin = jnp.tile(jnp.sin(ang) * sign, (1, 2))
    return cos, sin


def _identity_tables(n):
    return jnp.ones((n, LANES), F32), jnp.zeros((n, LANES), F32)


def _cat_weights(w_in):
    o = np.cumsum([0, 512, 128, 128, 512, 512, 512, 512, 512, 768, 16, 4096])
    wt = jnp.swapaxes(w_in, 1, 2)
    q, k, v, bg, cg, u, pu, z, xbc, dt, gl = [wt[:, o[i]:o[i + 1], :] for i in range(11)]
    dt = jnp.pad(dt, ((0, 0), (0, DT_PAD - dt.shape[1]), (0, 0)))
    gl = 0.5 * gl
    return jnp.concatenate([gl, bg, cg, u, pu, z, q, xbc[:, :512], xbc[:, 512:], k, v, dt],
                           axis=1).astype(BF16)


def kernel(x, c, ctx, c_ctx, w_mod, b_mod, norm_mix, w_in, q_norm, k_norm, conv_short, pool_w, pool_scale,
           ssd_conv_w, ssd_conv_b, ssd_dt_bias, ssd_a_log, ssd_d, ssd_norm, w_branch, w_out, norm_ffn,
           w_up, ffn_conv, w_down, final_norm):
    depth = w_mod.shape[0]
    b, l, _ = x.shape
    lc = ctx.shape[1]
    ltot = l + lc
    t = SSD_CHUNK

    tm_in = min(1024, l)
    tm_mid = min(512, l)
    tq = 512
    tk_lat = 768 if ltot % 768 == 0 else lc
    tk_ctx = lc

    cos_l, sin_l = _rope_tables(l)
    cos_c, sin_c = _identity_tables(lc)
    cos_all = jnp.concatenate([cos_l, cos_c], axis=0)
    sin_all = jnp.concatenate([sin_l, sin_c], axis=0)
    lane = np.arange(LANES)
    bd = jnp.asarray((lane[:, None] // HEAD_DIM) == (lane[None, :] // HEAD_DIM), BF16)
    col512 = np.arange(512)
    expand = np.zeros((2, DT_PAD, 512), np.float32)
    for d in range(2):
        expand[d, d * SSD_HEADS + col512 // HEAD_DIM, col512] = 1.0
    expand = jnp.asarray(expand, BF16)
    pos = np.arange(t)
    tri = jnp.asarray(np.stack([pos[:, None] >= pos[None, :], pos[:, None] <= pos[None, :]]), BF16)
    neg = jnp.asarray(np.where(np.stack([pos[:, None] >= pos[None, :], pos[:, None] <= pos[None, :]]),
                               0.0, -np.inf), F32)
    brow, bcol = np.arange(POOL_BLOCK)[:, None], np.arange(POOL_BLOCK + 2 * HALO)[None, :]
    band = jnp.asarray(np.stack([(bcol >= HALO + brow - w // 2) & (bcol < HALO + brow + w // 2)
                                 for w in POOL_WINDOWS]), BF16)
    hmask = jnp.asarray((lane[:, None] // SSD_STATE) == (col512[None, :] // 256), F32)

    cond8 = jnp.zeros((8, D_MODEL), F32).at[:b].set(c).at[b].set(c_ctx)
    mod_all = _modulation(cond8, w_mod, b_mod).reshape(depth, 8, 6, D_MODEL)
    w_cat = _cat_weights(w_in)
    wb, wo, pw = w_branch.astype(BF16), w_out.astype(BF16), pool_w.astype(BF16)
    gate_half = jnp.concatenate([jnp.full((D_FF,), 0.5, F32), jnp.ones((D_FF,), F32)])
    wu, wd = (w_up * gate_half).astype(BF16), w_down.astype(BF16)
    xc = ctx
    for i in range(depth):
        last = i == depth - 1
        mod_l, mod_c = mod_all[i, :b], mod_all[i, b:b + 1]
        gq = (jnp.tile(q_norm[i], 2) * QK_SCALE).reshape(1, LANES)
        gk = jnp.tile(k_norm[i], 2).reshape(1, LANES)
        neg_bound = jnp.full((1, LANES), -HEAD_DIM * QK_SCALE, F32) * (jnp.max(jnp.abs(q_norm[i]))
                                                                  * jnp.max(jnp.abs(k_norm[i])))
        a_rate_x = jnp.repeat(-jnp.exp(ssd_a_log[i].astype(F32)) * math.log2(math.e),
                              HEAD_DIM, axis=1).reshape(2, 1, 512)
        dskip_x = jnp.repeat(ssd_d[i], HEAD_DIM).reshape(1, 512)
        dt_bias_pad = jnp.pad(ssd_dt_bias[i].reshape(1, 2 * SSD_HEADS), ((0, 0), (0, DT_PAD - 2 * SSD_HEADS)))
        ps = pool_scale[i].reshape(1, 512)
        sn = ssd_norm[i].reshape(1, 512)
        scb = ssd_conv_b[i].reshape(1, 768)

        proj = _inproj(x, mod_l, norm_mix[i], w_cat, i, tm_in)
        proj_c = _inproj(xc, mod_c, norm_mix[i], w_cat, i, lc)

        qh, kt, v1, xs_s, bc_s, dtx = _seqprep(proj, proj_c, cos_all, sin_all, gq, neg_bound, gk, bd,
                                               ssd_conv_w[i], scb, dt_bias_pad, expand)
        att = _flash(qh, kt, v1, l, 0, ltot, 0, tq, tk_lat)

        yf, yb = _ssd_scan(xs_s, bc_s, dtx, a_rate_x, tri, neg, hmask, l)

        x_mid = _merge(proj, att, yf, yb, xs_s, x, mod_l, conv_short[i], band, pw, ps, dskip_x, sn, wb, wo, i, tm_mid, 0)
        if not last:
            att_c = _flash(qh, kt, v1, lc, l, lc, l, min(tq, lc), tk_ctx)
            xc_mid = _merge(proj_c, att_c, yf, yb, xs_s, xc, mod_c, conv_short[i], band, pw, ps, dskip_x, sn, wb, wo, i, lc, l)
            xc = _ffn(xc_mid, mod_c, norm_ffn[i], wu, ffn_conv[i], wd, final_norm, i, lc, False)
        x = _ffn(x_mid, mod_l, norm_ffn[i], wu, ffn_conv[i], wd, final_norm, i, tm_mid, last)
    return x
```
